```python
import jax, jax.numpy as jnp
from jax import lax
import numpy as np

D_MODEL = 2048
BATCH = 4
SEQ = 2048
DEPTH = 4

EPS = 1e-6
SSM_EXPAND = 2
SSM_D_INNER = SSM_EXPAND * D_MODEL
SSM_HEAD_DIM = 64
SSM_HEADS = SSM_D_INNER // SSM_HEAD_DIM
SSM_GROUPS = 8
SSM_STATE = 128
SSM_CONV = 4
SSM_CHUNK = 128
SSM_CONV_DIM = SSM_D_INNER + 2 * SSM_GROUPS * SSM_STATE
ATTN_HEADS = 16
ATTN_HEAD_DIM = 128
ATTN_KV_HEADS = 4
ATTN_WIDTH = ATTN_HEADS * ATTN_HEAD_DIM
ATTN_KV_WIDTH = ATTN_KV_HEADS * ATTN_HEAD_DIM
IDX_HEADS = 16
IDX_HEAD_DIM = 64
TOPK_MAX = 256
Q_BLOCK = 128
ROPE_THETA = 500000.0
ATTN_ROT_DIM = ATTN_HEAD_DIM // 4
IDX_ROT_DIM = IDX_HEAD_DIM // 4
FFN_HIDDEN = -(-8 * D_MODEL // (3 * 256)) * 256

IN_SPLITS = (
    SSM_D_INNER,
    SSM_CONV_DIM,
    SSM_HEADS,
    ATTN_WIDTH,
    ATTN_KV_WIDTH,
    ATTN_KV_WIDTH,
    IDX_HEADS * IDX_HEAD_DIM,
    IDX_HEAD_DIM,
    IDX_HEADS,
    D_MODEL,
    D_MODEL,
)
IN_COLS = int(sum(IN_SPLITS))
IN_OFFSETS = tuple(int(o) for o in np.cumsum(IN_SPLITS)[:-1])

kernel_name = "hybrid_ssd_dsa_gated_block"


def rmsnorm(x, g):
    xf = x.astype(jnp.float32)
    xf = xf * lax.rsqrt(jnp.mean(xf * xf, axis=-1, keepdims=True) + EPS)
    return xf.astype(x.dtype) * g


def rope_tables(positions, rot_dim):
    inv_freq = ROPE_THETA ** (-(jnp.arange(0, rot_dim, 2, dtype=jnp.float32) / rot_dim))
    ang = positions.astype(jnp.float32)[..., None] * inv_freq
    return jnp.cos(ang), jnp.sin(ang)


def apply_partial_rope(x, cos, sin):
    rot = 2 * cos.shape[-1]
    x1, x2, xp = x[..., : rot // 2], x[..., rot // 2: rot], x[..., rot:]
    c = cos[:, :, None, :].astype(x.dtype)
    s = sin[:, :, None, :].astype(x.dtype)
    return jnp.concatenate([x1 * c - x2 * s, x2 * c + x1 * s, xp], axis=-1)


def causal_dwconv(x, w, b):
    k = w.shape[0]
    xp = jnp.pad(x, ((0, 0), (k - 1, 0), (0, 0)))
    y = lax.conv_general_dilated(xp, w[:, None, :].astype(x.dtype), window_strides=(1,),
                                 padding="VALID", dimension_numbers=("NWC", "WIO", "NWC"),
                                 feature_group_count=x.shape[-1])
    return y + b


def segsum(a):
    cs = jnp.cumsum(a, axis=-1)
    t = a.shape[-1]
    diff = cs[..., :, None] - cs[..., None, :]
    mask = jnp.tril(jnp.ones((t, t), dtype=bool))
    return jnp.where(mask, diff, -jnp.inf)


def ssd_chunked(x, dt, a, bm, cm):
    b, s, h, p = x.shape
    g, n = bm.shape[2], bm.shape[3]
    r = h // g
    nc, lc = s // SSM_CHUNK, SSM_CHUNK
    xd = (x.astype(jnp.float32) * dt[..., None]).reshape(b, nc, lc, g, r, p)
    adt = (dt * a).reshape(b, nc, lc, g, r).transpose(0, 3, 4, 1, 2)
    bc = bm.astype(jnp.float32).reshape(b, nc, lc, g, n)
    cc = cm.astype(jnp.float32).reshape(b, nc, lc, g, n)
    a_cs = jnp.cumsum(adt, axis=-1)
    lmat = jnp.exp(segsum(adt))
    cb = jnp.einsum("bclgn,bcsgn->bcgls", cc, bc)
    y_diag = jnp.einsum("bcgls,bgrcls,bcsgrp->bclgrp", cb, lmat, xd)
    decay = jnp.exp(a_cs[..., -1:] - a_cs)
    states = jnp.einsum("bcsgn,bgrcs,bcsgrp->bcgrpn", bc, decay, xd)
    chunk_decay = jnp.exp(a_cs[..., -1])

    def step(hstate, inp):
        st, dec = inp
        return dec[..., None, None] * hstate + st, hstate

    h0 = jnp.zeros((b, g, r, p, n), jnp.float32)
    _, prev = lax.scan(step, h0, (jnp.moveaxis(states, 1, 0), jnp.moveaxis(chunk_decay, -1, 0)))
    prev = jnp.moveaxis(prev, 0, 1)
    y_off = jnp.einsum("bclgn,bcgrpn,bgrcl->bclgrp", cc, prev, jnp.exp(a_cs))
    return (y_diag + y_off).reshape(b, s, h, p)


def mamba_branch(z, xbc, dt_raw, conv_w, conv_b, dt_bias, a_log, d_skip, norm_g):
    b, s = z.shape[:2]
    xbc = jax.nn.silu(causal_dwconv(xbc, conv_w, conv_b))
    xs, bm, cm = jnp.split(xbc, [SSM_D_INNER, SSM_D_INNER + SSM_GROUPS * SSM_STATE], axis=-1)
    xs = xs.reshape(b, s, SSM_HEADS, SSM_HEAD_DIM)
    bm = bm.reshape(b, s, SSM_GROUPS, SSM_STATE)
    cm = cm.reshape(b, s, SSM_GROUPS, SSM_STATE)
    dt = jax.nn.softplus(dt_raw.astype(jnp.float32) + dt_bias.astype(jnp.float32))
    a = -jnp.exp(a_log.astype(jnp.float32))
    y = ssd_chunked(xs, dt, a, bm, cm) + d_skip.astype(jnp.float32)[:, None] * xs.astype(jnp.float32)
    y = y.reshape(b, s, SSM_D_INNER) * jax.nn.silu(z.astype(jnp.float32))
    yg = y.reshape(b, s, SSM_GROUPS, SSM_D_INNER // SSM_GROUPS)
    yg = yg * lax.rsqrt(jnp.mean(yg * yg, axis=-1, keepdims=True) + EPS)
    return yg.reshape(b, s, SSM_D_INNER).astype(z.dtype) * norm_g


def dsa_branch(q, k, v, q_idx, k_idx, w_idx, idx_k_norm, cos_a, sin_a, cos_i, sin_i):
    b, s = q.shape[:2]
    rep = ATTN_HEADS // ATTN_KV_HEADS
    q = apply_partial_rope(q.reshape(b, s, ATTN_HEADS, ATTN_HEAD_DIM), cos_a, sin_a)
    k = apply_partial_rope(k.reshape(b, s, ATTN_KV_HEADS, ATTN_HEAD_DIM), cos_a, sin_a)
    v = v.reshape(b, s, ATTN_KV_HEADS, ATTN_HEAD_DIM)
    q_idx = apply_partial_rope(q_idx.reshape(b, s, IDX_HEADS, IDX_HEAD_DIM), cos_i, sin_i)
    k_idx = apply_partial_rope(rmsnorm(k_idx, idx_k_norm)[:, :, None, :], cos_i, sin_i)[:, :, 0, :]
    w_idx = w_idx.astype(jnp.float32) * (IDX_HEADS ** -0.5 * IDX_HEAD_DIM ** -0.5)
    top_k = min(TOPK_MAX, s // 4)
    nb = s // Q_BLOCK
    q_blk = jnp.moveaxis(q.reshape(b, nb, Q_BLOCK, ATTN_KV_HEADS, rep, ATTN_HEAD_DIM), 1, 0)
    qi_blk = jnp.moveaxis(q_idx.reshape(b, nb, Q_BLOCK, IDX_HEADS, IDX_HEAD_DIM), 1, 0)
    w_blk = jnp.moveaxis(w_idx.reshape(b, nb, Q_BLOCK, IDX_HEADS), 1, 0)
    starts = jnp.arange(nb, dtype=jnp.int32) * Q_BLOCK
    key_pos = jnp.arange(s, dtype=jnp.int32)
    scale = ATTN_HEAD_DIM ** -0.5

    def block(args):
        qb, qib, wb, start = args
        qpos = start + jnp.arange(Q_BLOCK, dtype=jnp.int32)
        causal = key_pos[None, :] <= qpos[:, None]
        logits = jnp.einsum("bqhd,bsd->bqhs", qib, k_idx).astype(jnp.float32)
        score = jnp.einsum("bqhs,bqh->bqs", jax.nn.relu(logits), wb)
        score = jnp.where(causal[None], score, -jnp.inf)
        _, idx = lax.top_k(score, top_k)
        k_sel = jax.vmap(lambda kk, ii: kk[ii])(k, idx)
        v_sel = jax.vmap(lambda vv, ii: vv[ii])(v, idx)
        valid = idx <= qpos[None, :, None]
        sc = jnp.einsum("bqgrd,bqkgd->bqgrk", qb, k_sel).astype(jnp.float32) * scale
        sc = jnp.where(valid[:, :, None, None, :], sc, -jnp.inf)
        p = jax.nn.softmax(sc, axis=-1).astype(v.dtype)
        o = jnp.einsum("bqgrk,bqkgd->bqgrd", p, v_sel)
        return o.reshape(b, Q_BLOCK, ATTN_WIDTH)

    out = lax.map(block, (q_blk, qi_blk, w_blk, starts))
    return jnp.moveaxis(out, 0, 1).reshape(b, s, ATTN_WIDTH)


def setup_inputs(seed: int = 0) -> dict:
    key = jax.random.key(seed)
    ks = jax.random.split(key, 20)
    f32 = jnp.float32

    def nrm(k, shape, fan_in):
        return jax.random.normal(k, shape, f32) * (fan_in ** -0.5)

    def gain(k, shape):
        return 1.0 + 0.02 * jax.random.normal(k, shape, f32)

    x = jax.random.normal(ks[0], (BATCH, SEQ, D_MODEL), f32)
    start = jax.random.randint(ks[1], (BATCH, 1), 0, 4096, dtype=jnp.int32)
    positions = (start + jnp.arange(SEQ, dtype=jnp.int32)[None, :]).astype(jnp.int32)
    dt0 = jnp.exp(jax.random.uniform(ks[6], (DEPTH, SSM_HEADS), f32, np.log(1e-3), np.log(1e-1)))
    dt_bias = dt0 + jnp.log(-jnp.expm1(-dt0))
    return {
        "x": x,
        "positions": positions,
        "mix_norm": gain(ks[2], (DEPTH, D_MODEL)),
        "w_in": nrm(ks[3], (DEPTH, D_MODEL, IN_COLS), D_MODEL),
        "conv_w": nrm(ks[4], (DEPTH, SSM_CONV, SSM_CONV_DIM), SSM_CONV),
        "conv_b": 0.02 * jax.random.normal(ks[5], (DEPTH, SSM_CONV_DIM), f32),
        "dt_bias": dt_bias,
        "a_log": jnp.log(jax.random.uniform(ks[7], (DEPTH, SSM_HEADS), f32, 1.0, 16.0)),
        "d_skip": 1.0 + 0.1 * jax.random.normal(ks[8], (DEPTH, SSM_HEADS), f32),
        "ssm_norm": gain(ks[9], (DEPTH, SSM_D_INNER)),
        "idx_k_norm": gain(ks[10], (DEPTH, IDX_HEAD_DIM)),
        "w_proj_a": nrm(ks[11], (DEPTH, SSM_D_INNER, D_MODEL), SSM_D_INNER),
        "w_proj_b": nrm(ks[12], (DEPTH, ATTN_WIDTH, D_MODEL), ATTN_WIDTH),
        "w_out": nrm(ks[13], (DEPTH, D_MODEL, D_MODEL), D_MODEL),
        "ffn_norm": gain(ks[14], (DEPTH, D_MODEL)),
        "w_ffn_gate": nrm(ks[15], (DEPTH, D_MODEL, FFN_HIDDEN), D_MODEL),
        "w_ffn_up": nrm(ks[16], (DEPTH, D_MODEL, FFN_HIDDEN), D_MODEL),
        "w_ffn_down": nrm(ks[17], (DEPTH, FFN_HIDDEN, D_MODEL), FFN_HIDDEN),
        "final_norm": gain(ks[18], (D_MODEL,)),
    }


def reference(x, positions, mix_norm, w_in, conv_w, conv_b, dt_bias, a_log, d_skip, ssm_norm,
              idx_k_norm, w_proj_a, w_proj_b, w_out, ffn_norm, w_ffn_gate, w_ffn_up, w_ffn_down,
              final_norm):
    cos_a, sin_a = rope_tables(positions, ATTN_ROT_DIM)
    cos_i, sin_i = rope_tables(positions, IDX_ROT_DIM)
    for i in range(DEPTH):
        u = rmsnorm(x, mix_norm[i])
        proj = jnp.einsum("bsd,dc->bsc", u, w_in[i])
        (z, xbc, dt_raw, q, k, v, q_idx, k_idx, w_idx,
         g_a, g_b) = jnp.split(proj, IN_OFFSETS, axis=-1)
        y_a = mamba_branch(z, xbc, dt_raw, conv_w[i], conv_b[i], dt_bias[i], a_log[i],
                           d_skip[i], ssm_norm[i])
        y_b = dsa_branch(q, k, v, q_idx, k_idx, w_idx, idx_k_norm[i], cos_a, sin_a, cos_i, sin_i)
        merged = (jax.nn.sigmoid(g_a) * jnp.einsum("bse,ed->bsd", y_a, w_proj_a[i])
                  + jax.nn.sigmoid(g_b) * jnp.einsum("bse,ed->bsd", y_b, w_proj_b[i]))
        x = x + jnp.einsum("bsd,de->bse", merged, w_out[i])
        h = rmsnorm(x, ffn_norm[i])
        ff = jax.nn.silu(jnp.einsum("bsd,df->bsf", h, w_ffn_gate[i])) * jnp.einsum("bsd,df->bsf", h, w_ffn_up[i])
        x = x + jnp.einsum("bsf,fd->bsd", ff, w_ffn_down[i])
    return rmsnorm(x, final_norm)
```

```python
import functools

import jax
import jax.numpy as jnp
from jax import lax
from jax.experimental import pallas as pl
from jax.experimental.pallas import tpu as pltpu

F32 = jnp.float32
BF16 = jnp.bfloat16

D_MODEL = 2048
BATCH = 4
SEQ = 2048
DEPTH = 4
TOKENS = BATCH * SEQ
EPS = 1e-6

SSM_D_INNER = 4096
SSM_HEAD_DIM = 64
SSM_HEADS = 64
SSM_GROUPS = 8
SSM_HEADS_PER_GROUP = SSM_HEADS // SSM_GROUPS
SSM_GROUP_WIDTH = SSM_D_INNER // SSM_GROUPS
SSM_STATE = 128
SSM_CONV = 4
SSM_CHUNK = 128
SSM_CONV_DIM = SSM_D_INNER + 2 * SSM_GROUPS * SSM_STATE

ATTN_HEADS = 16
ATTN_HEAD_DIM = 128
ATTN_KV_HEADS = 4
ATTN_WIDTH = ATTN_HEADS * ATTN_HEAD_DIM
ATTN_KV_WIDTH = ATTN_KV_HEADS * ATTN_HEAD_DIM
IDX_HEADS = 16
IDX_HEAD_DIM = 64
IDX_WIDTH = IDX_HEADS * IDX_HEAD_DIM
TOP_K = min(256, SEQ // 4)
Q_BLOCK = 128
ROPE_THETA = 500000.0
ATTN_ROT_DIM = ATTN_HEAD_DIM // 4
IDX_ROT_DIM = IDX_HEAD_DIM // 4
FFN_HIDDEN = 5632

COL_Z = 0
COL_XBC = COL_Z + SSM_D_INNER
COL_Q = COL_XBC + SSM_CONV_DIM
COL_K = COL_Q + ATTN_WIDTH
COL_V = COL_K + ATTN_KV_WIDTH
COL_QI = COL_V + ATTN_KV_WIDTH
COL_GA = COL_QI + IDX_WIDTH
COL_GB = COL_GA + D_MODEL
MAIN_COLS = COL_GB + D_MODEL
SMALL_COLS = 256
SMALL_DT = 128
SMALL_W = SMALL_DT + SSM_HEADS

LANES = 128
VMEM_LIMIT_BYTES = 56 * 1024 * 1024

INT_MIN = -(2 ** 31)


def _params(*sem):
    return pltpu.CompilerParams(dimension_semantics=sem, vmem_limit_bytes=VMEM_LIMIT_BYTES)


def _dot(a, b):
    return jnp.dot(a, b, preferred_element_type=F32)


def _sigmoid(x):
    return 1.0 / (1.0 + jnp.exp(-x))


def _rmsnorm_kernel(x_ref, g_ref, o_ref):
    x = x_ref[...]
    ms = jnp.mean(x * x, axis=-1, keepdims=True)
    o_ref[...] = ((x * lax.rsqrt(ms + EPS)) * g_ref[...]).astype(o_ref.dtype)


def rmsnorm(x, g, out_dtype, name, tm=512):
    m, d = x.shape
    return pl.pallas_call(
        _rmsnorm_kernel,
        grid=(m // tm,),
        in_specs=[pl.BlockSpec((tm, d), lambda i: (i, 0)),
                  pl.BlockSpec((1, d), lambda i: (0, 0))],
        out_specs=pl.BlockSpec((tm, d), lambda i: (i, 0)),
        out_shape=jax.ShapeDtypeStruct((m, d), out_dtype),
        compiler_params=_params("parallel"),
        name=name,
    )(x, g.reshape(1, d))


def _mm_kernel(a_ref, w_ref, o_ref):
    o_ref[...] = _dot(a_ref[...], w_ref[...]).astype(o_ref.dtype)


def matmul(a, w, tm, tn, out_dtype, name):
    m, k = a.shape
    n = w.shape[1]
    return pl.pallas_call(
        _mm_kernel,
        grid=(n // tn, m // tm),
        in_specs=[pl.BlockSpec((tm, k), lambda j, i: (i, 0)),
                  pl.BlockSpec((k, tn), lambda j, i: (0, j))],
        out_specs=pl.BlockSpec((tm, tn), lambda j, i: (i, j)),
        out_shape=jax.ShapeDtypeStruct((m, n), out_dtype),
        compiler_params=_params("parallel", "parallel"),
        name=name,
    )(a, w)


def _mm_residual_kernel(a_ref, w_ref, r_ref, o_ref):
    o_ref[...] = r_ref[...] + _dot(a_ref[...], w_ref[...])


def matmul_residual(a, w, r, tm, tn, name):
    m, k = a.shape
    n = w.shape[1]
    return pl.pallas_call(
        _mm_residual_kernel,
        grid=(n // tn, m // tm),
        in_specs=[pl.BlockSpec((tm, k), lambda j, i: (i, 0)),
                  pl.BlockSpec((k, tn), lambda j, i: (0, j)),
                  pl.BlockSpec((tm, tn), lambda j, i: (i, j))],
        out_specs=pl.BlockSpec((tm, tn), lambda j, i: (i, j)),
        out_shape=jax.ShapeDtypeStruct((m, n), F32),
        compiler_params=_params("parallel", "parallel"),
        name=name,
    )(a, w, r)


def _ffn_up_kernel(a_ref, wg_ref, wu_ref, o_ref):
    a = a_ref[...]
    g = _dot(a, wg_ref[...])
    u = _dot(a, wu_ref[...])
    o_ref[...] = ((g * _sigmoid(g)) * u).astype(o_ref.dtype)


def ffn_up(a, wg, wu, tm, tn, name):
    m, k = a.shape
    n = wg.shape[1]
    return pl.pallas_call(
        _ffn_up_kernel,
        grid=(n // tn, m // tm),
        in_specs=[pl.BlockSpec((tm, k), lambda j, i: (i, 0)),
                  pl.BlockSpec((k, tn), lambda j, i: (0, j)),
                  pl.BlockSpec((k, tn), lambda j, i: (0, j))],
        out_specs=pl.BlockSpec((tm, tn), lambda j, i: (i, j)),
        out_shape=jax.ShapeDtypeStruct((m, n), BF16),
        compiler_params=_params("parallel", "parallel"),
        name=name,
    )(a, wg, wu)


def _merge_kernel(ya_ref, wa_ref, yb_ref, wb_ref, ga_ref, gb_ref, o_ref):
    pa = _dot(ya_ref[...], wa_ref[...])
    pb = _dot(yb_ref[...], wb_ref[...])
    o_ref[...] = (_sigmoid(ga_ref[...]) * pa + _sigmoid(gb_ref[...]) * pb).astype(o_ref.dtype)


def merge_branches(ya, wa, yb, wb, proj, tm, tn, name):
    m = ya.shape[0]
    n = wa.shape[1]
    ga0 = COL_GA // tn
    gb0 = COL_GB // tn
    return pl.pallas_call(
        _merge_kernel,
        grid=(n // tn, m // tm),
        in_specs=[pl.BlockSpec((tm, ya.shape[1]), lambda j, i: (i, 0)),
                  pl.BlockSpec((wa.shape[0], tn), lambda j, i: (0, j)),
                  pl.BlockSpec((tm, yb.shape[1]), lambda j, i: (i, 0)),
                  pl.BlockSpec((wb.shape[0], tn), lambda j, i: (0, j)),
                  pl.BlockSpec((tm, tn), lambda j, i: (i, ga0 + j)),
                  pl.BlockSpec((tm, tn), lambda j, i: (i, gb0 + j))],
        out_specs=pl.BlockSpec((tm, tn), lambda j, i: (i, j)),
        out_shape=jax.ShapeDtypeStruct((m, n), BF16),
        compiler_params=_params("parallel", "parallel"),
        name=name,
    )(ya, wa, yb, wb, proj, proj)


CONV_ROWS = 256
CONV_PAD = 8


def _conv_kernel(x_ref, w_ref, b_ref, o_ref, pad_ref):
    pad_ref[0:CONV_PAD, :] = jnp.zeros((CONV_PAD, pad_ref.shape[1]), F32)
    pad_ref[CONV_PAD:, :] = x_ref[...]
    w = w_ref[...]
    b = b_ref[...]
    for r0 in range(0, SEQ, CONV_ROWS):
        acc = b + w[SSM_CONV - 1:SSM_CONV, :] * pad_ref[CONV_PAD + r0:CONV_PAD + r0 + CONV_ROWS, :]
        for back in range(1, SSM_CONV):
            lo = CONV_PAD + r0 - back
            acc = acc + w[SSM_CONV - 1 - back:SSM_CONV - back, :] * pad_ref[lo:lo + CONV_ROWS, :]
        o_ref[r0:r0 + CONV_ROWS, :] = acc * _sigmoid(acc)


def causal_conv_silu(proj, conv_w, conv_b, name, tc=512):
    c0 = COL_XBC // tc
    return pl.pallas_call(
        _conv_kernel,
        grid=(BATCH, SSM_CONV_DIM // tc),
        in_specs=[pl.BlockSpec((SEQ, tc), lambda b, j: (b, c0 + j)),
                  pl.BlockSpec((SSM_CONV, tc), lambda b, j: (0, j)),
                  pl.BlockSpec((1, tc), lambda b, j: (0, j))],
        out_specs=pl.BlockSpec((SEQ, tc), lambda b, j: (b, j)),
        out_shape=jax.ShapeDtypeStruct((TOKENS, SSM_CONV_DIM), F32),
        scratch_shapes=[pltpu.VMEM((SEQ + CONV_PAD, tc), F32)],
        compiler_params=_params("parallel", "parallel"),
        name=name,
    )(proj, conv_w, conv_b.reshape(1, SSM_CONV_DIM))


def _softplus(x):
    return jnp.maximum(x, 0.0) + jnp.log1p(jnp.exp(-jnp.abs(x)))


def _split3(x):
    hi = x.astype(BF16)
    r1 = x - hi.astype(F32)
    mid = r1.astype(BF16)
    lo = (r1 - mid.astype(F32)).astype(BF16)
    return hi, mid, lo


def _ssd_kernel(xs_ref, b_ref, c_ref, z_ref, dtc_ref, dtr_ref, biasr_ref, biasc_ref,
                alogr_ref, alogc_ref, dsk_ref, ng_ref, o_ref, st_ref, y_ref):
    @pl.when(pl.program_id(2) == 0)
    def _():
        st_ref[...] = jnp.zeros_like(st_ref)

    lc = SSM_CHUNK
    hp = SSM_HEAD_DIM
    dtc = _softplus(dtc_ref[...] + biasr_ref[...])
    adt_c = dtc * (-jnp.exp(alogr_ref[...]))
    dtr = _softplus(dtr_ref[...] + biasc_ref[...])
    adt_r = dtr * (-jnp.exp(alogc_ref[...]))

    ii = lax.broadcasted_iota(jnp.int32, (lc, lc), 0)
    jj = lax.broadcasted_iota(jnp.int32, (lc, lc), 1)
    causal = jj <= ii
    tri = jnp.where(causal, 1.0, 0.0).astype(BF16)
    tri_t = jnp.where(ii <= jj, 1.0, 0.0).astype(BF16)
    cs_c = sum(_dot(tri, p) for p in _split3(adt_c))
    cs_r = sum(_dot(p, tri_t) for p in _split3(adt_r))
    cs_last = cs_c[lc - 1:lc, :]
    decay_c = jnp.exp(cs_last - cs_c)
    ecs_c = jnp.exp(cs_c)
    ecl = jnp.exp(cs_last)

    bc = b_ref[...]
    cb16 = c_ref[...].astype(BF16)
    b16 = bc.astype(BF16)
    cb = lax.dot_general(cb16, b16, (((1,), (1,)), ((), ())), preferred_element_type=F32)
    bt16 = bc.T.astype(BF16)
    xs = xs_ref[...]
    for h in range(SSM_HEADS_PER_GROUP):
        lo = h * hp
        xd = xs[:, lo:lo + hp] * dtc[:, h:h + 1]
        seg = cs_c[:, h:h + 1] - cs_r[h:h + 1, :]
        lmat = jnp.exp(jnp.where(causal, seg, -jnp.inf))
        y_diag = _dot((cb * lmat).astype(BF16), xd.astype(BF16))
        prev = st_ref[h]
        y_off = _dot(cb16, prev.astype(BF16)) * ecs_c[:, h:h + 1]
        st_new = _dot(bt16, (xd * decay_c[:, h:h + 1]).astype(BF16))
        st_ref[h] = prev * ecl[:, h:h + 1] + st_new
        y_ref[:, lo:lo + hp] = y_diag + y_off

    y = y_ref[...] + dsk_ref[...] * xs
    z = z_ref[...]
    y = y * (z * _sigmoid(z))
    ms = jnp.mean(y * y, axis=-1, keepdims=True)
    o_ref[...] = ((y * lax.rsqrt(ms + EPS)) * ng_ref[...]).astype(o_ref.dtype)


def ssd_branch(proj, xbc, dt_col, dt_row, dt_bias, a_log, d_skip, norm_g, name):
    nc = SEQ // SSM_CHUNK
    gw = SSM_GROUP_WIDTH
    hg = SSM_HEADS_PER_GROUP
    b_col0 = SSM_D_INNER // SSM_STATE
    c_col0 = b_col0 + SSM_GROUPS
    row = lambda b, g, c: b * nc + c
    return pl.pallas_call(
        _ssd_kernel,
        grid=(BATCH, SSM_GROUPS, nc),
        in_specs=[
            pl.BlockSpec((SSM_CHUNK, gw), lambda b, g, c: (row(b, g, c), g)),
            pl.BlockSpec((SSM_CHUNK, SSM_STATE), lambda b, g, c: (row(b, g, c), b_col0 + g)),
            pl.BlockSpec((SSM_CHUNK, SSM_STATE), lambda b, g, c: (row(b, g, c), c_col0 + g)),
            pl.BlockSpec((SSM_CHUNK, gw), lambda b, g, c: (row(b, g, c), COL_Z // gw + g)),
            pl.BlockSpec((None, None, SSM_CHUNK, hg), lambda b, g, c: (b, g, c, 0)),
            pl.BlockSpec((None, None, hg, SSM_CHUNK), lambda b, g, c: (b, g, 0, c)),
            pl.BlockSpec((None, 1, hg), lambda b, g, c: (g, 0, 0)),
            pl.BlockSpec((None, hg, 1), lambda b, g, c: (g, 0, 0)),
            pl.BlockSpec((None, 1, hg), lambda b, g, c: (g, 0, 0)),
            pl.BlockSpec((None, hg, 1), lambda b, g, c: (g, 0, 0)),
            pl.BlockSpec((1, gw), lambda b, g, c: (0, g)),
            pl.BlockSpec((1, gw), lambda b, g, c: (0, g)),
        ],
        out_specs=pl.BlockSpec((SSM_CHUNK, gw), lambda b, g, c: (row(b, g, c), g)),
        out_shape=jax.ShapeDtypeStruct((TOKENS, SSM_D_INNER), BF16),
        scratch_shapes=[pltpu.VMEM((hg, SSM_STATE, SSM_HEAD_DIM), F32),
                        pltpu.VMEM((SSM_CHUNK, gw), F32)],
        compiler_params=_params("parallel", "parallel", "arbitrary"),
        name=name,
    )(xbc, xbc, xbc, proj, dt_col, dt_row,
      dt_bias.reshape(SSM_GROUPS, 1, hg), dt_bias.reshape(SSM_GROUPS, hg, 1),
      a_log.reshape(SSM_GROUPS, 1, hg), a_log.reshape(SSM_GROUPS, hg, 1),
      jnp.repeat(d_skip, SSM_HEAD_DIM).reshape(1, SSM_D_INNER),
      norm_g.reshape(1, SSM_D_INNER))


PREP_ROWS = 256


def _rope(x, c, sn, sp, half):
    return x * c + pltpu.roll(x, LANES - half, 1) * sn + pltpu.roll(x, half, 1) * sp


def _prep_kernel(q_ref, k_ref, v_ref, qi_ref, sm_ref, ca_ref, sna_ref, spa_ref,
                 ci_ref, sni_ref, spi_ref, gk_ref,
                 qo_ref, kt_ref, vo_ref, qio_ref, kit_ref):
    ca, sna, spa = ca_ref[...], sna_ref[...], spa_ref[...]
    ci, sni, spi = ci_ref[...], sni_ref[...], spi_ref[...]
    ha = ATTN_ROT_DIM // 2
    hi = IDX_ROT_DIM // 2
    for h in range(ATTN_HEADS):
        lo = h * ATTN_HEAD_DIM
        qo_ref[:, lo:lo + ATTN_HEAD_DIM] = _rope(
            q_ref[:, lo:lo + ATTN_HEAD_DIM], ca, sna, spa, ha).astype(BF16)
    for g in range(ATTN_KV_HEADS):
        lo = g * ATTN_HEAD_DIM
        kr = _rope(k_ref[:, lo:lo + ATTN_HEAD_DIM], ca, sna, spa, ha)
        kt_ref[g] = kr.T.astype(BF16)
    vo_ref[...] = v_ref[...].astype(BF16)
    for j in range(IDX_WIDTH // LANES):
        lo = j * LANES
        qio_ref[:, lo:lo + LANES] = _rope(qi_ref[:, lo:lo + LANES], ci, sni, spi, hi).astype(BF16)
    xk = sm_ref[...]
    ms = jnp.sum(xk * xk, axis=-1, keepdims=True) * (1.0 / IDX_HEAD_DIM)
    xk = (xk * lax.rsqrt(ms + EPS)) * gk_ref[...]
    ki_a = _rope(xk, ci, sni, spi, hi)
    ki_b = pltpu.roll(ki_a, IDX_HEAD_DIM, 1)
    kit_ref[0] = ki_a.T.astype(BF16)
    kit_ref[1] = ki_b.T.astype(BF16)


def dsa_prep(proj, small, tabs, idx_k_norm, name):
    r = PREP_ROWS
    gk = jnp.concatenate([idx_k_norm, jnp.zeros((LANES - IDX_HEAD_DIM,), F32)]).reshape(1, LANES)
    tab_spec = pl.BlockSpec((r, LANES), lambda i: (i, 0))
    return pl.pallas_call(
        _prep_kernel,
        grid=(TOKENS // r,),
        in_specs=[
            pl.BlockSpec((r, ATTN_WIDTH), lambda i: (i, COL_Q // ATTN_WIDTH)),
            pl.BlockSpec((r, ATTN_KV_WIDTH), lambda i: (i, COL_K // ATTN_KV_WIDTH)),
            pl.BlockSpec((r, ATTN_KV_WIDTH), lambda i: (i, COL_V // ATTN_KV_WIDTH)),
            pl.BlockSpec((r, IDX_WIDTH), lambda i: (i, COL_QI // IDX_WIDTH)),
            pl.BlockSpec((r, LANES), lambda i: (i, 0)),
            tab_spec, tab_spec, tab_spec, tab_spec, tab_spec, tab_spec,
            pl.BlockSpec((1, LANES), lambda i: (0, 0)),
        ],
        out_specs=[
            pl.BlockSpec((r, ATTN_WIDTH), lambda i: (i, 0)),
            pl.BlockSpec((ATTN_KV_HEADS, ATTN_HEAD_DIM, r), lambda i: (0, 0, i)),
            pl.BlockSpec((r, ATTN_KV_WIDTH), lambda i: (i, 0)),
            pl.BlockSpec((r, IDX_WIDTH), lambda i: (i, 0)),
            pl.BlockSpec((2, LANES, r), lambda i: (0, 0, i)),
        ],
        out_shape=[
            jax.ShapeDtypeStruct((TOKENS, ATTN_WIDTH), BF16),
            jax.ShapeDtypeStruct((ATTN_KV_HEADS, ATTN_HEAD_DIM, TOKENS), BF16),
            jax.ShapeDtypeStruct((TOKENS, ATTN_KV_WIDTH), BF16),
            jax.ShapeDtypeStruct((TOKENS, IDX_WIDTH), BF16),
            jax.ShapeDtypeStruct((2, LANES, TOKENS), BF16),
        ],
        compiler_params=_params("parallel"),
        name=name,
    )(proj, proj, proj, proj, small, *tabs, gk)


def _dsa_kernel(q_ref, kt_ref, v_ref, qi_ref, kit_ref, sm_ref, o_ref, key_ref, madd_ref):
    tq = Q_BLOCK
    qb = pl.program_id(1)
    neg_inf = -jnp.inf

    w = sm_ref[...] * (IDX_HEADS ** -0.5 * IDX_HEAD_DIM ** -0.5)
    w0 = SMALL_W - SMALL_DT
    ki_a = kit_ref[0]
    ki_b = kit_ref[1]
    score = jnp.zeros((tq, SEQ), F32)
    for j in range(IDX_HEADS // 2):
        pair = qi_ref[:, j * LANES:(j + 1) * LANES]
        la = _dot(pair, ki_a)
        lb = _dot(pair, ki_b)
        score = score + jnp.maximum(la, 0.0) * w[:, w0 + 2 * j:w0 + 2 * j + 1]
        score = score + jnp.maximum(lb, 0.0) * w[:, w0 + 2 * j + 1:w0 + 2 * j + 2]
    score = jnp.where(score == 0.0, 0.0, score)

    row = qb * tq + lax.broadcasted_iota(jnp.int32, (tq, SEQ), 0)
    col = lax.broadcasted_iota(jnp.int32, (tq, SEQ), 1)
    causal = col <= row
    bits = lax.bitcast_convert_type(jnp.where(causal, score, neg_inf), jnp.int32)
    key_ref[...] = jnp.where(bits >= 0, bits, bits ^ jnp.int32(0x7FFFFFFF))

    def radix_step(i, t):
        cand = t | lax.shift_left(jnp.int32(1), 31 - i)
        cnt = jnp.sum(jnp.where(key_ref[...] >= (cand ^ jnp.int32(INT_MIN)), 1.0, 0.0),
                      axis=1, keepdims=True)
        return jnp.where(cnt >= float(TOP_K), cand, t)

    t_bits = lax.fori_loop(0, 32, radix_step, jnp.zeros((tq, 1), jnp.int32))
    thr = t_bits ^ jnp.int32(INT_MIN)

    key = key_ref[...]
    ge = key >= thr
    cnt_ge = jnp.sum(jnp.where(causal, jnp.where(ge, 1.0, 0.0), 0.0), axis=1, keepdims=True)
    madd_ref[...] = jnp.where(causal, jnp.where(ge, 0.0, neg_inf), neg_inf)

    @pl.when(jnp.max(cnt_ge) > float(TOP_K))
    def _():
        key2 = key_ref[...]
        gt = key2 > thr
        eq = key2 == thr
        need = float(TOP_K) - jnp.sum(jnp.where(gt, 1.0, 0.0), axis=1, keepdims=True)
        eqf = jnp.where(eq, 1.0, 0.0)
        a_i = lax.broadcasted_iota(jnp.int32, (LANES, LANES), 0)
        a_j = lax.broadcasted_iota(jnp.int32, (LANES, LANES), 1)
        before = jnp.where(a_i < a_j, 1.0, 0.0).astype(BF16)
        run = jnp.zeros((tq, 1), F32)
        for c in range(SEQ // LANES):
            sl = slice(c * LANES, (c + 1) * LANES)
            e = eqf[:, sl]
            rank = _dot(e.astype(BF16), before) + run
            keep_eq = jnp.where(rank < need, 0.0, neg_inf)
            val = jnp.where(gt[:, sl], 0.0, jnp.where(eq[:, sl], keep_eq, neg_inf))
            madd_ref[:, sl] = jnp.where(causal[:, sl], val, neg_inf)
            run = run + jnp.sum(e, axis=1, keepdims=True)

    scale = ATTN_HEAD_DIM ** -0.5
    rep = ATTN_HEADS // ATTN_KV_HEADS
    for h in range(ATTN_HEADS):
        g = h // rep
        lo = h * ATTN_HEAD_DIM
        s = _dot(q_ref[:, lo:lo + ATTN_HEAD_DIM], kt_ref[g]) * scale + madd_ref[...]
        m = jnp.max(s, axis=1, keepdims=True)
        p = jnp.exp(s - m)
        l = jnp.sum(p, axis=1, keepdims=True)
        o = _dot(p.astype(BF16), v_ref[:, g * ATTN_HEAD_DIM:(g + 1) * ATTN_HEAD_DIM])
        o_ref[:, lo:lo + ATTN_HEAD_DIM] = (o / l).astype(o_ref.dtype)


def dsa_attention(q_r, k_t, v_b, qi_r, ki_t, small, name):
    nq = SEQ // Q_BLOCK
    row = lambda b, i: b * nq + i
    return pl.pallas_call(
        _dsa_kernel,
        grid=(BATCH, nq),
        in_specs=[
            pl.BlockSpec((Q_BLOCK, ATTN_WIDTH), lambda b, i: (row(b, i), 0)),
            pl.BlockSpec((ATTN_KV_HEADS, ATTN_HEAD_DIM, SEQ), lambda b, i: (0, 0, b)),
            pl.BlockSpec((SEQ, ATTN_KV_WIDTH), lambda b, i: (b, 0)),
            pl.BlockSpec((Q_BLOCK, IDX_WIDTH), lambda b, i: (row(b, i), 0)),
            pl.BlockSpec((2, LANES, SEQ), lambda b, i: (0, 0, b)),
            pl.BlockSpec((Q_BLOCK, LANES), lambda b, i: (row(b, i), SMALL_DT // LANES)),
        ],
        out_specs=pl.BlockSpec((Q_BLOCK, ATTN_WIDTH), lambda b, i: (row(b, i), 0)),
        out_shape=jax.ShapeDtypeStruct((TOKENS, ATTN_WIDTH), BF16),
        scratch_shapes=[pltpu.VMEM((Q_BLOCK, SEQ), jnp.int32),
                        pltpu.VMEM((Q_BLOCK, SEQ), F32)],
        compiler_params=_params("parallel", "arbitrary"),
        name=name,
    )(q_r, k_t, v_b, qi_r, ki_t, small)


def _rope_lane_tables(positions):
    pos = positions.astype(F32).reshape(TOKENS, 1)

    def tables(rot_dim, width):
        inv_freq = ROPE_THETA ** (-(jnp.arange(0, rot_dim, 2, dtype=F32) / rot_dim))
        ang = pos * inv_freq
        cos, sin = jnp.cos(ang), jnp.sin(ang)
        half = rot_dim // 2
        ones = jnp.ones((TOKENS, width - rot_dim), F32)
        zeros = lambda n: jnp.zeros((TOKENS, n), F32)
        c = jnp.concatenate([cos, cos, ones], axis=1)
        sn = jnp.concatenate([-sin, zeros(width - half)], axis=1)
        sp = jnp.concatenate([zeros(half), sin, zeros(width - rot_dim)], axis=1)
        reps = LANES // width
        return tuple(jnp.tile(t, (1, reps)) for t in (c, sn, sp))

    return tables(ATTN_ROT_DIM, ATTN_HEAD_DIM) + tables(IDX_ROT_DIM, IDX_HEAD_DIM)


def _regroup_w_in(w):
    sizes = (SSM_D_INNER, SSM_CONV_DIM, SSM_HEADS, ATTN_WIDTH, ATTN_KV_WIDTH, ATTN_KV_WIDTH,
             IDX_WIDTH, IDX_HEAD_DIM, IDX_HEADS, D_MODEL, D_MODEL)
    parts = []
    off = 0
    for s in sizes:
        parts.append(w[:, off:off + s])
        off += s
    z, xbc, dt, q, k, v, qi, ki, wi, ga, gb = parts
    main = jnp.concatenate([z, xbc, q, k, v, qi, ga, gb], axis=1).astype(BF16)
    zc = lambda n: jnp.zeros((D_MODEL, n), F32)
    small = jnp.concatenate(
        [ki, zc(LANES - IDX_HEAD_DIM), dt, wi, zc(SMALL_COLS - SMALL_W - IDX_HEADS)],
        axis=1).astype(BF16)
    return main, small


def kernel(x, positions, mix_norm, w_in, conv_w, conv_b, dt_bias, a_log, d_skip, ssm_norm,
           idx_k_norm, w_proj_a, w_proj_b, w_out, ffn_norm, w_ffn_gate, w_ffn_up, w_ffn_down,
           final_norm):
    tabs = _rope_lane_tables(positions)
    xf = x.reshape(TOKENS, D_MODEL)
    for i in range(DEPTH):
        w_main, w_small = _regroup_w_in(w_in[i])
        u = rmsnorm(xf, mix_norm[i], BF16, f"mix_norm_{i}")
        proj = matmul(u, w_main, 1024, 1024, F32, f"in_proj_{i}")
        small = matmul(u, w_small, 1024, SMALL_COLS, F32, f"in_proj_small_{i}")

        xbc = causal_conv_silu(proj, conv_w[i], conv_b[i], f"conv_{i}")
        dt_raw = small[:, SMALL_DT:SMALL_W].reshape(BATCH, SEQ, SSM_GROUPS, SSM_HEADS_PER_GROUP)
        dt_col = jnp.transpose(dt_raw, (0, 2, 1, 3))
        dt_row = jnp.transpose(dt_raw, (0, 2, 3, 1))
        y_a = ssd_branch(proj, xbc, dt_col, dt_row, dt_bias[i], a_log[i], d_skip[i],
                         ssm_norm[i], f"ssd_{i}")

        q_r, k_t, v_b, qi_r, ki_t = dsa_prep(proj, small, tabs, idx_k_norm[i], f"dsa_prep_{i}")
        y_b = dsa_attention(q_r, k_t, v_b, qi_r, ki_t, small, f"dsa_{i}")

        merged = merge_branches(y_a, w_proj_a[i].astype(BF16), y_b, w_proj_b[i].astype(BF16),
                                proj, 512, 1024, f"merge_{i}")
        xf = matmul_residual(merged, w_out[i].astype(BF16), xf, 1024, 1024, f"out_proj_{i}")

        h = rmsnorm(xf, ffn_norm[i], BF16, f"ffn_norm_{i}")
        ff = ffn_up(h, w_ffn_gate[i].astype(BF16), w_ffn_up[i].astype(BF16), 1024, 512,
                    f"ffn_up_{i}")
        xf = matmul_residual(ff, w_ffn_down[i].astype(BF16), xf, 512, 1024, f"ffn_down_{i}")
    out = rmsnorm(xf, final_norm, F32, "final_norm")
    return out.reshape(BATCH, SEQ, D_MODEL)
```

```python
import jax
import jax.numpy as jnp
from jax import lax
from jax.experimental import pallas as pl
from jax.experimental.pallas import tpu as pltpu

F32 = jnp.float32
BF16 = jnp.bfloat16

D_MODEL = 2048
BATCH = 4
SEQ = 2048
DEPTH = 4
TOKENS = BATCH * SEQ
EPS = 1e-6

SSM_D_INNER = 4096
SSM_HEAD_DIM = 64
SSM_HEADS = 64
SSM_GROUPS = 8
SSM_HEADS_PER_GROUP = SSM_HEADS // SSM_GROUPS
SSM_GROUP_WIDTH = SSM_D_INNER // SSM_GROUPS
SSM_STATE = 128
SSM_CONV = 4
SSM_CHUNK = 128
SSM_CONV_DIM = SSM_D_INNER + 2 * SSM_GROUPS * SSM_STATE

ATTN_HEADS = 16
ATTN_HEAD_DIM = 128
ATTN_KV_HEADS = 4
ATTN_WIDTH = ATTN_HEADS * ATTN_HEAD_DIM
ATTN_KV_WIDTH = ATTN_KV_HEADS * ATTN_HEAD_DIM
IDX_HEADS = 16
IDX_HEAD_DIM = 64
IDX_WIDTH = IDX_HEADS * IDX_HEAD_DIM
TOP_K = min(256, SEQ // 4)
Q_BLOCK = 128
ROPE_THETA = 500000.0
ATTN_ROT_DIM = ATTN_HEAD_DIM // 4
IDX_ROT_DIM = IDX_HEAD_DIM // 4
FFN_HIDDEN = 5632

COL_Z = 0
COL_XS = COL_Z + SSM_D_INNER
COL_B = COL_XS + SSM_D_INNER
COL_C = COL_B + SSM_GROUPS * SSM_STATE
PROJ_A_COLS = COL_C + SSM_GROUPS * SSM_STATE
COL_Q = 0
COL_K = COL_Q + ATTN_WIDTH
COL_V = COL_K + ATTN_KV_WIDTH
COL_QI = COL_V + ATTN_KV_WIDTH
COL_GA = COL_QI + IDX_WIDTH
COL_GB = COL_GA + D_MODEL
PROJ_B_COLS = COL_GB + D_MODEL
SMALL_COLS = 256
SMALL_DT = 128
SMALL_W = SMALL_DT + SSM_HEADS

LANES = 128
SUBLANES = 8
VMEM_LIMIT_BYTES = 56 * 1024 * 1024

INT_MIN = -(2 ** 31)


def _params(*sem):
    return pltpu.CompilerParams(dimension_semantics=sem, vmem_limit_bytes=VMEM_LIMIT_BYTES)


def _dot(a, b):
    return jnp.dot(a, b, preferred_element_type=F32)


def _sigmoid(x):
    return 1.0 / (1.0 + jnp.exp(-x))


def _rmsnorm_kernel(x_ref, g_ref, o_ref):
    x = x_ref[...]
    ms = jnp.mean(x * x, axis=-1, keepdims=True)
    o_ref[...] = ((x * lax.rsqrt(ms + EPS)) * g_ref[...]).astype(o_ref.dtype)


def rmsnorm(x, g, out_dtype, name, tm=512):
    m, d = x.shape
    return pl.pallas_call(
        _rmsnorm_kernel,
        grid=(m // tm,),
        in_specs=[pl.BlockSpec((tm, d), lambda i: (i, 0)),
                  pl.BlockSpec((1, d), lambda i: (0, 0))],
        out_specs=pl.BlockSpec((tm, d), lambda i: (i, 0)),
        out_shape=jax.ShapeDtypeStruct((m, d), out_dtype),
        compiler_params=_params("parallel"),
        name=name,
    )(x, g.reshape(1, d))


def _cast_on_first_m_step(pairs):
    @pl.when(pl.program_id(1) == 0)
    def _():
        for w_ref, wb_ref in pairs:
            wb_ref[...] = w_ref[...].astype(BF16)


def _weight_spec(k, tn, layer):
    return pl.BlockSpec((None, k, tn), lambda j, i: (layer, 0, j))


def _mm_kernel(a_ref, w_ref, o_ref):
    o_ref[...] = _dot(a_ref[...], w_ref[...]).astype(o_ref.dtype)


def matmul_bf16w(a, w, tm, tn, out_dtype, name):
    m, k = a.shape
    n = w.shape[1]
    return pl.pallas_call(
        _mm_kernel,
        grid=(n // tn, m // tm),
        in_specs=[pl.BlockSpec((tm, k), lambda j, i: (i, 0)),
                  pl.BlockSpec((k, tn), lambda j, i: (0, j))],
        out_specs=pl.BlockSpec((tm, tn), lambda j, i: (i, j)),
        out_shape=jax.ShapeDtypeStruct((m, n), out_dtype),
        compiler_params=_params("parallel", "parallel"),
        name=name,
    )(a, w)


def _mm_w32_kernel(a_ref, w_ref, o_ref, wb_ref):
    _cast_on_first_m_step([(w_ref, wb_ref)])
    o_ref[...] = _dot(a_ref[...], wb_ref[...]).astype(o_ref.dtype)


def matmul_w32(a, w_stack, layer, n, tm, tn, out_dtype, name):
    m, k = a.shape
    return pl.pallas_call(
        _mm_w32_kernel,
        grid=(n // tn, m // tm),
        in_specs=[pl.BlockSpec((tm, k), lambda j, i: (i, 0)),
                  _weight_spec(k, tn, layer)],
        out_specs=pl.BlockSpec((tm, tn), lambda j, i: (i, j)),
        out_shape=jax.ShapeDtypeStruct((m, n), out_dtype),
        scratch_shapes=[pltpu.VMEM((k, tn), BF16)],
        compiler_params=_params("parallel", "arbitrary"),
        name=name,
    )(a, w_stack)


def _mm_residual_kernel(a_ref, w_ref, r_ref, o_ref, wb_ref):
    _cast_on_first_m_step([(w_ref, wb_ref)])
    o_ref[...] = r_ref[...] + _dot(a_ref[...], wb_ref[...])


def matmul_residual(a, w_stack, layer, r, tm, tn, name):
    m, k = a.shape
    n = w_stack.shape[2]
    return pl.pallas_call(
        _mm_residual_kernel,
        grid=(n // tn, m // tm),
        in_specs=[pl.BlockSpec((tm, k), lambda j, i: (i, 0)),
                  _weight_spec(k, tn, layer),
                  pl.BlockSpec((tm, tn), lambda j, i: (i, j))],
        out_specs=pl.BlockSpec((tm, tn), lambda j, i: (i, j)),
        out_shape=jax.ShapeDtypeStruct((m, n), F32),
        scratch_shapes=[pltpu.VMEM((k, tn), BF16)],
        compiler_params=_params("parallel", "arbitrary"),
        name=name,
    )(a, w_stack, r)


def _ffn_up_kernel(a_ref, wg_ref, wu_ref, o_ref, wgb_ref, wub_ref):
    _cast_on_first_m_step([(wg_ref, wgb_ref), (wu_ref, wub_ref)])
    a = a_ref[...]
    g = _dot(a, wgb_ref[...])
    u = _dot(a, wub_ref[...])
    o_ref[...] = ((g * _sigmoid(g)) * u).astype(o_ref.dtype)


def ffn_up(a, wg_stack, wu_stack, layer, tm, tn, name):
    m, k = a.shape
    n = wg_stack.shape[2]
    return pl.pallas_call(
        _ffn_up_kernel,
        grid=(n // tn, m // tm),
        in_specs=[pl.BlockSpec((tm, k), lambda j, i: (i, 0)),
                  _weight_spec(k, tn, layer),
                  _weight_spec(k, tn, layer)],
        out_specs=pl.BlockSpec((tm, tn), lambda j, i: (i, j)),
        out_shape=jax.ShapeDtypeStruct((m, n), BF16),
        scratch_shapes=[pltpu.VMEM((k, tn), BF16), pltpu.VMEM((k, tn), BF16)],
        compiler_params=_params("parallel", "arbitrary"),
        name=name,
    )(a, wg_stack, wu_stack)


def _merge_kernel(ya_ref, wa_ref, yb_ref, wb_ref, ga_ref, gb_ref, o_ref, wab_ref, wbb_ref):
    _cast_on_first_m_step([(wa_ref, wab_ref), (wb_ref, wbb_ref)])
    pa = _dot(ya_ref[...], wab_ref[...])
    pb = _dot(yb_ref[...], wbb_ref[...])
    o_ref[...] = (_sigmoid(ga_ref[...]) * pa + _sigmoid(gb_ref[...]) * pb).astype(o_ref.dtype)


def merge_branches(ya, wa_stack, yb, wb_stack, layer, proj_b, tm, tn, name):
    m = ya.shape[0]
    ka, kb = wa_stack.shape[1], wb_stack.shape[1]
    n = wa_stack.shape[2]
    ga0 = COL_GA // tn
    gb0 = COL_GB // tn
    return pl.pallas_call(
        _merge_kernel,
        grid=(n // tn, m // tm),
        in_specs=[pl.BlockSpec((tm, ka), lambda j, i: (i, 0)),
                  _weight_spec(ka, tn, layer),
                  pl.BlockSpec((tm, kb), lambda j, i: (i, 0)),
                  _weight_spec(kb, tn, layer),
                  pl.BlockSpec((tm, tn), lambda j, i: (i, ga0 + j)),
                  pl.BlockSpec((tm, tn), lambda j, i: (i, gb0 + j))],
        out_specs=pl.BlockSpec((tm, tn), lambda j, i: (i, j)),
        out_shape=jax.ShapeDtypeStruct((m, n), BF16),
        scratch_shapes=[pltpu.VMEM((ka, tn), BF16), pltpu.VMEM((kb, tn), BF16)],
        compiler_params=_params("parallel", "arbitrary"),
        name=name,
    )(ya, wa_stack, yb, wb_stack, proj_b, proj_b)


SSD_STEP_CHUNKS = 4
SSD_STEP_ROWS = SSD_STEP_CHUNKS * SSM_CHUNK
SSD_CONV_WIDTH = SSM_GROUP_WIDTH + 2 * SSM_STATE


def _softplus(x):
    return jnp.maximum(x, 0.0) + jnp.log1p(jnp.exp(-jnp.abs(x)))


def _split3(x):
    hi = x.astype(BF16)
    r1 = x - hi.astype(F32)
    mid = r1.astype(BF16)
    lo = (r1 - mid.astype(F32)).astype(BF16)
    return hi, mid, lo


def _ssd_kernel(x_ref, xh_ref, b_ref, bh_ref, c_ref, ch_ref, cw_ref, cbias_ref, z_ref,
                dtc_ref, dtr_ref, biasr_ref, biasc_ref, alogr_ref, alogc_ref, dsk_ref, ng_ref,
                o_ref, st_ref, pad_ref, xc_ref, y_ref):
    step = pl.program_id(2)

    @pl.when(step == 0)
    def _():
        st_ref[...] = jnp.zeros_like(st_ref)

    gw = SSM_GROUP_WIDTH
    n = SSM_STATE
    lc = SSM_CHUNK
    hp = SSM_HEAD_DIM
    halo = SUBLANES

    def halo_rows(h_ref):
        first = jnp.full(h_ref.shape, step, jnp.int32) == 0
        return jnp.where(first, 0.0, h_ref[...])

    pad_ref[0:halo, 0:gw] = halo_rows(xh_ref)
    pad_ref[0:halo, gw:gw + n] = halo_rows(bh_ref)
    pad_ref[0:halo, gw + n:] = halo_rows(ch_ref)
    pad_ref[halo:, 0:gw] = x_ref[...]
    pad_ref[halo:, gw:gw + n] = b_ref[...]
    pad_ref[halo:, gw + n:] = c_ref[...]
    cw = cw_ref[...]
    cbias = cbias_ref[...]
    for r0 in range(0, SSD_STEP_ROWS, lc):
        acc = cbias + cw[SSM_CONV - 1:SSM_CONV, :] * pad_ref[halo + r0:halo + r0 + lc, :]
        for back in range(1, SSM_CONV):
            lo = halo + r0 - back
            acc = acc + cw[SSM_CONV - 1 - back:SSM_CONV - back, :] * pad_ref[lo:lo + lc, :]
        xc_ref[r0:r0 + lc, :] = acc * _sigmoid(acc)

    dtc_all = _softplus(dtc_ref[...] + biasr_ref[...])
    adt_c_all = dtc_all * (-jnp.exp(alogr_ref[...]))
    dtr_all = _softplus(dtr_ref[...] + biasc_ref[...])
    adt_r_all = dtr_all * (-jnp.exp(alogc_ref[...]))

    ii = lax.broadcasted_iota(jnp.int32, (lc, lc), 0)
    jj = lax.broadcasted_iota(jnp.int32, (lc, lc), 1)
    causal = jj <= ii
    tri = jnp.where(causal, 1.0, 0.0).astype(BF16)
    tri_t = jnp.where(ii <= jj, 1.0, 0.0).astype(BF16)

    for cc in range(SSD_STEP_CHUNKS):
        r0 = cc * lc
        dtc = dtc_all[r0:r0 + lc, :]
        cs_c = sum(_dot(tri, p) for p in _split3(adt_c_all[r0:r0 + lc, :]))
        cs_r = sum(_dot(p, tri_t) for p in _split3(adt_r_all[:, r0:r0 + lc]))
        cs_last = cs_c[lc - 1:lc, :]
        decay_c = jnp.exp(cs_last - cs_c)
        ecs_c = jnp.exp(cs_c)
        ecl = jnp.exp(cs_last)

        xs = xc_ref[r0:r0 + lc, 0:gw]
        bc = xc_ref[r0:r0 + lc, gw:gw + n]
        c16 = xc_ref[r0:r0 + lc, gw + n:].astype(BF16)
        cb = lax.dot_general(c16, bc.astype(BF16), (((1,), (1,)), ((), ())),
                             preferred_element_type=F32)
        bt16 = bc.T.astype(BF16)
        for h in range(SSM_HEADS_PER_GROUP):
            lo = h * hp
            xd = xs[:, lo:lo + hp] * dtc[:, h:h + 1]
            seg = cs_c[:, h:h + 1] - cs_r[h:h + 1, :]
            lmat = jnp.exp(jnp.where(causal, seg, -jnp.inf))
            y_diag = _dot((cb * lmat).astype(BF16), xd.astype(BF16))
            prev = st_ref[h]
            y_off = _dot(c16, prev.astype(BF16)) * ecs_c[:, h:h + 1]
            st_new = _dot(bt16, (xd * decay_c[:, h:h + 1]).astype(BF16))
            st_ref[h] = prev * ecl[:, h:h + 1] + st_new
            y_ref[r0:r0 + lc, lo:lo + hp] = y_diag + y_off

    y = y_ref[...] + dsk_ref[...] * xc_ref[:, 0:gw]
    z = z_ref[...]
    y = y * (z * _sigmoid(z))
    ms = jnp.mean(y * y, axis=-1, keepdims=True)
    o_ref[...] = ((y * lax.rsqrt(ms + EPS)) * ng_ref[...]).astype(o_ref.dtype)


def ssd_branch(proj_a, dt_col, dt_row, conv_w, conv_b, dt_bias, a_log, d_skip, norm_g, name):
    rows = SSD_STEP_ROWS
    steps = SEQ // rows
    gw = SSM_GROUP_WIDTH
    n = SSM_STATE
    hg = SSM_HEADS_PER_GROUP
    ng = SSM_GROUPS
    cwid = SSD_CONV_WIDTH

    def per_group(p, lead):
        xs = p[:, :SSM_D_INNER].reshape(lead, ng, gw)
        bs = p[:, SSM_D_INNER:SSM_D_INNER + ng * n].reshape(lead, ng, n)
        cs = p[:, SSM_D_INNER + ng * n:].reshape(lead, ng, n)
        return jnp.transpose(jnp.concatenate([xs, bs, cs], axis=2), (1, 0, 2))

    row = lambda b, g, s: b * steps + s
    hrow = lambda b, g, s: jnp.maximum(row(b, g, s) * (rows // SUBLANES) - 1, 0)
    return pl.pallas_call(
        _ssd_kernel,
        grid=(BATCH, ng, steps),
        in_specs=[
            pl.BlockSpec((rows, gw), lambda b, g, s: (row(b, g, s), COL_XS // gw + g)),
            pl.BlockSpec((SUBLANES, gw), lambda b, g, s: (hrow(b, g, s), COL_XS // gw + g)),
            pl.BlockSpec((rows, n), lambda b, g, s: (row(b, g, s), COL_B // n + g)),
            pl.BlockSpec((SUBLANES, n), lambda b, g, s: (hrow(b, g, s), COL_B // n + g)),
            pl.BlockSpec((rows, n), lambda b, g, s: (row(b, g, s), COL_C // n + g)),
            pl.BlockSpec((SUBLANES, n), lambda b, g, s: (hrow(b, g, s), COL_C // n + g)),
            pl.BlockSpec((None, SSM_CONV, cwid), lambda b, g, s: (g, 0, 0)),
            pl.BlockSpec((None, 1, cwid), lambda b, g, s: (g, 0, 0)),
            pl.BlockSpec((rows, gw), lambda b, g, s: (row(b, g, s), COL_Z // gw + g)),
            pl.BlockSpec((None, None, rows, hg), lambda b, g, s: (b, g, s, 0)),
            pl.BlockSpec((None, None, hg, rows), lambda b, g, s: (b, g, 0, s)),
            pl.BlockSpec((None, 1, hg), lambda b, g, s: (g, 0, 0)),
            pl.BlockSpec((None, hg, 1), lambda b, g, s: (g, 0, 0)),
            pl.BlockSpec((None, 1, hg), lambda b, g, s: (g, 0, 0)),
            pl.BlockSpec((None, hg, 1), lambda b, g, s: (g, 0, 0)),
            pl.BlockSpec((1, gw), lambda b, g, s: (0, g)),
            pl.BlockSpec((1, gw), lambda b, g, s: (0, g)),
        ],
        out_specs=pl.BlockSpec((rows, gw), lambda b, g, s: (row(b, g, s), g)),
        out_shape=jax.ShapeDtypeStruct((TOKENS, SSM_D_INNER), BF16),
        scratch_shapes=[pltpu.VMEM((hg, n, SSM_HEAD_DIM), F32),
                        pltpu.VMEM((rows + SUBLANES, cwid), F32),
                        pltpu.VMEM((rows, cwid), F32),
                        pltpu.VMEM((rows, gw), F32)],
        compiler_params=_params("parallel", "parallel", "arbitrary"),
        name=name,
    )(proj_a, proj_a, proj_a, proj_a, proj_a, proj_a,
      per_group(conv_w, SSM_CONV), per_group(conv_b.reshape(1, SSM_CONV_DIM), 1),
      proj_a, dt_col, dt_row,
      dt_bias.reshape(ng, 1, hg), dt_bias.reshape(ng, hg, 1),
      a_log.reshape(ng, 1, hg), a_log.reshape(ng, hg, 1),
      jnp.repeat(d_skip, SSM_HEAD_DIM).reshape(1, SSM_D_INNER),
      norm_g.reshape(1, SSM_D_INNER))


PREP_ROWS = 256


def _rope(x, c, sn, sp, half):
    return x * c + pltpu.roll(x, LANES - half, 1) * sn + pltpu.roll(x, half, 1) * sp


def _prep_kernel(q_ref, k_ref, v_ref, qi_ref, sm_ref, ca_ref, sna_ref, spa_ref,
                 ci_ref, sni_ref, spi_ref, gk_ref,
                 qo_ref, kt_ref, vo_ref, qio_ref, kit_ref):
    ca, sna, spa = ca_ref[...], sna_ref[...], spa_ref[...]
    ci, sni, spi = ci_ref[...], sni_ref[...], spi_ref[...]
    ha = ATTN_ROT_DIM // 2
    hi = IDX_ROT_DIM // 2
    for h in range(ATTN_HEADS):
        lo = h * ATTN_HEAD_DIM
        qo_ref[:, lo:lo + ATTN_HEAD_DIM] = _rope(
            q_ref[:, lo:lo + ATTN_HEAD_DIM], ca, sna, spa, ha).astype(BF16)
    for g in range(ATTN_KV_HEADS):
        lo = g * ATTN_HEAD_DIM
        kr = _rope(k_ref[:, lo:lo + ATTN_HEAD_DIM], ca, sna, spa, ha)
        kt_ref[g] = kr.T.astype(BF16)
    vo_ref[...] = v_ref[...].astype(BF16)
    for j in range(IDX_WIDTH // LANES):
        lo = j * LANES
        qio_ref[:, lo:lo + LANES] = _rope(qi_ref[:, lo:lo + LANES], ci, sni, spi, hi).astype(BF16)
    xk = sm_ref[...]
    ms = jnp.sum(xk * xk, axis=-1, keepdims=True) * (1.0 / IDX_HEAD_DIM)
    xk = (xk * lax.rsqrt(ms + EPS)) * gk_ref[...]
    ki_a = _rope(xk, ci, sni, spi, hi)
    ki_b = pltpu.roll(ki_a, IDX_HEAD_DIM, 1)
    kit_ref[0] = ki_a.T.astype(BF16)
    kit_ref[1] = ki_b.T.astype(BF16)


def dsa_prep(proj_b, small, tabs, idx_k_norm, name):
    r = PREP_ROWS
    gk = jnp.concatenate([idx_k_norm, jnp.zeros((LANES - IDX_HEAD_DIM,), F32)]).reshape(1, LANES)
    tab_spec = pl.BlockSpec((r, LANES), lambda i: (i, 0))
    return pl.pallas_call(
        _prep_kernel,
        grid=(TOKENS // r,),
        in_specs=[
            pl.BlockSpec((r, ATTN_WIDTH), lambda i: (i, COL_Q // ATTN_WIDTH)),
            pl.BlockSpec((r, ATTN_KV_WIDTH), lambda i: (i, COL_K // ATTN_KV_WIDTH)),
            pl.BlockSpec((r, ATTN_KV_WIDTH), lambda i: (i, COL_V // ATTN_KV_WIDTH)),
            pl.BlockSpec((r, IDX_WIDTH), lambda i: (i, COL_QI // IDX_WIDTH)),
            pl.BlockSpec((r, LANES), lambda i: (i, 0)),
            tab_spec, tab_spec, tab_spec, tab_spec, tab_spec, tab_spec,
            pl.BlockSpec((1, LANES), lambda i: (0, 0)),
        ],
        out_specs=[
            pl.BlockSpec((r, ATTN_WIDTH), lambda i: (i, 0)),
            pl.BlockSpec((ATTN_KV_HEADS, ATTN_HEAD_DIM, r), lambda i: (0, 0, i)),
            pl.BlockSpec((r, ATTN_KV_WIDTH), lambda i: (i, 0)),
            pl.BlockSpec((r, IDX_WIDTH), lambda i: (i, 0)),
            pl.BlockSpec((2, LANES, r), lambda i: (0, 0, i)),
        ],
        out_shape=[
            jax.ShapeDtypeStruct((TOKENS, ATTN_WIDTH), BF16),
            jax.ShapeDtypeStruct((ATTN_KV_HEADS, ATTN_HEAD_DIM, TOKENS), BF16),
            jax.ShapeDtypeStruct((TOKENS, ATTN_KV_WIDTH), BF16),
            jax.ShapeDtypeStruct((TOKENS, IDX_WIDTH), BF16),
            jax.ShapeDtypeStruct((2, LANES, TOKENS), BF16),
        ],
        compiler_params=_params("parallel"),
        name=name,
    )(proj_b, proj_b, proj_b, proj_b, small, *tabs, gk)


def _dsa_kernel(q_ref, kt_ref, v_ref, qi_ref, kit_ref, sm_ref, o_ref, key_ref, madd_ref):
    tq = Q_BLOCK
    qb = pl.program_id(1)
    neg_inf = -jnp.inf

    w = sm_ref[...] * (IDX_HEADS ** -0.5 * IDX_HEAD_DIM ** -0.5)
    w0 = SMALL_W - SMALL_DT
    ki_a = kit_ref[0]
    ki_b = kit_ref[1]
    score = jnp.zeros((tq, SEQ), F32)
    for j in range(IDX_HEADS // 2):
        pair = qi_ref[:, j * LANES:(j + 1) * LANES]
        la = _dot(pair, ki_a)
        lb = _dot(pair, ki_b)
        score = score + jnp.maximum(la, 0.0) * w[:, w0 + 2 * j:w0 + 2 * j + 1]
        score = score + jnp.maximum(lb, 0.0) * w[:, w0 + 2 * j + 1:w0 + 2 * j + 2]
    score = jnp.where(score == 0.0, 0.0, score)

    row = qb * tq + lax.broadcasted_iota(jnp.int32, (tq, SEQ), 0)
    col = lax.broadcasted_iota(jnp.int32, (tq, SEQ), 1)
    causal = col <= row
    bits = lax.bitcast_convert_type(jnp.where(causal, score, neg_inf), jnp.int32)
    key_ref[...] = jnp.where(bits >= 0, bits, bits ^ jnp.int32(0x7FFFFFFF))

    def radix_step(i, t):
        cand = t | lax.shift_left(jnp.int32(1), 31 - i)
        cnt = jnp.sum(jnp.where(key_ref[...] >= (cand ^ jnp.int32(INT_MIN)), 1.0, 0.0),
                      axis=1, keepdims=True)
        return jnp.where(cnt >= float(TOP_K), cand, t)

    t_bits = lax.fori_loop(0, 32, radix_step, jnp.zeros((tq, 1), jnp.int32))
    thr = t_bits ^ jnp.int32(INT_MIN)

    key = key_ref[...]
    ge = key >= thr
    cnt_ge = jnp.sum(jnp.where(causal, jnp.where(ge, 1.0, 0.0), 0.0), axis=1, keepdims=True)
    madd_ref[...] = jnp.where(causal, jnp.where(ge, 0.0, neg_inf), neg_inf)

    @pl.when(jnp.max(cnt_ge) > float(TOP_K))
    def _():
        key2 = key_ref[...]
        gt = key2 > thr
        eq = key2 == thr
        need = float(TOP_K) - jnp.sum(jnp.where(gt, 1.0, 0.0), axis=1, keepdims=True)
        eqf = jnp.where(eq, 1.0, 0.0)
        a_i = lax.broadcasted_iota(jnp.int32, (LANES, LANES), 0)
        a_j = lax.broadcasted_iota(jnp.int32, (LANES, LANES), 1)
        before = jnp.where(a_i < a_j, 1.0, 0.0).astype(BF16)
        run = jnp.zeros((tq, 1), F32)
        for c in range(SEQ // LANES):
            sl = slice(c * LANES, (c + 1) * LANES)
            e = eqf[:, sl]
            rank = _dot(e.astype(BF16), before) + run
            keep_eq = jnp.where(rank < need, 0.0, neg_inf)
            val = jnp.where(gt[:, sl], 0.0, jnp.where(eq[:, sl], keep_eq, neg_inf))
            madd_ref[:, sl] = jnp.where(causal[:, sl], val, neg_inf)
            run = run + jnp.sum(e, axis=1, keepdims=True)

    scale = ATTN_HEAD_DIM ** -0.5
    rep = ATTN_HEADS // ATTN_KV_HEADS
    for h in range(ATTN_HEADS):
        g = h // rep
        lo = h * ATTN_HEAD_DIM
        s = _dot(q_ref[:, lo:lo + ATTN_HEAD_DIM], kt_ref[g]) * scale + madd_ref[...]
        m = jnp.max(s, axis=1, keepdims=True)
        p = jnp.exp(s - m)
        l = jnp.sum(p, axis=1, keepdims=True)
        o = _dot(p.astype(BF16), v_ref[:, g * ATTN_HEAD_DIM:(g + 1) * ATTN_HEAD_DIM])
        o_ref[:, lo:lo + ATTN_HEAD_DIM] = (o / l).astype(o_ref.dtype)


def dsa_attention(q_r, k_t, v_b, qi_r, ki_t, small, name):
    nq = SEQ // Q_BLOCK
    row = lambda b, i: b * nq + i
    return pl.pallas_call(
        _dsa_kernel,
        grid=(BATCH, nq),
        in_specs=[
            pl.BlockSpec((Q_BLOCK, ATTN_WIDTH), lambda b, i: (row(b, i), 0)),
            pl.BlockSpec((ATTN_KV_HEADS, ATTN_HEAD_DIM, SEQ), lambda b, i: (0, 0, b)),
            pl.BlockSpec((SEQ, ATTN_KV_WIDTH), lambda b, i: (b, 0)),
            pl.BlockSpec((Q_BLOCK, IDX_WIDTH), lambda b, i: (row(b, i), 0)),
            pl.BlockSpec((2, LANES, SEQ), lambda b, i: (0, 0, b)),
            pl.BlockSpec((Q_BLOCK, LANES), lambda b, i: (row(b, i), SMALL_DT // LANES)),
        ],
        out_specs=pl.BlockSpec((Q_BLOCK, ATTN_WIDTH), lambda b, i: (row(b, i), 0)),
        out_shape=jax.ShapeDtypeStruct((TOKENS, ATTN_WIDTH), BF16),
        scratch_shapes=[pltpu.VMEM((Q_BLOCK, SEQ), jnp.int32),
                        pltpu.VMEM((Q_BLOCK, SEQ), F32)],
        compiler_params=_params("parallel", "arbitrary"),
        name=name,
    )(q_r, k_t, v_b, qi_r, ki_t, small)


def _rope_lane_tables(positions):
    pos = positions.astype(F32).reshape(TOKENS, 1)

    def tables(rot_dim, width):
        inv_freq = ROPE_THETA ** (-(jnp.arange(0, rot_dim, 2, dtype=F32) / rot_dim))
        ang = pos * inv_freq
        cos, sin = jnp.cos(ang), jnp.sin(ang)
        half = rot_dim // 2
        ones = jnp.ones((TOKENS, width - rot_dim), F32)
        zeros = lambda n: jnp.zeros((TOKENS, n), F32)
        c = jnp.concatenate([cos, cos, ones], axis=1)
        sn = jnp.concatenate([-sin, zeros(width - half)], axis=1)
        sp = jnp.concatenate([zeros(half), sin, zeros(width - rot_dim)], axis=1)
        reps = LANES // width
        return tuple(jnp.tile(t, (1, reps)) for t in (c, sn, sp))

    return tables(ATTN_ROT_DIM, ATTN_HEAD_DIM) + tables(IDX_ROT_DIM, IDX_HEAD_DIM)


def _regroup_w_in_tail(w):
    sizes = (SSM_HEADS, ATTN_WIDTH, ATTN_KV_WIDTH, ATTN_KV_WIDTH, IDX_WIDTH, IDX_HEAD_DIM,
             IDX_HEADS, D_MODEL, D_MODEL)
    parts = []
    off = PROJ_A_COLS
    for s in sizes:
        parts.append(w[:, off:off + s])
        off += s
    dt, q, k, v, qi, ki, wi, ga, gb = parts
    main = jnp.concatenate([q, k, v, qi, ga, gb], axis=1).astype(BF16)
    zc = lambda n: jnp.zeros((D_MODEL, n), F32)
    small = jnp.concatenate(
        [ki, zc(LANES - IDX_HEAD_DIM), dt, wi, zc(SMALL_COLS - SMALL_W - IDX_HEADS)],
        axis=1).astype(BF16)
    return main, small


def kernel(x, positions, mix_norm, w_in, conv_w, conv_b, dt_bias, a_log, d_skip, ssm_norm,
           idx_k_norm, w_proj_a, w_proj_b, w_out, ffn_norm, w_ffn_gate, w_ffn_up, w_ffn_down,
           final_norm):
    tabs = _rope_lane_tables(positions)
    xf = x.reshape(TOKENS, D_MODEL)
    for i in range(DEPTH):
        w_tail, w_small = _regroup_w_in_tail(w_in[i])
        u = rmsnorm(xf, mix_norm[i], BF16, f"mix_norm_{i}")
        proj_a = matmul_w32(u, w_in, i, PROJ_A_COLS, 1024, 1024, F32, f"in_proj_a_{i}")
        proj_b = matmul_bf16w(u, w_tail, 1024, 1024, F32, f"in_proj_b_{i}")
        small = matmul_bf16w(u, w_small, 1024, SMALL_COLS, F32, f"in_proj_small_{i}")

        dt_raw = small[:, SMALL_DT:SMALL_W].reshape(BATCH, SEQ, SSM_GROUPS, SSM_HEADS_PER_GROUP)
        dt_col = jnp.transpose(dt_raw, (0, 2, 1, 3))
        dt_row = jnp.transpose(dt_raw, (0, 2, 3, 1))
        y_a = ssd_branch(proj_a, dt_col, dt_row, conv_w[i], conv_b[i], dt_bias[i], a_log[i],
                         d_skip[i], ssm_norm[i], f"ssd_{i}")

        q_r, k_t, v_b, qi_r, ki_t = dsa_prep(proj_b, small, tabs, idx_k_norm[i], f"dsa_prep_{i}")
        y_b = dsa_attention(q_r, k_t, v_b, qi_r, ki_t, small, f"dsa_{i}")

        merged = merge_branches(y_a, w_proj_a, y_b, w_proj_b, i, proj_b, 512, 512, f"merge_{i}")
        xf = matmul_residual(merged, w_out, i, xf, 1024, 1024, f"out_proj_{i}")

        h = rmsnorm(xf, ffn_norm[i], BF16, f"ffn_norm_{i}")
        ff = ffn_up(h, w_ffn_gate, w_ffn_up, i, 1024, 512, f"ffn_up_{i}")
        xf = matmul_residual(ff, w_ffn_down, i, xf, 512, 512, f"ffn_down_{i}")
    out = rmsnorm(xf, final_norm, F32, "final_norm")
    return out.reshape(BATCH, SEQ, D_MODEL)
```

```python
import math

import jax
import jax.numpy as jnp
from jax import lax
from jax.experimental import pallas as pl
from jax.experimental.pallas import tpu as pltpu

F32 = jnp.float32
BF16 = jnp.bfloat16

D_MODEL = 2048
BATCH = 4
SEQ = 2048
DEPTH = 4
TOKENS = BATCH * SEQ
EPS = 1e-6

SSM_D_INNER = 4096
SSM_HEAD_DIM = 64
SSM_HEADS = 64
SSM_GROUPS = 8
SSM_HEADS_PER_GROUP = SSM_HEADS // SSM_GROUPS
SSM_GROUP_WIDTH = SSM_D_INNER // SSM_GROUPS
SSM_STATE = 128
SSM_CONV = 4
SSM_CHUNK = 128
SSM_CONV_DIM = SSM_D_INNER + 2 * SSM_GROUPS * SSM_STATE

ATTN_HEADS = 16
ATTN_HEAD_DIM = 128
ATTN_KV_HEADS = 4
ATTN_WIDTH = ATTN_HEADS * ATTN_HEAD_DIM
ATTN_KV_WIDTH = ATTN_KV_HEADS * ATTN_HEAD_DIM
IDX_HEADS = 16
IDX_HEAD_DIM = 64
IDX_WIDTH = IDX_HEADS * IDX_HEAD_DIM
TOP_K = min(256, SEQ // 4)
Q_BLOCK = 128
ROPE_THETA = 500000.0
ATTN_ROT_DIM = ATTN_HEAD_DIM // 4
IDX_ROT_DIM = IDX_HEAD_DIM // 4
FFN_HIDDEN = 5632

COL_Z = 0
COL_XS = COL_Z + SSM_D_INNER
COL_B = COL_XS + SSM_D_INNER
COL_C = COL_B + SSM_GROUPS * SSM_STATE
COL_Q = COL_C + SSM_GROUPS * SSM_STATE
COL_K = COL_Q + ATTN_WIDTH
COL_V = COL_K + ATTN_KV_WIDTH
COL_QI = COL_V + ATTN_KV_WIDTH
COL_GA = COL_QI + IDX_WIDTH
COL_GB = COL_GA + D_MODEL
MAIN_COLS = COL_GB + D_MODEL
SMALL_COLS = 256
SMALL_DT = 128
SMALL_W = SMALL_DT + SSM_HEADS

LANES = 128
SUBLANES = 8
VMEM_LIMIT_BYTES = 56 * 1024 * 1024

INT_MIN = -(2 ** 31)


def _params(*sem):
    return pltpu.CompilerParams(dimension_semantics=sem, vmem_limit_bytes=VMEM_LIMIT_BYTES)


def _dot(a, b):
    return jnp.dot(a, b, preferred_element_type=F32)


def _sigmoid(x):
    return 1.0 / (1.0 + jnp.exp(-x))


def _rmsnorm_kernel(x_ref, g_ref, o_ref):
    x = x_ref[...]
    ms = jnp.mean(x * x, axis=-1, keepdims=True)
    o_ref[...] = ((x * lax.rsqrt(ms + EPS)) * g_ref[...]).astype(o_ref.dtype)


def rmsnorm(x, g, out_dtype, name, tm=512):
    m, d = x.shape
    return pl.pallas_call(
        _rmsnorm_kernel,
        grid=(m // tm,),
        in_specs=[pl.BlockSpec((tm, d), lambda i: (i, 0)),
                  pl.BlockSpec((1, d), lambda i: (0, 0))],
        out_specs=pl.BlockSpec((tm, d), lambda i: (i, 0)),
        out_shape=jax.ShapeDtypeStruct((m, d), out_dtype),
        compiler_params=_params("parallel"),
        name=name,
    )(x, g.reshape(1, d))


def _cast_on_first_m_step(pairs):
    @pl.when(pl.program_id(1) == 0)
    def _():
        for w_ref, wb_ref in pairs:
            wb_ref[...] = w_ref[...].astype(BF16)


def _weight_spec(k, tn, layer):
    return pl.BlockSpec((None, k, tn), lambda j, i: (layer, 0, j))


def _mm_kernel(a_ref, w_ref, o_ref):
    o_ref[...] = _dot(a_ref[...], w_ref[...]).astype(o_ref.dtype)


def matmul_bf16w(a, w, tm, tn, out_dtype, name):
    m, k = a.shape
    n = w.shape[1]
    return pl.pallas_call(
        _mm_kernel,
        grid=(n // tn, m // tm),
        in_specs=[pl.BlockSpec((tm, k), lambda j, i: (i, 0)),
                  pl.BlockSpec((k, tn), lambda j, i: (0, j))],
        out_specs=pl.BlockSpec((tm, tn), lambda j, i: (i, j)),
        out_shape=jax.ShapeDtypeStruct((m, n), out_dtype),
        compiler_params=_params("parallel", "parallel"),
        name=name,
    )(a, w)


def _mm_residual_kernel(a_ref, w_ref, r_ref, o_ref, wb_ref):
    _cast_on_first_m_step([(w_ref, wb_ref)])
    o_ref[...] = r_ref[...] + _dot(a_ref[...], wb_ref[...])


def matmul_residual(a, w_stack, layer, r, tm, tn, name):
    m, k = a.shape
    n = w_stack.shape[2]
    return pl.pallas_call(
        _mm_residual_kernel,
        grid=(n // tn, m // tm),
        in_specs=[pl.BlockSpec((tm, k), lambda j, i: (i, 0)),
                  _weight_spec(k, tn, layer),
                  pl.BlockSpec((tm, tn), lambda j, i: (i, j))],
        out_specs=pl.BlockSpec((tm, tn), lambda j, i: (i, j)),
        out_shape=jax.ShapeDtypeStruct((m, n), F32),
        scratch_shapes=[pltpu.VMEM((k, tn), BF16)],
        compiler_params=_params("parallel", "arbitrary"),
        name=name,
    )(a, w_stack, r)


def _ffn_up_kernel(a_ref, wg_ref, wu_ref, o_ref, wgb_ref, wub_ref):
    _cast_on_first_m_step([(wg_ref, wgb_ref), (wu_ref, wub_ref)])
    a = a_ref[...]
    g = _dot(a, wgb_ref[...])
    u = _dot(a, wub_ref[...])
    o_ref[...] = ((g * _sigmoid(g)) * u).astype(o_ref.dtype)


def ffn_up(a, wg_stack, wu_stack, layer, tm, tn, name):
    m, k = a.shape
    n = wg_stack.shape[2]
    return pl.pallas_call(
        _ffn_up_kernel,
        grid=(n // tn, m // tm),
        in_specs=[pl.BlockSpec((tm, k), lambda j, i: (i, 0)),
                  _weight_spec(k, tn, layer),
                  _weight_spec(k, tn, layer)],
        out_specs=pl.BlockSpec((tm, tn), lambda j, i: (i, j)),
        out_shape=jax.ShapeDtypeStruct((m, n), BF16),
        scratch_shapes=[pltpu.VMEM((k, tn), BF16), pltpu.VMEM((k, tn), BF16)],
        compiler_params=_params("parallel", "arbitrary"),
        name=name,
    )(a, wg_stack, wu_stack)


def _merge_kernel(ya_ref, wa_ref, yb_ref, wb_ref, ga_ref, gb_ref, o_ref, wab_ref, wbb_ref):
    _cast_on_first_m_step([(wa_ref, wab_ref), (wb_ref, wbb_ref)])
    pa = _dot(ya_ref[...], wab_ref[...])
    pb = _dot(yb_ref[...], wbb_ref[...])
    o_ref[...] = (_sigmoid(ga_ref[...]) * pa + _sigmoid(gb_ref[...]) * pb).astype(o_ref.dtype)


def merge_branches(ya, wa_stack, yb, wb_stack, layer, proj, tm, tn, name):
    m = ya.shape[0]
    ka, kb = wa_stack.shape[1], wb_stack.shape[1]
    n = wa_stack.shape[2]
    ga0 = COL_GA // tn
    gb0 = COL_GB // tn
    return pl.pallas_call(
        _merge_kernel,
        grid=(n // tn, m // tm),
        in_specs=[pl.BlockSpec((tm, ka), lambda j, i: (i, 0)),
                  _weight_spec(ka, tn, layer),
                  pl.BlockSpec((tm, kb), lambda j, i: (i, 0)),
                  _weight_spec(kb, tn, layer),
                  pl.BlockSpec((tm, tn), lambda j, i: (i, ga0 + j)),
                  pl.BlockSpec((tm, tn), lambda j, i: (i, gb0 + j))],
        out_specs=pl.BlockSpec((tm, tn), lambda j, i: (i, j)),
        out_shape=jax.ShapeDtypeStruct((m, n), BF16),
        scratch_shapes=[pltpu.VMEM((ka, tn), BF16), pltpu.VMEM((kb, tn), BF16)],
        compiler_params=_params("parallel", "arbitrary"),
        name=name,
    )(ya, wa_stack, yb, wb_stack, proj, proj)


SSD_STEP_CHUNKS = 4
SSD_STEP_ROWS = SSD_STEP_CHUNKS * SSM_CHUNK
SSD_CONV_WIDTH = SSM_GROUP_WIDTH + 2 * SSM_STATE


def _softplus(x):
    return jnp.maximum(x, 0.0) + jnp.log1p(jnp.exp(-jnp.abs(x)))


def _split3(x):
    hi = x.astype(BF16)
    r1 = x - hi.astype(F32)
    mid = r1.astype(BF16)
    lo = (r1 - mid.astype(F32)).astype(BF16)
    return hi, mid, lo


def _ssd_kernel(x_ref, xh_ref, b_ref, bh_ref, c_ref, ch_ref, cw_ref, cbias_ref, z_ref,
                dtc_ref, dtr_ref, biasr_ref, biasc_ref, alogr_ref, alogc_ref, dsk_ref, ng_ref,
                o_ref, st_ref, pad_ref, xc_ref, y_ref):
    step = pl.program_id(2)

    @pl.when(step == 0)
    def _():
        st_ref[...] = jnp.zeros_like(st_ref)

    gw = SSM_GROUP_WIDTH
    n = SSM_STATE
    lc = SSM_CHUNK
    hp = SSM_HEAD_DIM
    halo = SUBLANES

    def halo_rows(h_ref):
        first = jnp.full(h_ref.shape, step, jnp.int32) == 0
        return jnp.where(first, 0.0, h_ref[...])

    pad_ref[0:halo, 0:gw] = halo_rows(xh_ref)
    pad_ref[0:halo, gw:gw + n] = halo_rows(bh_ref)
    pad_ref[0:halo, gw + n:] = halo_rows(ch_ref)
    pad_ref[halo:, 0:gw] = x_ref[...]
    pad_ref[halo:, gw:gw + n] = b_ref[...]
    pad_ref[halo:, gw + n:] = c_ref[...]
    cw = cw_ref[...]
    cbias = cbias_ref[...]
    for r0 in range(0, SSD_STEP_ROWS, lc):
        acc = cbias + cw[SSM_CONV - 1:SSM_CONV, :] * pad_ref[halo + r0:halo + r0 + lc, :]
        for back in range(1, SSM_CONV):
            lo = halo + r0 - back
            acc = acc + cw[SSM_CONV - 1 - back:SSM_CONV - back, :] * pad_ref[lo:lo + lc, :]
        xc_ref[r0:r0 + lc, :] = acc * _sigmoid(acc)

    dtc_all = _softplus(dtc_ref[...] + biasr_ref[...])
    adt_c_all = dtc_all * (-jnp.exp(alogr_ref[...]))
    dtr_all = _softplus(dtr_ref[...] + biasc_ref[...])
    adt_r_all = dtr_all * (-jnp.exp(alogc_ref[...]))

    ii = lax.broadcasted_iota(jnp.int32, (lc, lc), 0)
    jj = lax.broadcasted_iota(jnp.int32, (lc, lc), 1)
    causal = jj <= ii
    tri = jnp.where(causal, 1.0, 0.0).astype(BF16)
    tri_t = jnp.where(ii <= jj, 1.0, 0.0).astype(BF16)

    for cc in range(SSD_STEP_CHUNKS):
        r0 = cc * lc
        dtc = dtc_all[r0:r0 + lc, :]
        cs_c = sum(_dot(tri, p) for p in _split3(adt_c_all[r0:r0 + lc, :]))
        cs_r = sum(_dot(p, tri_t) for p in _split3(adt_r_all[:, r0:r0 + lc]))
        cs_last = cs_c[lc - 1:lc, :]
        decay_c = jnp.exp(cs_last - cs_c)
        ecs_c = jnp.exp(cs_c)
        ecl = jnp.exp(cs_last)

        xs = xc_ref[r0:r0 + lc, 0:gw]
        bc = xc_ref[r0:r0 + lc, gw:gw + n]
        c16 = xc_ref[r0:r0 + lc, gw + n:].astype(BF16)
        cb = lax.dot_general(c16, bc.astype(BF16), (((1,), (1,)), ((), ())),
                             preferred_element_type=F32)
        bt16 = bc.T.astype(BF16)
        for h in range(SSM_HEADS_PER_GROUP):
            lo = h * hp
            xd = xs[:, lo:lo + hp] * dtc[:, h:h + 1]
            seg = cs_c[:, h:h + 1] - cs_r[h:h + 1, :]
            lmat = jnp.exp(jnp.where(causal, seg, -jnp.inf))
            y_diag = _dot((cb * lmat).astype(BF16), xd.astype(BF16))
            prev = st_ref[h]
            y_off = _dot(c16, prev.astype(BF16)) * ecs_c[:, h:h + 1]
            st_new = _dot(bt16, (xd * decay_c[:, h:h + 1]).astype(BF16))
            st_ref[h] = prev * ecl[:, h:h + 1] + st_new
            y_ref[r0:r0 + lc, lo:lo + hp] = y_diag + y_off

    y = y_ref[...] + dsk_ref[...] * xc_ref[:, 0:gw]
    z = z_ref[...]
    y = y * (z * _sigmoid(z))
    ms = jnp.mean(y * y, axis=-1, keepdims=True)
    o_ref[...] = ((y * lax.rsqrt(ms + EPS)) * ng_ref[...]).astype(o_ref.dtype)


def ssd_branch(proj, dt_col, dt_row, conv_w, conv_b, dt_bias, a_log, d_skip, norm_g, name):
    rows = SSD_STEP_ROWS
    steps = SEQ // rows
    gw = SSM_GROUP_WIDTH
    n = SSM_STATE
    hg = SSM_HEADS_PER_GROUP
    ng = SSM_GROUPS
    cwid = SSD_CONV_WIDTH

    def per_group(p, lead):
        xs = p[:, :SSM_D_INNER].reshape(lead, ng, gw)
        bs = p[:, SSM_D_INNER:SSM_D_INNER + ng * n].reshape(lead, ng, n)
        cs = p[:, SSM_D_INNER + ng * n:].reshape(lead, ng, n)
        return jnp.transpose(jnp.concatenate([xs, bs, cs], axis=2), (1, 0, 2))

    row = lambda b, g, s: b * steps + s
    hrow = lambda b, g, s: jnp.maximum(row(b, g, s) * (rows // SUBLANES) - 1, 0)
    return pl.pallas_call(
        _ssd_kernel,
        grid=(BATCH, ng, steps),
        in_specs=[
            pl.BlockSpec((rows, gw), lambda b, g, s: (row(b, g, s), COL_XS // gw + g)),
            pl.BlockSpec((SUBLANES, gw), lambda b, g, s: (hrow(b, g, s), COL_XS // gw + g)),
            pl.BlockSpec((rows, n), lambda b, g, s: (row(b, g, s), COL_B // n + g)),
            pl.BlockSpec((SUBLANES, n), lambda b, g, s: (hrow(b, g, s), COL_B // n + g)),
            pl.BlockSpec((rows, n), lambda b, g, s: (row(b, g, s), COL_C // n + g)),
            pl.BlockSpec((SUBLANES, n), lambda b, g, s: (hrow(b, g, s), COL_C // n + g)),
            pl.BlockSpec((None, SSM_CONV, cwid), lambda b, g, s: (g, 0, 0)),
            pl.BlockSpec((None, 1, cwid), lambda b, g, s: (g, 0, 0)),
            pl.BlockSpec((rows, gw), lambda b, g, s: (row(b, g, s), COL_Z // gw + g)),
            pl.BlockSpec((None, None, rows, hg), lambda b, g, s: (b, g, s, 0)),
            pl.BlockSpec((hg, rows), lambda b, g, s: (g, row(b, g, s))),
            pl.BlockSpec((None, 1, hg), lambda b, g, s: (g, 0, 0)),
            pl.BlockSpec((None, hg, 1), lambda b, g, s: (g, 0, 0)),
            pl.BlockSpec((None, 1, hg), lambda b, g, s: (g, 0, 0)),
            pl.BlockSpec((None, hg, 1), lambda b, g, s: (g, 0, 0)),
            pl.BlockSpec((1, gw), lambda b, g, s: (0, g)),
            pl.BlockSpec((1, gw), lambda b, g, s: (0, g)),
        ],
        out_specs=pl.BlockSpec((rows, gw), lambda b, g, s: (row(b, g, s), g)),
        out_shape=jax.ShapeDtypeStruct((TOKENS, SSM_D_INNER), BF16),
        scratch_shapes=[pltpu.VMEM((hg, n, SSM_HEAD_DIM), F32),
                        pltpu.VMEM((rows + SUBLANES, cwid), F32),
                        pltpu.VMEM((rows, cwid), F32),
                        pltpu.VMEM((rows, gw), F32)],
        compiler_params=_params("parallel", "parallel", "arbitrary"),
        name=name,
    )(proj, proj, proj, proj, proj, proj,
      per_group(conv_w, SSM_CONV), per_group(conv_b.reshape(1, SSM_CONV_DIM), 1),
      proj, dt_col, dt_row,
      dt_bias.reshape(ng, 1, hg), dt_bias.reshape(ng, hg, 1),
      a_log.reshape(ng, 1, hg), a_log.reshape(ng, hg, 1),
      jnp.repeat(d_skip, SSM_HEAD_DIM).reshape(1, SSM_D_INNER),
      norm_g.reshape(1, SSM_D_INNER))


KEY_CHUNK = 256
PREP_ROWS = KEY_CHUNK
PREP_QBLOCKS = PREP_ROWS // Q_BLOCK
N_KEY_CHUNKS = SEQ // KEY_CHUNK
HEAD_PAIRS = ATTN_HEADS // 2
IDX_PAIRS = IDX_HEADS // 2
Q_PRESCALE = ATTN_HEAD_DIM ** -0.5 * math.log2(math.e)
W_IDX_SCALE = IDX_HEADS ** -0.5 * IDX_HEAD_DIM ** -0.5
M_INIT = -1e30


def _rope(x, c, sn, sp, half):
    return x * c + pltpu.roll(x, LANES - half, 1) * sn + pltpu.roll(x, half, 1) * sp


def _prep_kernel(q_ref, k_ref, v_ref, qi_ref, smk_ref, smw_ref, ca_ref, sna_ref, spa_ref,
                 ci_ref, sni_ref, spi_ref, gk_ref,
                 qt_ref, ko_ref, vt_ref, qit_ref, kio_ref, wt_ref, dtr_ref):
    ca, sna, spa = ca_ref[...], sna_ref[...], spa_ref[...]
    ci, sni, spi = ci_ref[...], sni_ref[...], spi_ref[...]
    ha = ATTN_ROT_DIM // 2
    hi = IDX_ROT_DIM // 2
    hd = ATTN_HEAD_DIM
    zeros_half = jnp.zeros((LANES - IDX_HEAD_DIM, Q_BLOCK), F32)
    for qq in range(PREP_QBLOCKS):
        rows = slice(qq * Q_BLOCK, (qq + 1) * Q_BLOCK)
        for h in range(ATTN_HEADS):
            qr = _rope(q_ref[rows, h * hd:(h + 1) * hd], ca[rows], sna[rows], spa[rows], ha)
            qt_ref[qq, h // 2, :, (h % 2) * Q_BLOCK:(h % 2 + 1) * Q_BLOCK] = (
                (qr * Q_PRESCALE).T.astype(BF16))
        for j in range(IDX_PAIRS):
            xt = _rope(qi_ref[rows, j * LANES:(j + 1) * LANES],
                       ci[rows], sni[rows], spi[rows], hi).T
            even = jnp.concatenate([xt[:IDX_HEAD_DIM], zeros_half], axis=0)
            odd = jnp.concatenate([xt[IDX_HEAD_DIM:], zeros_half], axis=0)
            qit_ref[qq, j, :, 0:Q_BLOCK] = even.astype(BF16)
            qit_ref[qq, j, :, Q_BLOCK:] = odd.astype(BF16)
    for g in range(ATTN_KV_HEADS):
        cols = slice(g * hd, (g + 1) * hd)
        ko_ref[:, cols] = _rope(k_ref[:, cols], ca, sna, spa, ha).astype(BF16)
        vt_ref[g, 0] = v_ref[:, cols].T.astype(BF16)
    xk = smk_ref[...]
    ms = jnp.sum(xk * xk, axis=-1, keepdims=True) * (1.0 / IDX_HEAD_DIM)
    xk = (xk * lax.rsqrt(ms + EPS)) * gk_ref[...]
    kio_ref[...] = _rope(xk, ci, sni, spi, hi).astype(BF16)
    smt = smw_ref[...].T
    dtr_ref[...] = smt[:SSM_HEADS, :]
    w0 = SMALL_W - SMALL_DT
    for qq in range(PREP_QBLOCKS):
        wt_ref[qq] = smt[w0:w0 + IDX_HEADS, qq * Q_BLOCK:(qq + 1) * Q_BLOCK] * W_IDX_SCALE


def dsa_prep(proj, small, tabs, idx_k_norm, name):
    r = PREP_ROWS
    nqb = PREP_QBLOCKS
    gk = jnp.concatenate([idx_k_norm, jnp.zeros((LANES - IDX_HEAD_DIM,), F32)]).reshape(1, LANES)
    tab_spec = pl.BlockSpec((r, LANES), lambda i: (i, 0))
    return pl.pallas_call(
        _prep_kernel,
        grid=(TOKENS // r,),
        in_specs=[
            pl.BlockSpec((r, ATTN_WIDTH), lambda i: (i, COL_Q // ATTN_WIDTH)),
            pl.BlockSpec((r, ATTN_KV_WIDTH), lambda i: (i, COL_K // ATTN_KV_WIDTH)),
            pl.BlockSpec((r, ATTN_KV_WIDTH), lambda i: (i, COL_V // ATTN_KV_WIDTH)),
            pl.BlockSpec((r, IDX_WIDTH), lambda i: (i, COL_QI // IDX_WIDTH)),
            pl.BlockSpec((r, LANES), lambda i: (i, 0)),
            pl.BlockSpec((r, LANES), lambda i: (i, SMALL_DT // LANES)),
            tab_spec, tab_spec, tab_spec, tab_spec, tab_spec, tab_spec,
            pl.BlockSpec((1, LANES), lambda i: (0, 0)),
        ],
        out_specs=[
            pl.BlockSpec((nqb, HEAD_PAIRS, ATTN_HEAD_DIM, 2 * Q_BLOCK), lambda i: (i, 0, 0, 0)),
            pl.BlockSpec((r, ATTN_KV_WIDTH), lambda i: (i, 0)),
            pl.BlockSpec((ATTN_KV_HEADS, 1, ATTN_HEAD_DIM, r), lambda i: (0, i, 0, 0)),
            pl.BlockSpec((nqb, IDX_PAIRS, LANES, 2 * Q_BLOCK), lambda i: (i, 0, 0, 0)),
            pl.BlockSpec((r, LANES), lambda i: (i, 0)),
            pl.BlockSpec((nqb, IDX_HEADS, Q_BLOCK), lambda i: (i, 0, 0)),
            pl.BlockSpec((SSM_HEADS, r), lambda i: (0, i)),
        ],
        out_shape=[
            jax.ShapeDtypeStruct((TOKENS // Q_BLOCK, HEAD_PAIRS, ATTN_HEAD_DIM, 2 * Q_BLOCK), BF16),
            jax.ShapeDtypeStruct((TOKENS, ATTN_KV_WIDTH), BF16),
            jax.ShapeDtypeStruct((ATTN_KV_HEADS, TOKENS // r, ATTN_HEAD_DIM, r), BF16),
            jax.ShapeDtypeStruct((TOKENS // Q_BLOCK, IDX_PAIRS, LANES, 2 * Q_BLOCK), BF16),
            jax.ShapeDtypeStruct((TOKENS, LANES), BF16),
            jax.ShapeDtypeStruct((TOKENS // Q_BLOCK, IDX_HEADS, Q_BLOCK), F32),
            jax.ShapeDtypeStruct((SSM_HEADS, TOKENS), F32),
        ],
        compiler_params=_params("parallel"),
        name=name,
    )(proj, proj, proj, proj, small, small, *tabs, gk)


def _col_reduce8(x, op):
    rows, lanes = x.shape
    g = x.reshape(4, rows // (4 * SUBLANES), SUBLANES, lanes)
    p = op(g, axis=1)
    red = jnp.maximum if op is jnp.max else jnp.add
    return red(red(p[0], p[1]), red(p[2], p[3]))


def _dsa_kernel(qt_ref, k_ref, vt_ref, qit_ref, ki_ref, wt_ref, o_ref,
                key_ref, madd_ref, acc_ref, m_ref, l_ref, s_ref, pe_ref):
    tq = Q_BLOCK
    kc = KEY_CHUNK
    hd = ATTN_HEAD_DIM
    qb = pl.program_id(1)
    n_chunks = qb // (kc // tq) + 1
    neg_inf = -jnp.inf
    q_pos = qb * tq + lax.broadcasted_iota(jnp.int32, (kc, tq), 1)
    k_off = lax.broadcasted_iota(jnp.int32, (kc, tq), 0)

    def causal(c):
        return c * kc + k_off <= q_pos

    def key_rows(c):
        return pl.ds(pl.multiple_of(c * kc, kc), kc)

    wt = wt_ref[...]

    def score_chunk(c, carry):
        ki = ki_ref[key_rows(c), :]
        score = jnp.zeros((kc, tq), F32)
        for j in range(IDX_PAIRS):
            lt = _dot(ki, qit_ref[j])
            score = score + jnp.maximum(lt[:, :tq], 0.0) * wt[2 * j:2 * j + 1, :]
            score = score + jnp.maximum(lt[:, tq:], 0.0) * wt[2 * j + 1:2 * j + 2, :]
        score = jnp.where(score == 0.0, 0.0, score)
        bits = lax.bitcast_convert_type(jnp.where(causal(c), score, neg_inf), jnp.int32)
        key_ref[c] = jnp.where(bits >= 0, bits, bits ^ jnp.int32(0x7FFFFFFF))
        return carry

    lax.fori_loop(0, n_chunks, score_chunk, 0)

    @pl.when(n_chunks % 2 == 1)
    def _():
        key_ref[n_chunks] = jnp.full((kc, tq), INT_MIN, jnp.int32)

    n_chunk_pairs = (n_chunks + 1) // 2

    def radix_step(i, t):
        cand = t | lax.shift_left(jnp.int32(1), 31 - i)
        cand_key = cand ^ jnp.int32(INT_MIN)

        def count_pair(cp, cnt8):
            for u in range(2):
                hit = jnp.where(key_ref[2 * cp + u] >= cand_key, 1.0, 0.0)
                cnt8 = cnt8 + _col_reduce8(hit, jnp.sum)
            return cnt8

        cnt8 = lax.fori_loop(0, n_chunk_pairs, count_pair, jnp.zeros((SUBLANES, tq), F32))
        cnt = jnp.sum(cnt8, axis=0, keepdims=True)
        return jnp.where(cnt >= float(TOP_K), cand, t)

    t_bits = lax.fori_loop(0, 32, radix_step, jnp.zeros((1, tq), jnp.int32))
    thr = t_bits ^ jnp.int32(INT_MIN)

    def mask_chunk(c, cnt):
        ge = key_ref[c] >= thr
        ok = causal(c)
        madd_ref[c] = jnp.where(ok, jnp.where(ge, 0.0, neg_inf), neg_inf)
        return cnt + jnp.sum(jnp.where(ok, jnp.where(ge, 1.0, 0.0), 0.0), axis=0, keepdims=True)

    cnt_ge = lax.fori_loop(0, n_chunks, mask_chunk, jnp.zeros((1, tq), F32))

    @pl.when(jnp.max(cnt_ge) > float(TOP_K))
    def _():
        def count_gt(c, cnt):
            return cnt + jnp.sum(jnp.where(key_ref[c] > thr, 1.0, 0.0), axis=0, keepdims=True)

        need = float(TOP_K) - lax.fori_loop(0, n_chunks, count_gt, jnp.zeros((1, tq), F32))
        a_i = lax.broadcasted_iota(jnp.int32, (kc, kc), 0)
        a_j = lax.broadcasted_iota(jnp.int32, (kc, kc), 1)
        earlier = jnp.where(a_j < a_i, 1.0, 0.0).astype(BF16)

        def tie_chunk(c, run):
            key = key_ref[c]
            gt = key > thr
            eq = key == thr
            e = jnp.where(eq, 1.0, 0.0)
            rank = _dot(earlier, e.astype(BF16)) + run
            keep_eq = jnp.where(rank < need, 0.0, neg_inf)
            val = jnp.where(gt, 0.0, jnp.where(eq, keep_eq, neg_inf))
            madd_ref[c] = jnp.where(causal(c), val, neg_inf)
            return run + jnp.sum(e, axis=0, keepdims=True)

        lax.fori_loop(0, n_chunks, tie_chunk, jnp.zeros((1, tq), F32))

    m_ref[...] = jnp.full(m_ref.shape, M_INIT, F32)
    l_ref[...] = jnp.zeros(l_ref.shape, F32)
    acc_ref[...] = jnp.zeros(acc_ref.shape, F32)

    def attn_chunk(c, carry):
        rows = key_rows(c)
        madd = madd_ref[c]
        madd2 = jnp.concatenate([madd, madd], axis=1)
        alphas = []
        for p in range(HEAD_PAIRS):
            g = p // 2
            s = _dot(k_ref[rows, g * hd:(g + 1) * hd], qt_ref[p]) + madd2
            s_ref[p] = s
            m_old = m_ref[p]
            m_new = jnp.maximum(m_old, jnp.max(_col_reduce8(s, jnp.max), axis=0, keepdims=True))
            m_ref[p] = m_new
            alphas.append(jnp.exp2(m_old - m_new))
        for p in range(HEAD_PAIRS):
            pe = jnp.exp2(s_ref[p] - m_ref[p])
            l_ref[p] = l_ref[p] * alphas[p] + jnp.sum(_col_reduce8(pe, jnp.sum), axis=0,
                                                      keepdims=True)
            pe_ref[p] = pe.astype(BF16)
        for p in range(HEAD_PAIRS):
            acc_ref[p] = acc_ref[p] * alphas[p] + _dot(vt_ref[p // 2, c], pe_ref[p])
        return carry

    lax.fori_loop(0, n_chunks, attn_chunk, 0)

    for p in range(HEAD_PAIRS):
        o = acc_ref[p] / l_ref[p]
        for half in range(2):
            h = 2 * p + half
            o_ref[:, h * hd:(h + 1) * hd] = o[:, half * tq:(half + 1) * tq].T.astype(o_ref.dtype)


def dsa_attention(q_t, k_r, v_t, qi_t, ki, w_t, name):
    nq = SEQ // Q_BLOCK
    row = lambda b, i: b * nq + i
    return pl.pallas_call(
        _dsa_kernel,
        grid=(BATCH, nq),
        in_specs=[
            pl.BlockSpec((None, HEAD_PAIRS, ATTN_HEAD_DIM, 2 * Q_BLOCK),
                         lambda b, i: (row(b, i), 0, 0, 0)),
            pl.BlockSpec((SEQ, ATTN_KV_WIDTH), lambda b, i: (b, 0)),
            pl.BlockSpec((ATTN_KV_HEADS, N_KEY_CHUNKS, ATTN_HEAD_DIM, KEY_CHUNK),
                         lambda b, i: (0, b, 0, 0)),
            pl.BlockSpec((None, IDX_PAIRS, LANES, 2 * Q_BLOCK), lambda b, i: (row(b, i), 0, 0, 0)),
            pl.BlockSpec((SEQ, LANES), lambda b, i: (b, 0)),
            pl.BlockSpec((None, IDX_HEADS, Q_BLOCK), lambda b, i: (row(b, i), 0, 0)),
        ],
        out_specs=pl.BlockSpec((Q_BLOCK, ATTN_WIDTH), lambda b, i: (row(b, i), 0)),
        out_shape=jax.ShapeDtypeStruct((TOKENS, ATTN_WIDTH), BF16),
        scratch_shapes=[pltpu.VMEM((N_KEY_CHUNKS, KEY_CHUNK, Q_BLOCK), jnp.int32),
                        pltpu.VMEM((N_KEY_CHUNKS, KEY_CHUNK, Q_BLOCK), F32),
                        pltpu.VMEM((HEAD_PAIRS, ATTN_HEAD_DIM, 2 * Q_BLOCK), F32),
                        pltpu.VMEM((HEAD_PAIRS, 1, 2 * Q_BLOCK), F32),
                        pltpu.VMEM((HEAD_PAIRS, 1, 2 * Q_BLOCK), F32),
                        pltpu.VMEM((HEAD_PAIRS, KEY_CHUNK, 2 * Q_BLOCK), F32),
                        pltpu.VMEM((HEAD_PAIRS, KEY_CHUNK, 2 * Q_BLOCK), BF16)],
        compiler_params=_params("parallel", "arbitrary"),
        name=name,
    )(q_t, k_r, v_t, qi_t, ki, w_t)


def _rope_lane_tables(positions):
    pos = positions.astype(F32).reshape(TOKENS, 1)

    def tables(rot_dim, width):
        inv_freq = ROPE_THETA ** (-(jnp.arange(0, rot_dim, 2, dtype=F32) / rot_dim))
        ang = pos * inv_freq
        cos, sin = jnp.cos(ang), jnp.sin(ang)
        half = rot_dim // 2
        ones = jnp.ones((TOKENS, width - rot_dim), F32)
        zeros = lambda n: jnp.zeros((TOKENS, n), F32)
        c = jnp.concatenate([cos, cos, ones], axis=1)
        sn = jnp.concatenate([-sin, zeros(width - half)], axis=1)
        sp = jnp.concatenate([zeros(half), sin, zeros(width - rot_dim)], axis=1)
        reps = LANES // width
        return tuple(jnp.tile(t, (1, reps)) for t in (c, sn, sp))

    return tables(ATTN_ROT_DIM, ATTN_HEAD_DIM) + tables(IDX_ROT_DIM, IDX_HEAD_DIM)


def _regroup_w_in(w):
    sizes = (SSM_D_INNER, SSM_CONV_DIM, SSM_HEADS, ATTN_WIDTH, ATTN_KV_WIDTH, ATTN_KV_WIDTH,
             IDX_WIDTH, IDX_HEAD_DIM, IDX_HEADS, D_MODEL, D_MODEL)
    parts = []
    off = 0
    for s in sizes:
        parts.append(w[:, off:off + s])
        off += s
    z, xbc, dt, q, k, v, qi, ki, wi, ga, gb = parts
    main = jnp.concatenate([z, xbc, q, k, v, qi, ga, gb], axis=1).astype(BF16)
    zc = lambda n: jnp.zeros((D_MODEL, n), F32)
    small = jnp.concatenate(
        [ki, zc(LANES - IDX_HEAD_DIM), dt, wi, zc(SMALL_COLS - SMALL_W - IDX_HEADS)],
        axis=1).astype(BF16)
    return main, small


def kernel(x, positions, mix_norm, w_in, conv_w, conv_b, dt_bias, a_log, d_skip, ssm_norm,
           idx_k_norm, w_proj_a, w_proj_b, w_out, ffn_norm, w_ffn_gate, w_ffn_up, w_ffn_down,
           final_norm):
    tabs = _rope_lane_tables(positions)
    xf = x.reshape(TOKENS, D_MODEL)
    for i in range(DEPTH):
        w_main, w_small = _regroup_w_in(w_in[i])
        u = rmsnorm(xf, mix_norm[i], BF16, f"mix_norm_{i}")
        proj = matmul_bf16w(u, w_main, 1024, 1024, F32, f"in_proj_{i}")
        small = matmul_bf16w(u, w_small, 1024, SMALL_COLS, F32, f"in_proj_small_{i}")

        q_t, k_r, v_t, qi_t, ki, w_t, dt_row = dsa_prep(proj, small, tabs, idx_k_norm[i],
                                                        f"dsa_prep_{i}")
        y_b = dsa_attention(q_t, k_r, v_t, qi_t, ki, w_t, f"dsa_{i}")

        dt_raw = small[:, SMALL_DT:SMALL_W].reshape(BATCH, SEQ, SSM_GROUPS, SSM_HEADS_PER_GROUP)
        dt_col = jnp.transpose(dt_raw, (0, 2, 1, 3))
        y_a = ssd_branch(proj, dt_col, dt_row, conv_w[i], conv_b[i], dt_bias[i], a_log[i],
                         d_skip[i], ssm_norm[i], f"ssd_{i}")

        merged = merge_branches(y_a, w_proj_a, y_b, w_proj_b, i, proj, 512, 512, f"merge_{i}")
        xf = matmul_residual(merged, w_out, i, xf, 1024, 1024, f"out_proj_{i}")

        h = rmsnorm(xf, ffn_norm[i], BF16, f"ffn_norm_{i}")
        ff = ffn_up(h, w_ffn_gate, w_ffn_up, i, 1024, 512, f"ffn_up_{i}")
        xf = matmul_residual(ff, w_ffn_down, i, xf, 512, 512, f"ffn_down_{i}")
    out = rmsnorm(xf, final_norm, F32, "final_norm")
    return out.reshape(BATCH, SEQ, D_MODEL)
```

```python
import math

import jax
import jax.numpy as jnp
from jax import lax
from jax.experimental import pallas as pl
from jax.experimental.pallas import tpu as pltpu

F32 = jnp.float32
BF16 = jnp.bfloat16

D_MODEL = 2048
BATCH = 4
SEQ = 2048
DEPTH = 4
TOKENS = BATCH * SEQ
EPS = 1e-6

SSM_D_INNER = 4096
SSM_HEAD_DIM = 64
SSM_HEADS = 64
SSM_GROUPS = 8
SSM_HEADS_PER_GROUP = SSM_HEADS // SSM_GROUPS
SSM_GROUP_WIDTH = SSM_D_INNER // SSM_GROUPS
SSM_STATE = 128
SSM_CONV = 4
SSM_CHUNK = 128
SSM_CONV_DIM = SSM_D_INNER + 2 * SSM_GROUPS * SSM_STATE

ATTN_HEADS = 16
ATTN_HEAD_DIM = 128
ATTN_KV_HEADS = 4
ATTN_WIDTH = ATTN_HEADS * ATTN_HEAD_DIM
ATTN_KV_WIDTH = ATTN_KV_HEADS * ATTN_HEAD_DIM
IDX_HEADS = 16
IDX_HEAD_DIM = 64
IDX_WIDTH = IDX_HEADS * IDX_HEAD_DIM
TOP_K = min(256, SEQ // 4)
Q_BLOCK = 128
ROPE_THETA = 500000.0
ATTN_ROT_DIM = ATTN_HEAD_DIM // 4
IDX_ROT_DIM = IDX_HEAD_DIM // 4
FFN_HIDDEN = 5632

COL_Z = 0
COL_XS = COL_Z + SSM_D_INNER
COL_B = COL_XS + SSM_D_INNER
COL_C = COL_B + SSM_GROUPS * SSM_STATE
COL_Q = COL_C + SSM_GROUPS * SSM_STATE
COL_K = COL_Q + ATTN_WIDTH
COL_V = COL_K + ATTN_KV_WIDTH
COL_QI = COL_V + ATTN_KV_WIDTH
COL_GA = COL_QI + IDX_WIDTH
COL_GB = COL_GA + D_MODEL
MAIN_COLS = COL_GB + D_MODEL
SMALL_COLS = 256
SMALL_DT = 128
SMALL_W = SMALL_DT + SSM_HEADS

LANES = 128
SUBLANES = 8
VMEM_LIMIT_BYTES = 56 * 1024 * 1024

INT_MIN = -(2 ** 31)


def _params(*sem):
    return pltpu.CompilerParams(dimension_semantics=sem, vmem_limit_bytes=VMEM_LIMIT_BYTES)


def _dot(a, b):
    return jnp.dot(a, b, preferred_element_type=F32)


def _sigmoid(x):
    return 1.0 / (1.0 + jnp.exp2(x * (-math.log2(math.e))))


def _rmsnorm_kernel(x_ref, g_ref, o_ref):
    x = x_ref[...]
    ms = jnp.mean(x * x, axis=-1, keepdims=True)
    o_ref[...] = ((x * lax.rsqrt(ms + EPS)) * g_ref[...]).astype(o_ref.dtype)


def rmsnorm(x, g, out_dtype, name, tm=512):
    m, d = x.shape
    return pl.pallas_call(
        _rmsnorm_kernel,
        grid=(m // tm,),
        in_specs=[pl.BlockSpec((tm, d), lambda i: (i, 0)),
                  pl.BlockSpec((1, d), lambda i: (0, 0))],
        out_specs=pl.BlockSpec((tm, d), lambda i: (i, 0)),
        out_shape=jax.ShapeDtypeStruct((m, d), out_dtype),
        compiler_params=_params("parallel"),
        name=name,
    )(x, g.reshape(1, d))


def _cast_on_first_m_step(pairs):
    @pl.when(pl.program_id(1) == 0)
    def _():
        for w_ref, wb_ref in pairs:
            wb_ref[...] = w_ref[...].astype(BF16)


def _weight_spec(k, tn, layer):
    return pl.BlockSpec((None, k, tn), lambda j, i: (layer, 0, j))


def _mm_kernel(a_ref, w_ref, o_ref):
    o_ref[...] = _dot(a_ref[...], w_ref[...]).astype(o_ref.dtype)


def matmul_bf16w(a, w, tm, tn, out_dtype, name):
    m, k = a.shape
    n = w.shape[1]
    return pl.pallas_call(
        _mm_kernel,
        grid=(n // tn, m // tm),
        in_specs=[pl.BlockSpec((tm, k), lambda j, i: (i, 0)),
                  pl.BlockSpec((k, tn), lambda j, i: (0, j))],
        out_specs=pl.BlockSpec((tm, tn), lambda j, i: (i, j)),
        out_shape=jax.ShapeDtypeStruct((m, n), out_dtype),
        compiler_params=_params("parallel", "parallel"),
        name=name,
    )(a, w)


def _mm_residual_kernel(a_ref, w_ref, r_ref, o_ref, wb_ref):
    _cast_on_first_m_step([(w_ref, wb_ref)])
    o_ref[...] = r_ref[...] + _dot(a_ref[...], wb_ref[...])


def matmul_residual(a, w_stack, layer, r, tm, tn, name):
    m, k = a.shape
    n = w_stack.shape[2]
    return pl.pallas_call(
        _mm_residual_kernel,
        grid=(n // tn, m // tm),
        in_specs=[pl.BlockSpec((tm, k), lambda j, i: (i, 0)),
                  _weight_spec(k, tn, layer),
                  pl.BlockSpec((tm, tn), lambda j, i: (i, j))],
        out_specs=pl.BlockSpec((tm, tn), lambda j, i: (i, j)),
        out_shape=jax.ShapeDtypeStruct((m, n), F32),
        scratch_shapes=[pltpu.VMEM((k, tn), BF16)],
        compiler_params=_params("parallel", "arbitrary"),
        name=name,
    )(a, w_stack, r)


def _ffn_up_kernel(a_ref, wg_ref, wu_ref, o_ref, wgb_ref, wub_ref):
    _cast_on_first_m_step([(wg_ref, wgb_ref), (wu_ref, wub_ref)])
    a = a_ref[...]
    g = _dot(a, wgb_ref[...])
    u = _dot(a, wub_ref[...])
    o_ref[...] = ((g * _sigmoid(g)) * u).astype(o_ref.dtype)


def ffn_up(a, wg_stack, wu_stack, layer, tm, tn, name):
    m, k = a.shape
    n = wg_stack.shape[2]
    return pl.pallas_call(
        _ffn_up_kernel,
        grid=(n // tn, m // tm),
        in_specs=[pl.BlockSpec((tm, k), lambda j, i: (i, 0)),
                  _weight_spec(k, tn, layer),
                  _weight_spec(k, tn, layer)],
        out_specs=pl.BlockSpec((tm, tn), lambda j, i: (i, j)),
        out_shape=jax.ShapeDtypeStruct((m, n), BF16),
        scratch_shapes=[pltpu.VMEM((k, tn), BF16), pltpu.VMEM((k, tn), BF16)],
        compiler_params=_params("parallel", "arbitrary"),
        name=name,
    )(a, wg_stack, wu_stack)


def _merge_kernel(ya_ref, wa_ref, yb_ref, wb_ref, ga_ref, gb_ref, o_ref, wab_ref, wbb_ref):
    _cast_on_first_m_step([(wa_ref, wab_ref), (wb_ref, wbb_ref)])
    pa = _dot(ya_ref[...], wab_ref[...])
    pb = _dot(yb_ref[...], wbb_ref[...])
    o_ref[...] = (_sigmoid(ga_ref[...]) * pa + _sigmoid(gb_ref[...]) * pb).astype(o_ref.dtype)


def merge_branches(ya, wa_stack, yb, wb_stack, layer, proj, tm, tn, name):
    m = ya.shape[0]
    ka, kb = wa_stack.shape[1], wb_stack.shape[1]
    n = wa_stack.shape[2]
    ga0 = COL_GA // tn
    gb0 = COL_GB // tn
    return pl.pallas_call(
        _merge_kernel,
        grid=(n // tn, m // tm),
        in_specs=[pl.BlockSpec((tm, ka), lambda j, i: (i, 0)),
                  _weight_spec(ka, tn, layer),
                  pl.BlockSpec((tm, kb), lambda j, i: (i, 0)),
                  _weight_spec(kb, tn, layer),
                  pl.BlockSpec((tm, tn), lambda j, i: (i, ga0 + j)),
                  pl.BlockSpec((tm, tn), lambda j, i: (i, gb0 + j))],
        out_specs=pl.BlockSpec((tm, tn), lambda j, i: (i, j)),
        out_shape=jax.ShapeDtypeStruct((m, n), BF16),
        scratch_shapes=[pltpu.VMEM((ka, tn), BF16), pltpu.VMEM((kb, tn), BF16)],
        compiler_params=_params("parallel", "arbitrary"),
        name=name,
    )(ya, wa_stack, yb, wb_stack, proj, proj)


SSD_STEP_CHUNKS = 4
SSD_STEP_ROWS = SSD_STEP_CHUNKS * SSM_CHUNK
SSD_CONV_WIDTH = SSM_GROUP_WIDTH + 2 * SSM_STATE


def _softplus(x):
    return jnp.maximum(x, 0.0) + jnp.log1p(jnp.exp(-jnp.abs(x)))


def _split3(x):
    hi = x.astype(BF16)
    r1 = x - hi.astype(F32)
    mid = r1.astype(BF16)
    lo = (r1 - mid.astype(F32)).astype(BF16)
    return hi, mid, lo


def _ssd_kernel(x_ref, xh_ref, b_ref, bh_ref, c_ref, ch_ref, cw_ref, cbias_ref, z_ref,
                dtc_ref, dtr_ref, biasr_ref, biasc_ref, alogr_ref, alogc_ref, dsk_ref, ng_ref,
                o_ref, st_ref, pad_ref, xc_ref, y_ref):
    step = pl.program_id(2)

    @pl.when(step == 0)
    def _():
        st_ref[...] = jnp.zeros_like(st_ref)

    gw = SSM_GROUP_WIDTH
    n = SSM_STATE
    lc = SSM_CHUNK
    hp = SSM_HEAD_DIM
    halo = SUBLANES

    def halo_rows(h_ref):
        first = jnp.full(h_ref.shape, step, jnp.int32) == 0
        return jnp.where(first, 0.0, h_ref[...])

    pad_ref[0:halo, 0:gw] = halo_rows(xh_ref)
    pad_ref[0:halo, gw:gw + n] = halo_rows(bh_ref)
    pad_ref[0:halo, gw + n:] = halo_rows(ch_ref)
    pad_ref[halo:, 0:gw] = x_ref[...]
    pad_ref[halo:, gw:gw + n] = b_ref[...]
    pad_ref[halo:, gw + n:] = c_ref[...]
    cw = cw_ref[...]
    cbias = cbias_ref[...]
    for r0 in range(0, SSD_STEP_ROWS, lc):
        acc = cbias + cw[SSM_CONV - 1:SSM_CONV, :] * pad_ref[halo + r0:halo + r0 + lc, :]
        for back in range(1, SSM_CONV):
            lo = halo + r0 - back
            acc = acc + cw[SSM_CONV - 1 - back:SSM_CONV - back, :] * pad_ref[lo:lo + lc, :]
        xc_ref[r0:r0 + lc, :] = acc * _sigmoid(acc)

    log2e = math.log2(math.e)
    dtr_all = _softplus(dtr_ref[...] + biasc_ref[...])
    adt_r_all = dtr_all * (-log2e * jnp.exp(alogc_ref[...]))
    adt_c_all = _softplus(dtc_ref[...] + biasr_ref[...]) * (-log2e * jnp.exp(alogr_ref[...]))

    ii = lax.broadcasted_iota(jnp.int32, (lc, lc), 0)
    jj = lax.broadcasted_iota(jnp.int32, (lc, lc), 1)
    causal = jj <= ii
    tri = jnp.where(causal, 1.0, 0.0).astype(BF16)
    tri_t = jnp.where(ii <= jj, 1.0, 0.0).astype(BF16)

    for cc in range(SSD_STEP_CHUNKS):
        r0 = cc * lc
        cs_c = sum(_dot(tri, p) for p in _split3(adt_c_all[r0:r0 + lc, :]))
        cs_r = sum(_dot(p, tri_t) for p in _split3(adt_r_all[:, r0:r0 + lc]))
        ecl = jnp.exp2(cs_c[lc - 1:lc, :])
        dtr = dtr_all[:, r0:r0 + lc]
        ddr = dtr * jnp.exp2(cs_r[:, lc - 1:lc] - cs_r)

        x16 = xc_ref[r0:r0 + lc, 0:gw].astype(BF16)
        bc = xc_ref[r0:r0 + lc, gw:gw + n]
        cc32 = xc_ref[r0:r0 + lc, gw + n:]
        cb = lax.dot_general(cc32.astype(BF16), bc.astype(BF16), (((1,), (1,)), ((), ())),
                             preferred_element_type=F32)
        bt = bc.T
        for h in range(SSM_HEADS_PER_GROUP):
            lo = h * hp
            cs_i = jnp.broadcast_to(cs_c[:, h:h + 1], (lc, lc))
            lmat = jnp.exp2(jnp.where(causal, cs_i - cs_r[h:h + 1, :], -jnp.inf))
            intra = (cb * lmat) * dtr[h:h + 1, :]
            inter = cc32 * jnp.exp2(cs_i)
            prev = st_ref[h]
            xh = x16[:, lo:lo + hp]
            y_ref[r0:r0 + lc, lo:lo + hp] = _dot(
                jnp.concatenate([intra, inter], axis=1).astype(BF16),
                jnp.concatenate([xh, prev.astype(BF16)], axis=0))
            st_new = _dot((bt * ddr[h:h + 1, :]).astype(BF16), xh)
            st_ref[h] = prev * ecl[:, h:h + 1] + st_new

    y = y_ref[...] + dsk_ref[...] * xc_ref[:, 0:gw]
    z = z_ref[...]
    y = y * (z * _sigmoid(z))
    ms = jnp.mean(y * y, axis=-1, keepdims=True)
    o_ref[...] = ((y * lax.rsqrt(ms + EPS)) * ng_ref[...]).astype(o_ref.dtype)


def ssd_branch(proj, dt_col, dt_row, conv_w, conv_b, dt_bias, a_log, d_skip, norm_g, name):
    rows = SSD_STEP_ROWS
    steps = SEQ // rows
    gw = SSM_GROUP_WIDTH
    n = SSM_STATE
    hg = SSM_HEADS_PER_GROUP
    ng = SSM_GROUPS
    cwid = SSD_CONV_WIDTH

    def per_group(p, lead):
        xs = p[:, :SSM_D_INNER].reshape(lead, ng, gw)
        bs = p[:, SSM_D_INNER:SSM_D_INNER + ng * n].reshape(lead, ng, n)
        cs = p[:, SSM_D_INNER + ng * n:].reshape(lead, ng, n)
        return jnp.transpose(jnp.concatenate([xs, bs, cs], axis=2), (1, 0, 2))

    row = lambda b, g, s: b * steps + s
    hrow = lambda b, g, s: jnp.maximum(row(b, g, s) * (rows // SUBLANES) - 1, 0)
    return pl.pallas_call(
        _ssd_kernel,
        grid=(BATCH, ng, steps),
        in_specs=[
            pl.BlockSpec((rows, gw), lambda b, g, s: (row(b, g, s), COL_XS // gw + g)),
            pl.BlockSpec((SUBLANES, gw), lambda b, g, s: (hrow(b, g, s), COL_XS // gw + g)),
            pl.BlockSpec((rows, n), lambda b, g, s: (row(b, g, s), COL_B // n + g)),
            pl.BlockSpec((SUBLANES, n), lambda b, g, s: (hrow(b, g, s), COL_B // n + g)),
            pl.BlockSpec((rows, n), lambda b, g, s: (row(b, g, s), COL_C // n + g)),
            pl.BlockSpec((SUBLANES, n), lambda b, g, s: (hrow(b, g, s), COL_C // n + g)),
            pl.BlockSpec((None, SSM_CONV, cwid), lambda b, g, s: (g, 0, 0)),
            pl.BlockSpec((None, 1, cwid), lambda b, g, s: (g, 0, 0)),
            pl.BlockSpec((rows, gw), lambda b, g, s: (row(b, g, s), COL_Z // gw + g)),
            pl.BlockSpec((None, None, rows, hg), lambda b, g, s: (b, g, s, 0)),
            pl.BlockSpec((hg, rows), lambda b, g, s: (g, row(b, g, s))),
            pl.BlockSpec((None, 1, hg), lambda b, g, s: (g, 0, 0)),
            pl.BlockSpec((None, hg, 1), lambda b, g, s: (g, 0, 0)),
            pl.BlockSpec((None, 1, hg), lambda b, g, s: (g, 0, 0)),
            pl.BlockSpec((None, hg, 1), lambda b, g, s: (g, 0, 0)),
            pl.BlockSpec((1, gw), lambda b, g, s: (0, g)),
            pl.BlockSpec((1, gw), lambda b, g, s: (0, g)),
        ],
        out_specs=pl.BlockSpec((rows, gw), lambda b, g, s: (row(b, g, s), g)),
        out_shape=jax.ShapeDtypeStruct((TOKENS, SSM_D_INNER), BF16),
        scratch_shapes=[pltpu.VMEM((hg, n, SSM_HEAD_DIM), F32),
                        pltpu.VMEM((rows + SUBLANES, cwid), F32),
                        pltpu.VMEM((rows, cwid), F32),
                        pltpu.VMEM((rows, gw), F32)],
        compiler_params=_params("parallel", "parallel", "arbitrary"),
        name=name,
    )(proj, proj, proj, proj, proj, proj,
      per_group(conv_w, SSM_CONV), per_group(conv_b.reshape(1, SSM_CONV_DIM), 1),
      proj, dt_col, dt_row,
      dt_bias.reshape(ng, 1, hg), dt_bias.reshape(ng, hg, 1),
      a_log.reshape(ng, 1, hg), a_log.reshape(ng, hg, 1),
      jnp.repeat(d_skip, SSM_HEAD_DIM).reshape(1, SSM_D_INNER),
      norm_g.reshape(1, SSM_D_INNER))


KEY_CHUNK = 256
PREP_ROWS = KEY_CHUNK
PREP_QBLOCKS = PREP_ROWS // Q_BLOCK
N_KEY_CHUNKS = SEQ // KEY_CHUNK
HEAD_PAIRS = ATTN_HEADS // 2
IDX_PAIRS = IDX_HEADS // 2
Q_PRESCALE = ATTN_HEAD_DIM ** -0.5 * math.log2(math.e)
W_IDX_SCALE = IDX_HEADS ** -0.5 * IDX_HEAD_DIM ** -0.5
M_INIT = -1e30


def _rope(x, c, sn, sp, half):
    return x * c + pltpu.roll(x, LANES - half, 1) * sn + pltpu.roll(x, half, 1) * sp


def _prep_kernel(q_ref, k_ref, v_ref, qi_ref, smk_ref, smw_ref, ca_ref, sna_ref, spa_ref,
                 ci_ref, sni_ref, spi_ref, gk_ref,
                 qt_ref, ko_ref, vt_ref, qit_ref, kio_ref, wt_ref, dtr_ref):
    ca, sna, spa = ca_ref[...], sna_ref[...], spa_ref[...]
    ci, sni, spi = ci_ref[...], sni_ref[...], spi_ref[...]
    ha = ATTN_ROT_DIM // 2
    hi = IDX_ROT_DIM // 2
    hd = ATTN_HEAD_DIM
    zeros_half = jnp.zeros((LANES - IDX_HEAD_DIM, Q_BLOCK), F32)
    for qq in range(PREP_QBLOCKS):
        rows = slice(qq * Q_BLOCK, (qq + 1) * Q_BLOCK)
        for h in range(ATTN_HEADS):
            qr = _rope(q_ref[rows, h * hd:(h + 1) * hd], ca[rows], sna[rows], spa[rows], ha)
            qt_ref[qq, h // 2, :, (h % 2) * Q_BLOCK:(h % 2 + 1) * Q_BLOCK] = (
                (qr * Q_PRESCALE).T.astype(BF16))
        for j in range(IDX_PAIRS):
            xt = _rope(qi_ref[rows, j * LANES:(j + 1) * LANES],
                       ci[rows], sni[rows], spi[rows], hi).T
            even = jnp.concatenate([xt[:IDX_HEAD_DIM], zeros_half], axis=0)
            odd = jnp.concatenate([xt[IDX_HEAD_DIM:], zeros_half], axis=0)
            qit_ref[qq, j, :, 0:Q_BLOCK] = even.astype(BF16)
            qit_ref[qq, j, :, Q_BLOCK:] = odd.astype(BF16)
    for g in range(ATTN_KV_HEADS):
        cols = slice(g * hd, (g + 1) * hd)
        ko_ref[:, cols] = _rope(k_ref[:, cols], ca, sna, spa, ha).astype(BF16)
        vt_ref[g, 0] = v_ref[:, cols].T.astype(BF16)
    xk = smk_ref[...]
    ms = jnp.sum(xk * xk, axis=-1, keepdims=True) * (1.0 / IDX_HEAD_DIM)
    xk = (xk * lax.rsqrt(ms + EPS)) * gk_ref[...]
    kio_ref[...] = _rope(xk, ci, sni, spi, hi).astype(BF16)
    smt = smw_ref[...].T
    dtr_ref[...] = smt[:SSM_HEADS, :]
    w0 = SMALL_W - SMALL_DT
    for qq in range(PREP_QBLOCKS):
        wt_ref[qq] = smt[w0:w0 + IDX_HEADS, qq * Q_BLOCK:(qq + 1) * Q_BLOCK] * W_IDX_SCALE


def dsa_prep(proj, small, tabs, idx_k_norm, name):
    r = PREP_ROWS
    nqb = PREP_QBLOCKS
    gk = jnp.concatenate([idx_k_norm, jnp.zeros((LANES - IDX_HEAD_DIM,), F32)]).reshape(1, LANES)
    tab_spec = pl.BlockSpec((r, LANES), lambda i: (i, 0))
    return pl.pallas_call(
        _prep_kernel,
        grid=(TOKENS // r,),
        in_specs=[
            pl.BlockSpec((r, ATTN_WIDTH), lambda i: (i, COL_Q // ATTN_WIDTH)),
            pl.BlockSpec((r, ATTN_KV_WIDTH), lambda i: (i, COL_K // ATTN_KV_WIDTH)),
            pl.BlockSpec((r, ATTN_KV_WIDTH), lambda i: (i, COL_V // ATTN_KV_WIDTH)),
            pl.BlockSpec((r, IDX_WIDTH), lambda i: (i, COL_QI // IDX_WIDTH)),
            pl.BlockSpec((r, LANES), lambda i: (i, 0)),
            pl.BlockSpec((r, LANES), lambda i: (i, SMALL_DT // LANES)),
            tab_spec, tab_spec, tab_spec, tab_spec, tab_spec, tab_spec,
            pl.BlockSpec((1, LANES), lambda i: (0, 0)),
        ],
        out_specs=[
            pl.BlockSpec((nqb, HEAD_PAIRS, ATTN_HEAD_DIM, 2 * Q_BLOCK), lambda i: (i, 0, 0, 0)),
            pl.BlockSpec((r, ATTN_KV_WIDTH), lambda i: (i, 0)),
            pl.BlockSpec((ATTN_KV_HEADS, 1, ATTN_HEAD_DIM, r), lambda i: (0, i, 0, 0)),
            pl.BlockSpec((nqb, IDX_PAIRS, LANES, 2 * Q_BLOCK), lambda i: (i, 0, 0, 0)),
            pl.BlockSpec((r, LANES), lambda i: (i, 0)),
            pl.BlockSpec((nqb, IDX_HEADS, Q_BLOCK), lambda i: (i, 0, 0)),
            pl.BlockSpec((SSM_HEADS, r), lambda i: (0, i)),
        ],
        out_shape=[
            jax.ShapeDtypeStruct((TOKENS // Q_BLOCK, HEAD_PAIRS, ATTN_HEAD_DIM, 2 * Q_BLOCK), BF16),
            jax.ShapeDtypeStruct((TOKENS, ATTN_KV_WIDTH), BF16),
            jax.ShapeDtypeStruct((ATTN_KV_HEADS, TOKENS // r, ATTN_HEAD_DIM, r), BF16),
            jax.ShapeDtypeStruct((TOKENS // Q_BLOCK, IDX_PAIRS, LANES, 2 * Q_BLOCK), BF16),
            jax.ShapeDtypeStruct((TOKENS, LANES), BF16),
            jax.ShapeDtypeStruct((TOKENS // Q_BLOCK, IDX_HEADS, Q_BLOCK), F32),
            jax.ShapeDtypeStruct((SSM_HEADS, TOKENS), F32),
        ],
        compiler_params=_params("parallel"),
        name=name,
    )(proj, proj, proj, proj, small, small, *tabs, gk)


def _col_reduce8(x, op):
    rows, lanes = x.shape
    g = x.reshape(4, rows // (4 * SUBLANES), SUBLANES, lanes)
    p = op(g, axis=1)
    red = jnp.maximum if op is jnp.max else jnp.add
    return red(red(p[0], p[1]), red(p[2], p[3]))


def _dsa_kernel(qt_ref, k_ref, vt_ref, qit_ref, ki_ref, wt_ref, o_ref,
                key_ref, madd_ref, acc_ref, m_ref, l_ref, s_ref, pe_ref):
    tq = Q_BLOCK
    kc = KEY_CHUNK
    hd = ATTN_HEAD_DIM
    qb = pl.program_id(1)
    n_chunks = qb // (kc // tq) + 1
    neg_inf = -jnp.inf
    q_pos = qb * tq + lax.broadcasted_iota(jnp.int32, (kc, tq), 1)
    k_off = lax.broadcasted_iota(jnp.int32, (kc, tq), 0)

    def causal(c):
        return c * kc + k_off <= q_pos

    def key_rows(c):
        return pl.ds(pl.multiple_of(c * kc, kc), kc)

    wt = wt_ref[...]

    def score_chunk(c, carry):
        ki = ki_ref[key_rows(c), :]
        score = jnp.zeros((kc, tq), F32)
        for j in range(IDX_PAIRS):
            lt = _dot(ki, qit_ref[j])
            score = score + jnp.maximum(lt[:, :tq], 0.0) * wt[2 * j:2 * j + 1, :]
            score = score + jnp.maximum(lt[:, tq:], 0.0) * wt[2 * j + 1:2 * j + 2, :]
        score = jnp.where(score == 0.0, 0.0, score)
        bits = lax.bitcast_convert_type(jnp.where(causal(c), score, neg_inf), jnp.int32)
        key_ref[c] = jnp.where(bits >= 0, bits, bits ^ jnp.int32(0x7FFFFFFF))
        return carry

    lax.fori_loop(0, n_chunks, score_chunk, 0)

    @pl.when(n_chunks % 2 == 1)
    def _():
        key_ref[n_chunks] = jnp.full((kc, tq), INT_MIN, jnp.int32)

    n_chunk_pairs = (n_chunks + 1) // 2

    def radix_step(i, t):
        cand = t | lax.shift_left(jnp.int32(1), 31 - i)
        cand_key = cand ^ jnp.int32(INT_MIN)

        def count_pair(cp, cnt8):
            for u in range(2):
                hit = jnp.where(key_ref[2 * cp + u] >= cand_key, 1.0, 0.0)
                cnt8 = cnt8 + _col_reduce8(hit, jnp.sum)
            return cnt8

        cnt8 = lax.fori_loop(0, n_chunk_pairs, count_pair, jnp.zeros((SUBLANES, tq), F32))
        cnt = jnp.sum(cnt8, axis=0, keepdims=True)
        return jnp.where(cnt >= float(TOP_K), cand, t)

    t_bits = lax.fori_loop(0, 32, radix_step, jnp.zeros((1, tq), jnp.int32))
    thr = t_bits ^ jnp.int32(INT_MIN)

    def mask_chunk(c, cnt):
        ge = key_ref[c] >= thr
        ok = causal(c)
        madd_ref[c] = jnp.where(ok, jnp.where(ge, 0.0, neg_inf), neg_inf)
        return cnt + jnp.sum(jnp.where(ok, jnp.where(ge, 1.0, 0.0), 0.0), axis=0, keepdims=True)

    cnt_ge = lax.fori_loop(0, n_chunks, mask_chunk, jnp.zeros((1, tq), F32))

    @pl.when(jnp.max(cnt_ge) > float(TOP_K))
    def _():
        def count_gt(c, cnt):
            return cnt + jnp.sum(jnp.where(key_ref[c] > thr, 1.0, 0.0), axis=0, keepdims=True)

        need = float(TOP_K) - lax.fori_loop(0, n_chunks, count_gt, jnp.zeros((1, tq), F32))
        a_i = lax.broadcasted_iota(jnp.int32, (kc, kc), 0)
        a_j = lax.broadcasted_iota(jnp.int32, (kc, kc), 1)
        earlier = jnp.where(a_j < a_i, 1.0, 0.0).astype(BF16)

        def tie_chunk(c, run):
            key = key_ref[c]
            gt = key > thr
            eq = key == thr
            e = jnp.where(eq, 1.0, 0.0)
            rank = _dot(earlier, e.astype(BF16)) + run
            keep_eq = jnp.where(rank < need, 0.0, neg_inf)
            val = jnp.where(gt, 0.0, jnp.where(eq, keep_eq, neg_inf))
            madd_ref[c] = jnp.where(causal(c), val, neg_inf)
            return run + jnp.sum(e, axis=0, keepdims=True)

        lax.fori_loop(0, n_chunks, tie_chunk, jnp.zeros((1, tq), F32))

    m_ref[...] = jnp.full(m_ref.shape, M_INIT, F32)
    l_ref[...] = jnp.zeros(l_ref.shape, F32)
    acc_ref[...] = jnp.zeros(acc_ref.shape, F32)

    def attn_chunk(c, carry):
        rows = key_rows(c)
        madd = madd_ref[c]
        madd2 = jnp.concatenate([madd, madd], axis=1)
        alphas = []
        for p in range(HEAD_PAIRS):
            g = p // 2
            s = _dot(k_ref[rows, g * hd:(g + 1) * hd], qt_ref[p]) + madd2
            s_ref[p] = s
            m_old = m_ref[p]
            m_new = jnp.maximum(m_old, jnp.max(_col_reduce8(s, jnp.max), axis=0, keepdims=True))
            m_ref[p] = m_new
            alphas.append(jnp.exp2(m_old - m_new))
        for p in range(HEAD_PAIRS):
            pe = jnp.exp2(s_ref[p] - m_ref[p])
            l_ref[p] = l_ref[p] * alphas[p] + jnp.sum(_col_reduce8(pe, jnp.sum), axis=0,
                                                      keepdims=True)
            pe_ref[p] = pe.astype(BF16)
        for p in range(HEAD_PAIRS):
            acc_ref[p] = acc_ref[p] * alphas[p] + _dot(vt_ref[p // 2, c], pe_ref[p])
        return carry

    lax.fori_loop(0, n_chunks, attn_chunk, 0)

    for p in range(HEAD_PAIRS):
        o = acc_ref[p] / l_ref[p]
        for half in range(2):
            h = 2 * p + half
            o_ref[:, h * hd:(h + 1) * hd] = o[:, half * tq:(half + 1) * tq].T.astype(o_ref.dtype)


def dsa_attention(q_t, k_r, v_t, qi_t, ki, w_t, name):
    nq = SEQ // Q_BLOCK
    row = lambda b, i: b * nq + i
    return pl.pallas_call(
        _dsa_kernel,
        grid=(BATCH, nq),
        in_specs=[
            pl.BlockSpec((None, HEAD_PAIRS, ATTN_HEAD_DIM, 2 * Q_BLOCK),
                         lambda b, i: (row(b, i), 0, 0, 0)),
            pl.BlockSpec((SEQ, ATTN_KV_WIDTH), lambda b, i: (b, 0)),
            pl.BlockSpec((ATTN_KV_HEADS, N_KEY_CHUNKS, ATTN_HEAD_DIM, KEY_CHUNK),
                         lambda b, i: (0, b, 0, 0)),
            pl.BlockSpec((None, IDX_PAIRS, LANES, 2 * Q_BLOCK), lambda b, i: (row(b, i), 0, 0, 0)),
            pl.BlockSpec((SEQ, LANES), lambda b, i: (b, 0)),
            pl.BlockSpec((None, IDX_HEADS, Q_BLOCK), lambda b, i: (row(b, i), 0, 0)),
        ],
        out_specs=pl.BlockSpec((Q_BLOCK, ATTN_WIDTH), lambda b, i: (row(b, i), 0)),
        out_shape=jax.ShapeDtypeStruct((TOKENS, ATTN_WIDTH), BF16),
        scratch_shapes=[pltpu.VMEM((N_KEY_CHUNKS, KEY_CHUNK, Q_BLOCK), jnp.int32),
                        pltpu.VMEM((N_KEY_CHUNKS, KEY_CHUNK, Q_BLOCK), F32),
                        pltpu.VMEM((HEAD_PAIRS, ATTN_HEAD_DIM, 2 * Q_BLOCK), F32),
                        pltpu.VMEM((HEAD_PAIRS, 1, 2 * Q_BLOCK), F32),
                        pltpu.VMEM((HEAD_PAIRS, 1, 2 * Q_BLOCK), F32),
                        pltpu.VMEM((HEAD_PAIRS, KEY_CHUNK, 2 * Q_BLOCK), F32),
                        pltpu.VMEM((HEAD_PAIRS, KEY_CHUNK, 2 * Q_BLOCK), BF16)],
        compiler_params=_params("parallel", "arbitrary"),
        name=name,
    )(q_t, k_r, v_t, qi_t, ki, w_t)


def _rope_lane_tables(positions):
    pos = positions.astype(F32).reshape(TOKENS, 1)

    def tables(rot_dim, width):
        inv_freq = ROPE_THETA ** (-(jnp.arange(0, rot_dim, 2, dtype=F32) / rot_dim))
        ang = pos * inv_freq
        cos, sin = jnp.cos(ang), jnp.sin(ang)
        half = rot_dim // 2
        ones = jnp.ones((TOKENS, width - rot_dim), F32)
        zeros = lambda n: jnp.zeros((TOKENS, n), F32)
        c = jnp.concatenate([cos, cos, ones], axis=1)
        sn = jnp.concatenate([-sin, zeros(width - half)], axis=1)
        sp = jnp.concatenate([zeros(half), sin, zeros(width - rot_dim)], axis=1)
        reps = LANES // width
        return tuple(jnp.tile(t, (1, reps)) for t in (c, sn, sp))

    return tables(ATTN_ROT_DIM, ATTN_HEAD_DIM) + tables(IDX_ROT_DIM, IDX_HEAD_DIM)


def _regroup_w_in(w):
    sizes = (SSM_D_INNER, SSM_CONV_DIM, SSM_HEADS, ATTN_WIDTH, ATTN_KV_WIDTH, ATTN_KV_WIDTH,
             IDX_WIDTH, IDX_HEAD_DIM, IDX_HEADS, D_MODEL, D_MODEL)
    parts = []
    off = 0
    for s in sizes:
        parts.append(w[:, off:off + s])
        off += s
    z, xbc, dt, q, k, v, qi, ki, wi, ga, gb = parts
    main = jnp.concatenate([z, xbc, q, k, v, qi, ga, gb], axis=1).astype(BF16)
    zc = lambda n: jnp.zeros((D_MODEL, n), F32)
    small = jnp.concatenate(
        [ki, zc(LANES - IDX_HEAD_DIM), dt, wi, zc(SMALL_COLS - SMALL_W - IDX_HEADS)],
        axis=1).astype(BF16)
    return main, small


def kernel(x, positions, mix_norm, w_in, conv_w, conv_b, dt_bias, a_log, d_skip, ssm_norm,
           idx_k_norm, w_proj_a, w_proj_b, w_out, ffn_norm, w_ffn_gate, w_ffn_up, w_ffn_down,
           final_norm):
    tabs = _rope_lane_tables(positions)
    xf = x.reshape(TOKENS, D_MODEL)
    for i in range(DEPTH):
        w_main, w_small = _regroup_w_in(w_in[i])
        u = rmsnorm(xf, mix_norm[i], BF16, f"mix_norm_{i}")
        proj = matmul_bf16w(u, w_main, 1024, 1024, F32, f"in_proj_{i}")
        small = matmul_bf16w(u, w_small, 1024, SMALL_COLS, F32, f"in_proj_small_{i}")

        q_t, k_r, v_t, qi_t, ki, w_t, dt_row = dsa_prep(proj, small, tabs, idx_k_norm[i],
                                                        f"dsa_prep_{i}")
        y_b = dsa_attention(q_t, k_r, v_t, qi_t, ki, w_t, f"dsa_{i}")

        dt_raw = small[:, SMALL_DT:SMALL_W].reshape(BATCH, SEQ, SSM_GROUPS, SSM_HEADS_PER_GROUP)
        dt_col = jnp.transpose(dt_raw, (0, 2, 1, 3))
        y_a = ssd_branch(proj, dt_col, dt_row, conv_w[i], conv_b[i], dt_bias[i], a_log[i],
                         d_skip[i], ssm_norm[i], f"ssd_{i}")

        merged = merge_branches(y_a, w_proj_a, y_b, w_proj_b, i, proj, 512, 512, f"merge_{i}")
        xf = matmul_residual(merged, w_out, i, xf, 1024, 1024, f"out_proj_{i}")

        h = rmsnorm(xf, ffn_norm[i], BF16, f"ffn_norm_{i}")
        ff = ffn_up(h, w_ffn_gate, w_ffn_up, i, 1024, 512, f"ffn_up_{i}")
        xf = matmul_residual(ff, w_ffn_down, i, xf, 512, 512, f"ffn_down_{i}")
    out = rmsnorm(xf, final_norm, F32, "final_norm")
    return out.reshape(BATCH, SEQ, D_MODEL)
```

```python
import math

import jax
import jax.numpy as jnp
from jax import lax
from jax.experimental import pallas as pl
from jax.experimental.pallas import tpu as pltpu

F32 = jnp.float32
BF16 = jnp.bfloat16

D_MODEL = 2048
BATCH = 4
SEQ = 2048
DEPTH = 4
TOKENS = BATCH * SEQ
EPS = 1e-6

SSM_D_INNER = 4096
SSM_HEAD_DIM = 64
SSM_HEADS = 64
SSM_GROUPS = 8
SSM_HEADS_PER_GROUP = SSM_HEADS // SSM_GROUPS
SSM_GROUP_WIDTH = SSM_D_INNER // SSM_GROUPS
SSM_STATE = 128
SSM_CONV = 4
SSM_CHUNK = 128
SSM_CONV_DIM = SSM_D_INNER + 2 * SSM_GROUPS * SSM_STATE

ATTN_HEADS = 16
ATTN_HEAD_DIM = 128
ATTN_KV_HEADS = 4
ATTN_WIDTH = ATTN_HEADS * ATTN_HEAD_DIM
ATTN_KV_WIDTH = ATTN_KV_HEADS * ATTN_HEAD_DIM
IDX_HEADS = 16
IDX_HEAD_DIM = 64
IDX_WIDTH = IDX_HEADS * IDX_HEAD_DIM
TOP_K = min(256, SEQ // 4)
Q_BLOCK = 128
ROPE_THETA = 500000.0
ATTN_ROT_DIM = ATTN_HEAD_DIM // 4
IDX_ROT_DIM = IDX_HEAD_DIM // 4
FFN_HIDDEN = 5632

COL_Z = 0
COL_XS = COL_Z + SSM_D_INNER
COL_B = COL_XS + SSM_D_INNER
COL_C = COL_B + SSM_GROUPS * SSM_STATE
PROJ_A_COLS = COL_C + SSM_GROUPS * SSM_STATE
COL_Q = 0
COL_K = COL_Q + ATTN_WIDTH
COL_V = COL_K + ATTN_KV_WIDTH
COL_QI = COL_V + ATTN_KV_WIDTH
COL_GA = COL_QI + IDX_WIDTH
COL_GB = COL_GA + D_MODEL
PROJ_B_COLS = COL_GB + D_MODEL
SMALL_COLS = 256
SMALL_DT = 128
SMALL_W = SMALL_DT + SSM_HEADS

LANES = 128
SUBLANES = 8
VMEM_LIMIT_BYTES = 56 * 1024 * 1024

INT_MIN = -(2 ** 31)


def _params(*sem):
    return pltpu.CompilerParams(dimension_semantics=sem, vmem_limit_bytes=VMEM_LIMIT_BYTES)


def _dot(a, b):
    return jnp.dot(a, b, preferred_element_type=F32)


def _sigmoid(x):
    return 1.0 / (1.0 + jnp.exp2(x * (-math.log2(math.e))))


def _rmsnorm_kernel(x_ref, g_ref, o_ref):
    x = x_ref[...]
    ms = jnp.mean(x * x, axis=-1, keepdims=True)
    o_ref[...] = ((x * lax.rsqrt(ms + EPS)) * g_ref[...]).astype(o_ref.dtype)


def rmsnorm(x, g, out_dtype, name, tm=512):
    m, d = x.shape
    return pl.pallas_call(
        _rmsnorm_kernel,
        grid=(m // tm,),
        in_specs=[pl.BlockSpec((tm, d), lambda i: (i, 0)),
                  pl.BlockSpec((1, d), lambda i: (0, 0))],
        out_specs=pl.BlockSpec((tm, d), lambda i: (i, 0)),
        out_shape=jax.ShapeDtypeStruct((m, d), out_dtype),
        compiler_params=_params("parallel"),
        name=name,
    )(x, g.reshape(1, d))


def _cast_on_first_m_step(pairs):
    @pl.when(pl.program_id(1) == 0)
    def _():
        for w_ref, wb_ref in pairs:
            wb_ref[...] = w_ref[...].astype(BF16)


def _weight_spec(k, tn, layer):
    return pl.BlockSpec((None, k, tn), lambda j, i: (layer, 0, j))


def _mm_kernel(a_ref, w_ref, o_ref):
    o_ref[...] = _dot(a_ref[...], w_ref[...]).astype(o_ref.dtype)


def matmul_bf16w(a, w, tm, tn, out_dtype, name):
    m, k = a.shape
    n = w.shape[1]
    return pl.pallas_call(
        _mm_kernel,
        grid=(n // tn, m // tm),
        in_specs=[pl.BlockSpec((tm, k), lambda j, i: (i, 0)),
                  pl.BlockSpec((k, tn), lambda j, i: (0, j))],
        out_specs=pl.BlockSpec((tm, tn), lambda j, i: (i, j)),
        out_shape=jax.ShapeDtypeStruct((m, n), out_dtype),
        compiler_params=_params("parallel", "parallel"),
        name=name,
    )(a, w)


def _mm_residual_kernel(a_ref, w_ref, r_ref, o_ref, wb_ref):
    _cast_on_first_m_step([(w_ref, wb_ref)])
    o_ref[...] = r_ref[...] + _dot(a_ref[...], wb_ref[...])


def matmul_residual(a, w_stack, layer, r, tm, tn, name):
    m, k = a.shape
    n = w_stack.shape[2]
    return pl.pallas_call(
        _mm_residual_kernel,
        grid=(n // tn, m // tm),
        in_specs=[pl.BlockSpec((tm, k), lambda j, i: (i, 0)),
                  _weight_spec(k, tn, layer),
                  pl.BlockSpec((tm, tn), lambda j, i: (i, j))],
        out_specs=pl.BlockSpec((tm, tn), lambda j, i: (i, j)),
        out_shape=jax.ShapeDtypeStruct((m, n), F32),
        scratch_shapes=[pltpu.VMEM((k, tn), BF16)],
        compiler_params=_params("parallel", "arbitrary"),
        name=name,
    )(a, w_stack, r)


def _ffn_up_kernel(a_ref, wg_ref, wu_ref, o_ref, wgb_ref, wub_ref):
    _cast_on_first_m_step([(wg_ref, wgb_ref), (wu_ref, wub_ref)])
    a = a_ref[...]
    g = _dot(a, wgb_ref[...])
    u = _dot(a, wub_ref[...])
    o_ref[...] = ((g * _sigmoid(g)) * u).astype(o_ref.dtype)


def ffn_up(a, wg_stack, wu_stack, layer, tm, tn, name):
    m, k = a.shape
    n = wg_stack.shape[2]
    return pl.pallas_call(
        _ffn_up_kernel,
        grid=(n // tn, m // tm),
        in_specs=[pl.BlockSpec((tm, k), lambda j, i: (i, 0)),
                  _weight_spec(k, tn, layer),
                  _weight_spec(k, tn, layer)],
        out_specs=pl.BlockSpec((tm, tn), lambda j, i: (i, j)),
        out_shape=jax.ShapeDtypeStruct((m, n), BF16),
        scratch_shapes=[pltpu.VMEM((k, tn), BF16), pltpu.VMEM((k, tn), BF16)],
        compiler_params=_params("parallel", "arbitrary"),
        name=name,
    )(a, wg_stack, wu_stack)


def _merge_kernel(ya_ref, wa_ref, yb_ref, wb_ref, ga_ref, gb_ref, o_ref, wab_ref, wbb_ref):
    _cast_on_first_m_step([(wa_ref, wab_ref), (wb_ref, wbb_ref)])
    pa = _dot(ya_ref[...], wab_ref[...])
    pb = _dot(yb_ref[...], wbb_ref[...])
    o_ref[...] = (_sigmoid(ga_ref[...]) * pa + _sigmoid(gb_ref[...]) * pb).astype(o_ref.dtype)


def merge_branches(ya, wa_stack, yb, wb_stack, layer, proj, tm, tn, name):
    m = ya.shape[0]
    ka, kb = wa_stack.shape[1], wb_stack.shape[1]
    n = wa_stack.shape[2]
    ga0 = COL_GA // tn
    gb0 = COL_GB // tn
    return pl.pallas_call(
        _merge_kernel,
        grid=(n // tn, m // tm),
        in_specs=[pl.BlockSpec((tm, ka), lambda j, i: (i, 0)),
                  _weight_spec(ka, tn, layer),
                  pl.BlockSpec((tm, kb), lambda j, i: (i, 0)),
                  _weight_spec(kb, tn, layer),
                  pl.BlockSpec((tm, tn), lambda j, i: (i, ga0 + j)),
                  pl.BlockSpec((tm, tn), lambda j, i: (i, gb0 + j))],
        out_specs=pl.BlockSpec((tm, tn), lambda j, i: (i, j)),
        out_shape=jax.ShapeDtypeStruct((m, n), BF16),
        scratch_shapes=[pltpu.VMEM((ka, tn), BF16), pltpu.VMEM((kb, tn), BF16)],
        compiler_params=_params("parallel", "arbitrary"),
        name=name,
    )(ya, wa_stack, yb, wb_stack, proj, proj)


SSD_STEP_CHUNKS = 4
SSD_STEP_ROWS = SSD_STEP_CHUNKS * SSM_CHUNK
SSD_CONV_WIDTH = SSM_GROUP_WIDTH + 2 * SSM_STATE


def _softplus(x):
    return jnp.maximum(x, 0.0) + jnp.log1p(jnp.exp(-jnp.abs(x)))


def _split3(x):
    hi = x.astype(BF16)
    r1 = x - hi.astype(F32)
    mid = r1.astype(BF16)
    lo = (r1 - mid.astype(F32)).astype(BF16)
    return hi, mid, lo


def _ssd_kernel(x_ref, xh_ref, b_ref, bh_ref, c_ref, ch_ref, cw_ref, cbias_ref, z_ref,
                dtc_ref, dtr_ref, biasr_ref, biasc_ref, alogr_ref, alogc_ref, dsk_ref, ng_ref,
                u_ref, wb_ref, o_ref, pb_ref, st_ref, pad_ref, xc_ref, y_ref):
    def shadow_slice(k):
        rows_k = slice(k * SHADOW_SLICE_ROWS, (k + 1) * SHADOW_SLICE_ROWS)
        pb_ref[rows_k, :] = _dot(u_ref[rows_k, :], wb_ref[...])

    step = pl.program_id(2)

    @pl.when(step == 0)
    def _():
        st_ref[...] = jnp.zeros_like(st_ref)

    gw = SSM_GROUP_WIDTH
    n = SSM_STATE
    lc = SSM_CHUNK
    hp = SSM_HEAD_DIM
    halo = SUBLANES

    def halo_rows(h_ref):
        first = jnp.full(h_ref.shape, step, jnp.int32) == 0
        return jnp.where(first, 0.0, h_ref[...])

    pad_ref[0:halo, 0:gw] = halo_rows(xh_ref)
    pad_ref[0:halo, gw:gw + n] = halo_rows(bh_ref)
    pad_ref[0:halo, gw + n:] = halo_rows(ch_ref)
    pad_ref[halo:, 0:gw] = x_ref[...]
    pad_ref[halo:, gw:gw + n] = b_ref[...]
    pad_ref[halo:, gw + n:] = c_ref[...]
    cw = cw_ref[...]
    cbias = cbias_ref[...]
    for r0 in range(0, SSD_STEP_ROWS, lc):
        acc = cbias + cw[SSM_CONV - 1:SSM_CONV, :] * pad_ref[halo + r0:halo + r0 + lc, :]
        for back in range(1, SSM_CONV):
            lo = halo + r0 - back
            acc = acc + cw[SSM_CONV - 1 - back:SSM_CONV - back, :] * pad_ref[lo:lo + lc, :]
        xc_ref[r0:r0 + lc, :] = acc * _sigmoid(acc)

    log2e = math.log2(math.e)
    dtr_all = _softplus(dtr_ref[...] + biasc_ref[...])
    adt_r_all = dtr_all * (-log2e * jnp.exp(alogc_ref[...]))
    adt_c_all = _softplus(dtc_ref[...] + biasr_ref[...]) * (-log2e * jnp.exp(alogr_ref[...]))

    ii = lax.broadcasted_iota(jnp.int32, (lc, lc), 0)
    jj = lax.broadcasted_iota(jnp.int32, (lc, lc), 1)
    causal = jj <= ii
    tri = jnp.where(causal, 1.0, 0.0).astype(BF16)
    tri_t = jnp.where(ii <= jj, 1.0, 0.0).astype(BF16)

    for cc in range(SSD_STEP_CHUNKS):
        r0 = cc * lc
        cs_c = sum(_dot(tri, p) for p in _split3(adt_c_all[r0:r0 + lc, :]))
        cs_r = sum(_dot(p, tri_t) for p in _split3(adt_r_all[:, r0:r0 + lc]))
        ecl = jnp.exp2(cs_c[lc - 1:lc, :])
        dtr = dtr_all[:, r0:r0 + lc]
        ddr = dtr * jnp.exp2(cs_r[:, lc - 1:lc] - cs_r)

        x16 = xc_ref[r0:r0 + lc, 0:gw].astype(BF16)
        bc = xc_ref[r0:r0 + lc, gw:gw + n]
        cc32 = xc_ref[r0:r0 + lc, gw + n:]
        cb = lax.dot_general(cc32.astype(BF16), bc.astype(BF16), (((1,), (1,)), ((), ())),
                             preferred_element_type=F32)
        bt = bc.T
        for h in range(SSM_HEADS_PER_GROUP):
            if h % (SSM_HEADS_PER_GROUP // SHADOW_SLICES_PER_CHUNK) == 0:
                shadow_slice(cc * SHADOW_SLICES_PER_CHUNK
                             + h // (SSM_HEADS_PER_GROUP // SHADOW_SLICES_PER_CHUNK))
            lo = h * hp
            cs_i = jnp.broadcast_to(cs_c[:, h:h + 1], (lc, lc))
            lmat = jnp.exp2(jnp.where(causal, cs_i - cs_r[h:h + 1, :], -jnp.inf))
            intra = (cb * lmat) * dtr[h:h + 1, :]
            inter = cc32 * jnp.exp2(cs_i)
            prev = st_ref[h]
            xh = x16[:, lo:lo + hp]
            y_ref[r0:r0 + lc, lo:lo + hp] = _dot(
                jnp.concatenate([intra, inter], axis=1).astype(BF16),
                jnp.concatenate([xh, prev.astype(BF16)], axis=0))
            st_new = _dot((bt * ddr[h:h + 1, :]).astype(BF16), xh)
            st_ref[h] = prev * ecl[:, h:h + 1] + st_new

    y = y_ref[...] + dsk_ref[...] * xc_ref[:, 0:gw]
    z = z_ref[...]
    y = y * (z * _sigmoid(z))
    ms = jnp.mean(y * y, axis=-1, keepdims=True)
    o_ref[...] = ((y * lax.rsqrt(ms + EPS)) * ng_ref[...]).astype(o_ref.dtype)


SHADOW_TM = 1024
SHADOW_SLICES_PER_CHUNK = 1
SHADOW_SLICE_ROWS = SHADOW_TM // (SSD_STEP_CHUNKS * SHADOW_SLICES_PER_CHUNK)
SHADOW_TN = PROJ_B_COLS * TOKENS // SHADOW_TM // (BATCH * SSM_GROUPS * (SEQ // SSD_STEP_ROWS))


def ssd_branch(proj, dt_col, dt_row, conv_w, conv_b, dt_bias, a_log, d_skip, norm_g, u, w_b,
               name):
    rows = SSD_STEP_ROWS
    m_tiles = TOKENS // SHADOW_TM
    tile_i = lambda b, g, s: ((b * SSM_GROUPS + g) * (SEQ // rows) + s) % m_tiles
    tile_j = lambda b, g, s: ((b * SSM_GROUPS + g) * (SEQ // rows) + s) // m_tiles
    steps = SEQ // rows
    gw = SSM_GROUP_WIDTH
    n = SSM_STATE
    hg = SSM_HEADS_PER_GROUP
    ng = SSM_GROUPS
    cwid = SSD_CONV_WIDTH

    def per_group(p, lead):
        xs = p[:, :SSM_D_INNER].reshape(lead, ng, gw)
        bs = p[:, SSM_D_INNER:SSM_D_INNER + ng * n].reshape(lead, ng, n)
        cs = p[:, SSM_D_INNER + ng * n:].reshape(lead, ng, n)
        return jnp.transpose(jnp.concatenate([xs, bs, cs], axis=2), (1, 0, 2))

    row = lambda b, g, s: b * steps + s
    hrow = lambda b, g, s: jnp.maximum(row(b, g, s) * (rows // SUBLANES) - 1, 0)
    return pl.pallas_call(
        _ssd_kernel,
        grid=(BATCH, ng, steps),
        in_specs=[
            pl.BlockSpec((rows, gw), lambda b, g, s: (row(b, g, s), COL_XS // gw + g)),
            pl.BlockSpec((SUBLANES, gw), lambda b, g, s: (hrow(b, g, s), COL_XS // gw + g)),
            pl.BlockSpec((rows, n), lambda b, g, s: (row(b, g, s), COL_B // n + g)),
            pl.BlockSpec((SUBLANES, n), lambda b, g, s: (hrow(b, g, s), COL_B // n + g)),
            pl.BlockSpec((rows, n), lambda b, g, s: (row(b, g, s), COL_C // n + g)),
            pl.BlockSpec((SUBLANES, n), lambda b, g, s: (hrow(b, g, s), COL_C // n + g)),
            pl.BlockSpec((None, SSM_CONV, cwid), lambda b, g, s: (g, 0, 0)),
            pl.BlockSpec((None, 1, cwid), lambda b, g, s: (g, 0, 0)),
            pl.BlockSpec((rows, gw), lambda b, g, s: (row(b, g, s), COL_Z // gw + g)),
            pl.BlockSpec((None, None, rows, hg), lambda b, g, s: (b, g, s, 0)),
            pl.BlockSpec((hg, rows), lambda b, g, s: (g, row(b, g, s))),
            pl.BlockSpec((None, 1, hg), lambda b, g, s: (g, 0, 0)),
            pl.BlockSpec((None, hg, 1), lambda b, g, s: (g, 0, 0)),
            pl.BlockSpec((None, 1, hg), lambda b, g, s: (g, 0, 0)),
            pl.BlockSpec((None, hg, 1), lambda b, g, s: (g, 0, 0)),
            pl.BlockSpec((1, gw), lambda b, g, s: (0, g)),
            pl.BlockSpec((1, gw), lambda b, g, s: (0, g)),
            pl.BlockSpec((SHADOW_TM, D_MODEL), lambda b, g, s: (tile_i(b, g, s), 0)),
            pl.BlockSpec((D_MODEL, SHADOW_TN), lambda b, g, s: (0, tile_j(b, g, s))),
        ],
        out_specs=[pl.BlockSpec((rows, gw), lambda b, g, s: (row(b, g, s), g)),
                   pl.BlockSpec((SHADOW_TM, SHADOW_TN),
                                lambda b, g, s: (tile_i(b, g, s), tile_j(b, g, s)))],
        out_shape=[jax.ShapeDtypeStruct((TOKENS, SSM_D_INNER), BF16),
                   jax.ShapeDtypeStruct((TOKENS, PROJ_B_COLS), F32)],
        scratch_shapes=[pltpu.VMEM((hg, n, SSM_HEAD_DIM), F32),
                        pltpu.VMEM((rows + SUBLANES, cwid), F32),
                        pltpu.VMEM((rows, cwid), F32),
                        pltpu.VMEM((rows, gw), F32)],
        compiler_params=_params("parallel", "parallel", "arbitrary"),
        name=name,
    )(proj, proj, proj, proj, proj, proj,
      per_group(conv_w, SSM_CONV), per_group(conv_b.reshape(1, SSM_CONV_DIM), 1),
      proj, dt_col, dt_row,
      dt_bias.reshape(ng, 1, hg), dt_bias.reshape(ng, hg, 1),
      a_log.reshape(ng, 1, hg), a_log.reshape(ng, hg, 1),
      jnp.repeat(d_skip, SSM_HEAD_DIM).reshape(1, SSM_D_INNER),
      norm_g.reshape(1, SSM_D_INNER), u, w_b)


KEY_CHUNK = 256
PREP_ROWS = KEY_CHUNK
PREP_QBLOCKS = PREP_ROWS // Q_BLOCK
N_KEY_CHUNKS = SEQ // KEY_CHUNK
HEAD_PAIRS = ATTN_HEADS // 2
IDX_PAIRS = IDX_HEADS // 2
Q_PRESCALE = ATTN_HEAD_DIM ** -0.5 * math.log2(math.e)
W_IDX_SCALE = IDX_HEADS ** -0.5 * IDX_HEAD_DIM ** -0.5
M_INIT = -1e30


def _rope(x, c, sn, sp, half):
    return x * c + pltpu.roll(x, LANES - half, 1) * sn + pltpu.roll(x, half, 1) * sp


def _prep_kernel(q_ref, k_ref, v_ref, qi_ref, smk_ref, smw_ref, ca_ref, sna_ref, spa_ref,
                 ci_ref, sni_ref, spi_ref, gk_ref,
                 qt_ref, ko_ref, vt_ref, qit_ref, kio_ref, wt_ref):
    ca, sna, spa = ca_ref[...], sna_ref[...], spa_ref[...]
    ci, sni, spi = ci_ref[...], sni_ref[...], spi_ref[...]
    ha = ATTN_ROT_DIM // 2
    hi = IDX_ROT_DIM // 2
    hd = ATTN_HEAD_DIM
    zeros_half = jnp.zeros((LANES - IDX_HEAD_DIM, Q_BLOCK), F32)
    for qq in range(PREP_QBLOCKS):
        rows = slice(qq * Q_BLOCK, (qq + 1) * Q_BLOCK)
        for h in range(ATTN_HEADS):
            qr = _rope(q_ref[rows, h * hd:(h + 1) * hd], ca[rows], sna[rows], spa[rows], ha)
            qt_ref[qq, h // 2, :, (h % 2) * Q_BLOCK:(h % 2 + 1) * Q_BLOCK] = (
                (qr * Q_PRESCALE).T.astype(BF16))
        for j in range(IDX_PAIRS):
            xt = _rope(qi_ref[rows, j * LANES:(j + 1) * LANES],
                       ci[rows], sni[rows], spi[rows], hi).T
            even = jnp.concatenate([xt[:IDX_HEAD_DIM], zeros_half], axis=0)
            odd = jnp.concatenate([xt[IDX_HEAD_DIM:], zeros_half], axis=0)
            qit_ref[qq, j, :, 0:Q_BLOCK] = even.astype(BF16)
            qit_ref[qq, j, :, Q_BLOCK:] = odd.astype(BF16)
    for g in range(ATTN_KV_HEADS):
        cols = slice(g * hd, (g + 1) * hd)
        ko_ref[:, cols] = _rope(k_ref[:, cols], ca, sna, spa, ha).astype(BF16)
        vt_ref[g, 0] = v_ref[:, cols].T.astype(BF16)
    xk = smk_ref[...]
    ms = jnp.sum(xk * xk, axis=-1, keepdims=True) * (1.0 / IDX_HEAD_DIM)
    xk = (xk * lax.rsqrt(ms + EPS)) * gk_ref[...]
    kio_ref[...] = _rope(xk, ci, sni, spi, hi).astype(BF16)
    smt = smw_ref[...].T
    w0 = SMALL_W - SMALL_DT
    for qq in range(PREP_QBLOCKS):
        wt_ref[qq] = smt[w0:w0 + IDX_HEADS, qq * Q_BLOCK:(qq + 1) * Q_BLOCK] * W_IDX_SCALE


def dsa_prep(proj, small, tabs, idx_k_norm, name):
    r = PREP_ROWS
    nqb = PREP_QBLOCKS
    gk = jnp.concatenate([idx_k_norm, jnp.zeros((LANES - IDX_HEAD_DIM,), F32)]).reshape(1, LANES)
    tab_spec = pl.BlockSpec((r, LANES), lambda i: (i, 0))
    return pl.pallas_call(
        _prep_kernel,
        grid=(TOKENS // r,),
        in_specs=[
            pl.BlockSpec((r, ATTN_WIDTH), lambda i: (i, COL_Q // ATTN_WIDTH)),
            pl.BlockSpec((r, ATTN_KV_WIDTH), lambda i: (i, COL_K // ATTN_KV_WIDTH)),
            pl.BlockSpec((r, ATTN_KV_WIDTH), lambda i: (i, COL_V // ATTN_KV_WIDTH)),
            pl.BlockSpec((r, IDX_WIDTH), lambda i: (i, COL_QI // IDX_WIDTH)),
            pl.BlockSpec((r, LANES), lambda i: (i, 0)),
            pl.BlockSpec((r, LANES), lambda i: (i, SMALL_DT // LANES)),
            tab_spec, tab_spec, tab_spec, tab_spec, tab_spec, tab_spec,
            pl.BlockSpec((1, LANES), lambda i: (0, 0)),
        ],
        out_specs=[
            pl.BlockSpec((nqb, HEAD_PAIRS, ATTN_HEAD_DIM, 2 * Q_BLOCK), lambda i: (i, 0, 0, 0)),
            pl.BlockSpec((r, ATTN_KV_WIDTH), lambda i: (i, 0)),
            pl.BlockSpec((ATTN_KV_HEADS, 1, ATTN_HEAD_DIM, r), lambda i: (0, i, 0, 0)),
            pl.BlockSpec((nqb, IDX_PAIRS, LANES, 2 * Q_BLOCK), lambda i: (i, 0, 0, 0)),
            pl.BlockSpec((r, LANES), lambda i: (i, 0)),
            pl.BlockSpec((nqb, IDX_HEADS, Q_BLOCK), lambda i: (i, 0, 0)),
        ],
        out_shape=[
            jax.ShapeDtypeStruct((TOKENS // Q_BLOCK, HEAD_PAIRS, ATTN_HEAD_DIM, 2 * Q_BLOCK), BF16),
            jax.ShapeDtypeStruct((TOKENS, ATTN_KV_WIDTH), BF16),
            jax.ShapeDtypeStruct((ATTN_KV_HEADS, TOKENS // r, ATTN_HEAD_DIM, r), BF16),
            jax.ShapeDtypeStruct((TOKENS // Q_BLOCK, IDX_PAIRS, LANES, 2 * Q_BLOCK), BF16),
            jax.ShapeDtypeStruct((TOKENS, LANES), BF16),
            jax.ShapeDtypeStruct((TOKENS // Q_BLOCK, IDX_HEADS, Q_BLOCK), F32),
        ],
        compiler_params=_params("parallel"),
        name=name,
    )(proj, proj, proj, proj, small, small, *tabs, gk)


def _col_reduce8(x, op):
    rows, lanes = x.shape
    g = x.reshape(4, rows // (4 * SUBLANES), SUBLANES, lanes)
    p = op(g, axis=1)
    red = jnp.maximum if op is jnp.max else jnp.add
    return red(red(p[0], p[1]), red(p[2], p[3]))


def _dsa_kernel(qt_ref, k_ref, vt_ref, qit_ref, ki_ref, wt_ref, o_ref,
                key_ref, madd_ref, acc_ref, m_ref, l_ref, alpha_ref, s_ref, pe_ref):
    tq = Q_BLOCK
    kc = KEY_CHUNK
    hd = ATTN_HEAD_DIM
    qb = pl.program_id(1)
    n_chunks = qb // (kc // tq) + 1
    neg_inf = -jnp.inf
    q_pos = qb * tq + lax.broadcasted_iota(jnp.int32, (kc, tq), 1)
    k_off = lax.broadcasted_iota(jnp.int32, (kc, tq), 0)

    def causal(c):
        return c * kc + k_off <= q_pos

    def key_rows(c):
        return pl.ds(pl.multiple_of(c * kc, kc), kc)

    wt = wt_ref[...]

    def score_chunk(c, carry):
        ki = ki_ref[key_rows(c), :]
        score = jnp.zeros((kc, tq), F32)
        for j in range(IDX_PAIRS):
            lt = _dot(ki, qit_ref[j])
            score = score + jnp.maximum(lt[:, :tq], 0.0) * wt[2 * j:2 * j + 1, :]
            score = score + jnp.maximum(lt[:, tq:], 0.0) * wt[2 * j + 1:2 * j + 2, :]
        score = jnp.where(score == 0.0, 0.0, score)
        bits = lax.bitcast_convert_type(jnp.where(causal(c), score, neg_inf), jnp.int32)
        key_ref[c] = jnp.where(bits >= 0, bits, bits ^ jnp.int32(0x7FFFFFFF))
        return carry

    lax.fori_loop(0, n_chunks, score_chunk, 0)

    @pl.when(n_chunks % 2 == 1)
    def _():
        key_ref[n_chunks] = jnp.full((kc, tq), INT_MIN, jnp.int32)

    n_chunk_pairs = (n_chunks + 1) // 2

    def radix_step(i, t):
        cand = t | lax.shift_left(jnp.int32(1), 31 - i)
        cand_key = cand ^ jnp.int32(INT_MIN)

        def count_pair(cp, cnt8):
            for u in range(2):
                hit = jnp.where(key_ref[2 * cp + u] >= cand_key, 1.0, 0.0)
                cnt8 = cnt8 + _col_reduce8(hit, jnp.sum)
            return cnt8

        cnt8 = lax.fori_loop(0, n_chunk_pairs, count_pair, jnp.zeros((SUBLANES, tq), F32))
        cnt = jnp.sum(cnt8, axis=0, keepdims=True)
        return jnp.where(cnt >= float(TOP_K), cand, t)

    t_bits = lax.fori_loop(0, 32, radix_step, jnp.zeros((1, tq), jnp.int32))
    thr = t_bits ^ jnp.int32(INT_MIN)

    def mask_chunk(c, cnt):
        ge = key_ref[c] >= thr
        ok = causal(c)
        madd_ref[c] = jnp.where(ok, jnp.where(ge, 0.0, neg_inf), neg_inf)
        return cnt + jnp.sum(jnp.where(ok, jnp.where(ge, 1.0, 0.0), 0.0), axis=0, keepdims=True)

    cnt_ge = lax.fori_loop(0, n_chunks, mask_chunk, jnp.zeros((1, tq), F32))

    @pl.when(jnp.max(cnt_ge) > float(TOP_K))
    def _():
        def count_gt(c, cnt):
            return cnt + jnp.sum(jnp.where(key_ref[c] > thr, 1.0, 0.0), axis=0, keepdims=True)

        need = float(TOP_K) - lax.fori_loop(0, n_chunks, count_gt, jnp.zeros((1, tq), F32))
        a_i = lax.broadcasted_iota(jnp.int32, (kc, kc), 0)
        a_j = lax.broadcasted_iota(jnp.int32, (kc, kc), 1)
        earlier = jnp.where(a_j < a_i, 1.0, 0.0).astype(BF16)

        def tie_chunk(c, run):
            key = key_ref[c]
            gt = key > thr
            eq = key == thr
            e = jnp.where(eq, 1.0, 0.0)
            rank = _dot(earlier, e.astype(BF16)) + run
            keep_eq = jnp.where(rank < need, 0.0, neg_inf)
            val = jnp.where(gt, 0.0, jnp.where(eq, keep_eq, neg_inf))
            madd_ref[c] = jnp.where(causal(c), val, neg_inf)
            return run + jnp.sum(e, axis=0, keepdims=True)

        lax.fori_loop(0, n_chunks, tie_chunk, jnp.zeros((1, tq), F32))

    m_ref[...] = jnp.full(m_ref.shape, M_INIT, F32)
    l_ref[...] = jnp.zeros(l_ref.shape, F32)
    acc_ref[...] = jnp.zeros(acc_ref.shape, F32)

    def attn_chunk(c, carry):
        half_rows = kc // 2
        for p in range(HEAD_PAIRS):
            g = p // 2
            top8 = None
            for rh in range(2):
                r0 = pl.multiple_of(c * kc + rh * half_rows, half_rows)
                sub = slice(rh * half_rows, (rh + 1) * half_rows)
                s = _dot(k_ref[pl.ds(r0, half_rows), g * hd:(g + 1) * hd], qt_ref[p])
                madd = madd_ref[c, sub, :]
                s = s + jnp.concatenate([madd, madd], axis=1)
                s_ref[p, sub, :] = s
                t8 = _col_reduce8(s, jnp.max)
                top8 = t8 if top8 is None else jnp.maximum(top8, t8)
            m_old = m_ref[p]
            m_new = jnp.maximum(m_old, jnp.max(top8, axis=0, keepdims=True))
            m_ref[p] = m_new
            alpha_ref[p] = jnp.exp2(m_old - m_new)
        for p in range(HEAD_PAIRS):
            sum8 = None
            for rh in range(2):
                sub = slice(rh * half_rows, (rh + 1) * half_rows)
                pe = jnp.exp2(s_ref[p, sub, :] - m_ref[p])
                pe_ref[p, sub, :] = pe.astype(BF16)
                s8 = _col_reduce8(pe, jnp.sum)
                sum8 = s8 if sum8 is None else sum8 + s8
            l_ref[p] = l_ref[p] * alpha_ref[p] + jnp.sum(sum8, axis=0, keepdims=True)
        for p in range(HEAD_PAIRS):
            acc_ref[p] = acc_ref[p] * alpha_ref[p] + _dot(vt_ref[p // 2, c], pe_ref[p])
        return carry

    lax.fori_loop(0, n_chunks, attn_chunk, 0)

    for p in range(HEAD_PAIRS):
        o = acc_ref[p] / l_ref[p]
        for half in range(2):
            h = 2 * p + half
            o_ref[:, h * hd:(h + 1) * hd] = o[:, half * tq:(half + 1) * tq].T.astype(o_ref.dtype)


def dsa_attention(q_t, k_r, v_t, qi_t, ki, w_t, name):
    nq = SEQ // Q_BLOCK
    row = lambda b, i: b * nq + i
    return pl.pallas_call(
        _dsa_kernel,
        grid=(BATCH, nq),
        in_specs=[
            pl.BlockSpec((None, HEAD_PAIRS, ATTN_HEAD_DIM, 2 * Q_BLOCK),
                         lambda b, i: (row(b, i), 0, 0, 0)),
            pl.BlockSpec((SEQ, ATTN_KV_WIDTH), lambda b, i: (b, 0)),
            pl.BlockSpec((ATTN_KV_HEADS, N_KEY_CHUNKS, ATTN_HEAD_DIM, KEY_CHUNK),
                         lambda b, i: (0, b, 0, 0)),
            pl.BlockSpec((None, IDX_PAIRS, LANES, 2 * Q_BLOCK), lambda b, i: (row(b, i), 0, 0, 0)),
            pl.BlockSpec((SEQ, LANES), lambda b, i: (b, 0)),
            pl.BlockSpec((None, IDX_HEADS, Q_BLOCK), lambda b, i: (row(b, i), 0, 0)),
        ],
        out_specs=pl.BlockSpec((Q_BLOCK, ATTN_WIDTH), lambda b, i: (row(b, i), 0)),
        out_shape=jax.ShapeDtypeStruct((TOKENS, ATTN_WIDTH), BF16),
        scratch_shapes=[pltpu.VMEM((N_KEY_CHUNKS, KEY_CHUNK, Q_BLOCK), jnp.int32),
                        pltpu.VMEM((N_KEY_CHUNKS, KEY_CHUNK, Q_BLOCK), F32),
                        pltpu.VMEM((HEAD_PAIRS, ATTN_HEAD_DIM, 2 * Q_BLOCK), F32),
                        pltpu.VMEM((HEAD_PAIRS, 1, 2 * Q_BLOCK), F32),
                        pltpu.VMEM((HEAD_PAIRS, 1, 2 * Q_BLOCK), F32),
                        pltpu.VMEM((HEAD_PAIRS, 1, 2 * Q_BLOCK), F32),
                        pltpu.VMEM((HEAD_PAIRS, KEY_CHUNK, 2 * Q_BLOCK), F32),
                        pltpu.VMEM((HEAD_PAIRS, KEY_CHUNK, 2 * Q_BLOCK), BF16)],
        compiler_params=_params("parallel", "arbitrary"),
        name=name,
    )(q_t, k_r, v_t, qi_t, ki, w_t)


def _rope_lane_tables(positions):
    pos = positions.astype(F32).reshape(TOKENS, 1)

    def tables(rot_dim, width):
        inv_freq = ROPE_THETA ** (-(jnp.arange(0, rot_dim, 2, dtype=F32) / rot_dim))
        ang = pos * inv_freq
        cos, sin = jnp.cos(ang), jnp.sin(ang)
        half = rot_dim // 2
        ones = jnp.ones((TOKENS, width - rot_dim), F32)
        zeros = lambda n: jnp.zeros((TOKENS, n), F32)
        c = jnp.concatenate([cos, cos, ones], axis=1)
        sn = jnp.concatenate([-sin, zeros(width - half)], axis=1)
        sp = jnp.concatenate([zeros(half), sin, zeros(width - rot_dim)], axis=1)
        reps = LANES // width
        return tuple(jnp.tile(t, (1, reps)) for t in (c, sn, sp))

    return tables(ATTN_ROT_DIM, ATTN_HEAD_DIM) + tables(IDX_ROT_DIM, IDX_HEAD_DIM)


def _regroup_w_in(w):
    sizes = (SSM_HEADS, ATTN_WIDTH, ATTN_KV_WIDTH, ATTN_KV_WIDTH, IDX_WIDTH, IDX_HEAD_DIM,
             IDX_HEADS, D_MODEL, D_MODEL)
    parts = []
    off = PROJ_A_COLS
    for s in sizes:
        parts.append(w[:, off:off + s])
        off += s
    dt, q, k, v, qi, ki, wi, ga, gb = parts
    w_a = w[:, :PROJ_A_COLS].astype(BF16)
    w_b = jnp.concatenate([q, k, v, qi, ga, gb], axis=1).astype(BF16)
    zc = lambda n: jnp.zeros((D_MODEL, n), F32)
    small = jnp.concatenate(
        [ki, zc(LANES - IDX_HEAD_DIM), dt, wi, zc(SMALL_COLS - SMALL_W - IDX_HEADS)],
        axis=1).astype(BF16)
    return w_a, w_b, small


def kernel(x, positions, mix_norm, w_in, conv_w, conv_b, dt_bias, a_log, d_skip, ssm_norm,
           idx_k_norm, w_proj_a, w_proj_b, w_out, ffn_norm, w_ffn_gate, w_ffn_up, w_ffn_down,
           final_norm):
    tabs = _rope_lane_tables(positions)
    xf = x.reshape(TOKENS, D_MODEL)
    for i in range(DEPTH):
        w_a, w_b, w_small = _regroup_w_in(w_in[i])
        u = rmsnorm(xf, mix_norm[i], BF16, f"mix_norm_{i}")
        proj_a = matmul_bf16w(u, w_a, 1024, 1024, F32, f"in_proj_a_{i}")
        small = matmul_bf16w(u, w_small, 1024, SMALL_COLS, F32, f"in_proj_small_{i}")

        dt_raw = small[:, SMALL_DT:SMALL_W]
        dt_col = jnp.transpose(
            dt_raw.reshape(BATCH, SEQ, SSM_GROUPS, SSM_HEADS_PER_GROUP), (0, 2, 1, 3))
        y_a, proj_b = ssd_branch(proj_a, dt_col, dt_raw.T, conv_w[i], conv_b[i], dt_bias[i],
                                 a_log[i], d_skip[i], ssm_norm[i], u, w_b, f"ssd_{i}")

        q_t, k_r, v_t, qi_t, ki, w_t = dsa_prep(proj_b, small, tabs, idx_k_norm[i],
                                                f"dsa_prep_{i}")
        y_b = dsa_attention(q_t, k_r, v_t, qi_t, ki, w_t, f"dsa_{i}")

        merged = merge_branches(y_a, w_proj_a, y_b, w_proj_b, i, proj_b, 512, 512, f"merge_{i}")
        xf = matmul_residual(merged, w_out, i, xf, 1024, 1024, f"out_proj_{i}")

        h = rmsnorm(xf, ffn_norm[i], BF16, f"ffn_norm_{i}")
        ff = ffn_up(h, w_ffn_gate, w_ffn_up, i, 1024, 512, f"ffn_up_{i}")
        xf = matmul_residual(ff, w_ffn_down, i, xf, 512, 512, f"ffn_down_{i}")
    out = rmsnorm(xf, final_norm, F32, "final_norm")
    return out.reshape(BATCH, SEQ, D_MODEL)
```

```python
import math

import jax
import jax.numpy as jnp
from jax import lax
from jax.experimental import pallas as pl
from jax.experimental.pallas import tpu as pltpu

F32 = jnp.float32
BF16 = jnp.bfloat16

D_MODEL = 2048
BATCH = 4
SEQ = 2048
DEPTH = 4
TOKENS = BATCH * SEQ
EPS = 1e-6

SSM_D_INNER = 4096
SSM_HEAD_DIM = 64
SSM_HEADS = 64
SSM_GROUPS = 8
SSM_HEADS_PER_GROUP = SSM_HEADS // SSM_GROUPS
SSM_GROUP_WIDTH = SSM_D_INNER // SSM_GROUPS
SSM_STATE = 128
SSM_CONV = 4
SSM_CHUNK = 128
SSM_CONV_DIM = SSM_D_INNER + 2 * SSM_GROUPS * SSM_STATE

ATTN_HEADS = 16
ATTN_HEAD_DIM = 128
ATTN_KV_HEADS = 4
ATTN_WIDTH = ATTN_HEADS * ATTN_HEAD_DIM
ATTN_KV_WIDTH = ATTN_KV_HEADS * ATTN_HEAD_DIM
IDX_HEADS = 16
IDX_HEAD_DIM = 64
IDX_WIDTH = IDX_HEADS * IDX_HEAD_DIM
TOP_K = min(256, SEQ // 4)
Q_BLOCK = 128
ROPE_THETA = 500000.0
ATTN_ROT_DIM = ATTN_HEAD_DIM // 4
IDX_ROT_DIM = IDX_HEAD_DIM // 4
FFN_HIDDEN = 5632

COL_Z = 0
COL_XS = COL_Z + SSM_D_INNER
COL_B = COL_XS + SSM_D_INNER
COL_C = COL_B + SSM_GROUPS * SSM_STATE
PROJ_A_COLS = COL_C + SSM_GROUPS * SSM_STATE
COL_Q = 0
COL_K = COL_Q + ATTN_WIDTH
COL_V = COL_K + ATTN_KV_WIDTH
COL_QI = COL_V + ATTN_KV_WIDTH
COL_GA = COL_QI + IDX_WIDTH
COL_GB = COL_GA + D_MODEL
PROJ_B_COLS = COL_GB + D_MODEL
SMALL_COLS = 256
SMALL_DT = 128
SMALL_W = SMALL_DT + SSM_HEADS

LANES = 128
SUBLANES = 8
VMEM_LIMIT_BYTES = 56 * 1024 * 1024

INT_MIN = -(2 ** 31)


def _params(*sem):
    return pltpu.CompilerParams(dimension_semantics=sem, vmem_limit_bytes=VMEM_LIMIT_BYTES)


def _dot(a, b):
    return jnp.dot(a, b, preferred_element_type=F32)


def _sigmoid(x):
    return 1.0 / (1.0 + jnp.exp2(x * (-math.log2(math.e))))


def _rmsnorm_kernel(x_ref, g_ref, o_ref):
    x = x_ref[...]
    ms = jnp.mean(x * x, axis=-1, keepdims=True)
    o_ref[...] = ((x * lax.rsqrt(ms + EPS)) * g_ref[...]).astype(o_ref.dtype)


def rmsnorm(x, g, out_dtype, name, tm=512):
    m, d = x.shape
    return pl.pallas_call(
        _rmsnorm_kernel,
        grid=(m // tm,),
        in_specs=[pl.BlockSpec((tm, d), lambda i: (i, 0)),
                  pl.BlockSpec((1, d), lambda i: (0, 0))],
        out_specs=pl.BlockSpec((tm, d), lambda i: (i, 0)),
        out_shape=jax.ShapeDtypeStruct((m, d), out_dtype),
        compiler_params=_params("parallel"),
        name=name,
    )(x, g.reshape(1, d))


def _cast_on_first_m_step(pairs):
    @pl.when(pl.program_id(1) == 0)
    def _():
        for w_ref, wb_ref in pairs:
            wb_ref[...] = w_ref[...].astype(BF16)


def _weight_spec(k, tn, layer):
    return pl.BlockSpec((None, k, tn), lambda j, i: (layer, 0, j))


def _mm_kernel(a_ref, w_ref, o_ref):
    o_ref[...] = _dot(a_ref[...], w_ref[...]).astype(o_ref.dtype)


def matmul_bf16w(a, w, tm, tn, out_dtype, name):
    m, k = a.shape
    n = w.shape[1]
    return pl.pallas_call(
        _mm_kernel,
        grid=(n // tn, m // tm),
        in_specs=[pl.BlockSpec((tm, k), lambda j, i: (i, 0)),
                  pl.BlockSpec((k, tn), lambda j, i: (0, j))],
        out_specs=pl.BlockSpec((tm, tn), lambda j, i: (i, j)),
        out_shape=jax.ShapeDtypeStruct((m, n), out_dtype),
        compiler_params=_params("parallel", "parallel"),
        name=name,
    )(a, w)


def _mm_residual_kernel(a_ref, w_ref, r_ref, o_ref, wb_ref):
    _cast_on_first_m_step([(w_ref, wb_ref)])
    o_ref[...] = r_ref[...] + _dot(a_ref[...], wb_ref[...])


def matmul_residual(a, w_stack, layer, r, tm, tn, name):
    m, k = a.shape
    n = w_stack.shape[2]
    return pl.pallas_call(
        _mm_residual_kernel,
        grid=(n // tn, m // tm),
        in_specs=[pl.BlockSpec((tm, k), lambda j, i: (i, 0)),
                  _weight_spec(k, tn, layer),
                  pl.BlockSpec((tm, tn), lambda j, i: (i, j))],
        out_specs=pl.BlockSpec((tm, tn), lambda j, i: (i, j)),
        out_shape=jax.ShapeDtypeStruct((m, n), F32),
        scratch_shapes=[pltpu.VMEM((k, tn), BF16)],
        compiler_params=_params("parallel", "arbitrary"),
        name=name,
    )(a, w_stack, r)


def _ffn_up_kernel(a_ref, wg_ref, wu_ref, o_ref, wgb_ref, wub_ref):
    _cast_on_first_m_step([(wg_ref, wgb_ref), (wu_ref, wub_ref)])
    a = a_ref[...]
    g = _dot(a, wgb_ref[...])
    u = _dot(a, wub_ref[...])
    o_ref[...] = ((g * _sigmoid(g)) * u).astype(o_ref.dtype)


def ffn_up(a, wg_stack, wu_stack, layer, tm, tn, name):
    m, k = a.shape
    n = wg_stack.shape[2]
    return pl.pallas_call(
        _ffn_up_kernel,
        grid=(n // tn, m // tm),
        in_specs=[pl.BlockSpec((tm, k), lambda j, i: (i, 0)),
                  _weight_spec(k, tn, layer),
                  _weight_spec(k, tn, layer)],
        out_specs=pl.BlockSpec((tm, tn), lambda j, i: (i, j)),
        out_shape=jax.ShapeDtypeStruct((m, n), BF16),
        scratch_shapes=[pltpu.VMEM((k, tn), BF16), pltpu.VMEM((k, tn), BF16)],
        compiler_params=_params("parallel", "arbitrary"),
        name=name,
    )(a, wg_stack, wu_stack)


def _merge_kernel(pa_ref, yb_ref, wb_ref, ga_ref, gb_ref, o_ref, wbb_ref):
    _cast_on_first_m_step([(wb_ref, wbb_ref)])
    pb = _dot(yb_ref[...], wbb_ref[...])
    o_ref[...] = (_sigmoid(ga_ref[...]) * pa_ref[...]
                  + _sigmoid(gb_ref[...]) * pb).astype(o_ref.dtype)


def merge_branches(pa, yb, wb_stack, layer, proj, tm, tn, name):
    m = yb.shape[0]
    kb = wb_stack.shape[1]
    n = wb_stack.shape[2]
    ga0 = COL_GA // tn
    gb0 = COL_GB // tn
    return pl.pallas_call(
        _merge_kernel,
        grid=(n // tn, m // tm),
        in_specs=[pl.BlockSpec((tm, tn), lambda j, i: (i, j)),
                  pl.BlockSpec((tm, kb), lambda j, i: (i, 0)),
                  _weight_spec(kb, tn, layer),
                  pl.BlockSpec((tm, tn), lambda j, i: (i, ga0 + j)),
                  pl.BlockSpec((tm, tn), lambda j, i: (i, gb0 + j))],
        out_specs=pl.BlockSpec((tm, tn), lambda j, i: (i, j)),
        out_shape=jax.ShapeDtypeStruct((m, n), BF16),
        scratch_shapes=[pltpu.VMEM((kb, tn), BF16)],
        compiler_params=_params("parallel", "arbitrary"),
        name=name,
    )(pa, yb, wb_stack, proj, proj)


SSD_STEP_CHUNKS = 4
SSD_STEP_ROWS = SSD_STEP_CHUNKS * SSM_CHUNK
SSD_CONV_WIDTH = SSM_GROUP_WIDTH + 2 * SSM_STATE


def _softplus(x):
    return jnp.maximum(x, 0.0) + jnp.log1p(jnp.exp(-jnp.abs(x)))


def _split3(x):
    hi = x.astype(BF16)
    r1 = x - hi.astype(F32)
    mid = r1.astype(BF16)
    lo = (r1 - mid.astype(F32)).astype(BF16)
    return hi, mid, lo


def _ssd_kernel(x_ref, xh_ref, b_ref, bh_ref, c_ref, ch_ref, cw_ref, cbias_ref, z_ref,
                dtc_ref, dtr_ref, biasr_ref, biasc_ref, alogr_ref, alogc_ref, dsk_ref, ng_ref,
                u_ref, wb_ref, o_ref, pb_ref, st_ref, pad_ref, xc_ref, y_ref):
    def shadow_slice(k):
        rows_k = slice(k * SHADOW_SLICE_ROWS, (k + 1) * SHADOW_SLICE_ROWS)
        pb_ref[rows_k, :] = _dot(u_ref[rows_k, :], wb_ref[...])

    step = pl.program_id(2)

    @pl.when(step == 0)
    def _():
        st_ref[...] = jnp.zeros_like(st_ref)

    gw = SSM_GROUP_WIDTH
    n = SSM_STATE
    lc = SSM_CHUNK
    hp = SSM_HEAD_DIM
    halo = SUBLANES

    def halo_rows(h_ref):
        first = jnp.full(h_ref.shape, step, jnp.int32) == 0
        return jnp.where(first, 0.0, h_ref[...])

    pad_ref[0:halo, 0:gw] = halo_rows(xh_ref)
    pad_ref[0:halo, gw:gw + n] = halo_rows(bh_ref)
    pad_ref[0:halo, gw + n:] = halo_rows(ch_ref)
    pad_ref[halo:, 0:gw] = x_ref[...]
    pad_ref[halo:, gw:gw + n] = b_ref[...]
    pad_ref[halo:, gw + n:] = c_ref[...]
    cw = cw_ref[...]
    cbias = cbias_ref[...]
    for r0 in range(0, SSD_STEP_ROWS, lc):
        acc = cbias + cw[SSM_CONV - 1:SSM_CONV, :] * pad_ref[halo + r0:halo + r0 + lc, :]
        for back in range(1, SSM_CONV):
            lo = halo + r0 - back
            acc = acc + cw[SSM_CONV - 1 - back:SSM_CONV - back, :] * pad_ref[lo:lo + lc, :]
        xc_ref[r0:r0 + lc, :] = acc * _sigmoid(acc)

    log2e = math.log2(math.e)
    dtr_all = _softplus(dtr_ref[...] + biasc_ref[...])
    adt_r_all = dtr_all * (-log2e * jnp.exp(alogc_ref[...]))
    adt_c_all = _softplus(dtc_ref[...] + biasr_ref[...]) * (-log2e * jnp.exp(alogr_ref[...]))

    ii = lax.broadcasted_iota(jnp.int32, (lc, lc), 0)
    jj = lax.broadcasted_iota(jnp.int32, (lc, lc), 1)
    causal = jj <= ii
    tri = jnp.where(causal, 1.0, 0.0).astype(BF16)
    tri_t = jnp.where(ii <= jj, 1.0, 0.0).astype(BF16)

    for cc in range(SSD_STEP_CHUNKS):
        r0 = cc * lc
        cs_c = sum(_dot(tri, p) for p in _split3(adt_c_all[r0:r0 + lc, :]))
        cs_r = sum(_dot(p, tri_t) for p in _split3(adt_r_all[:, r0:r0 + lc]))
        ecl = jnp.exp2(cs_c[lc - 1:lc, :])
        dtr = dtr_all[:, r0:r0 + lc]
        ddr = dtr * jnp.exp2(cs_r[:, lc - 1:lc] - cs_r)

        x16 = xc_ref[r0:r0 + lc, 0:gw].astype(BF16)
        bc = xc_ref[r0:r0 + lc, gw:gw + n]
        cc32 = xc_ref[r0:r0 + lc, gw + n:]
        cb = lax.dot_general(cc32.astype(BF16), bc.astype(BF16), (((1,), (1,)), ((), ())),
                             preferred_element_type=F32)
        bt = bc.T
        for h in range(SSM_HEADS_PER_GROUP):
            if h % (SSM_HEADS_PER_GROUP // SHADOW_SLICES_PER_CHUNK) == 0:
                shadow_slice(cc * SHADOW_SLICES_PER_CHUNK
                             + h // (SSM_HEADS_PER_GROUP // SHADOW_SLICES_PER_CHUNK))
            lo = h * hp
            cs_i = jnp.broadcast_to(cs_c[:, h:h + 1], (lc, lc))
            lmat = jnp.exp2(jnp.where(causal, cs_i - cs_r[h:h + 1, :], -jnp.inf))
            intra = (cb * lmat) * dtr[h:h + 1, :]
            inter = cc32 * jnp.exp2(cs_i)
            prev = st_ref[h]
            xh = x16[:, lo:lo + hp]
            y_ref[r0:r0 + lc, lo:lo + hp] = _dot(
                jnp.concatenate([intra, inter], axis=1).astype(BF16),
                jnp.concatenate([xh, prev.astype(BF16)], axis=0))
            st_new = _dot((bt * ddr[h:h + 1, :]).astype(BF16), xh)
            st_ref[h] = prev * ecl[:, h:h + 1] + st_new

    y = y_ref[...] + dsk_ref[...] * xc_ref[:, 0:gw]
    z = z_ref[...]
    y = y * (z * _sigmoid(z))
    ms = jnp.mean(y * y, axis=-1, keepdims=True)
    o_ref[...] = ((y * lax.rsqrt(ms + EPS)) * ng_ref[...]).astype(o_ref.dtype)


SHADOW_TM = 1024
SHADOW_SLICES_PER_CHUNK = 1
SHADOW_SLICE_ROWS = SHADOW_TM // (SSD_STEP_CHUNKS * SHADOW_SLICES_PER_CHUNK)
SHADOW_TN = PROJ_B_COLS * TOKENS // SHADOW_TM // (BATCH * SSM_GROUPS * (SEQ // SSD_STEP_ROWS))


def ssd_branch(proj, dt_col, dt_row, conv_w, conv_b, dt_bias, a_log, d_skip, norm_g, u, w_b,
               name):
    rows = SSD_STEP_ROWS
    m_tiles = TOKENS // SHADOW_TM
    tile_i = lambda b, g, s: ((b * SSM_GROUPS + g) * (SEQ // rows) + s) % m_tiles
    tile_j = lambda b, g, s: ((b * SSM_GROUPS + g) * (SEQ // rows) + s) // m_tiles
    steps = SEQ // rows
    gw = SSM_GROUP_WIDTH
    n = SSM_STATE
    hg = SSM_HEADS_PER_GROUP
    ng = SSM_GROUPS
    cwid = SSD_CONV_WIDTH

    def per_group(p, lead):
        xs = p[:, :SSM_D_INNER].reshape(lead, ng, gw)
        bs = p[:, SSM_D_INNER:SSM_D_INNER + ng * n].reshape(lead, ng, n)
        cs = p[:, SSM_D_INNER + ng * n:].reshape(lead, ng, n)
        return jnp.transpose(jnp.concatenate([xs, bs, cs], axis=2), (1, 0, 2))

    row = lambda b, g, s: b * steps + s
    hrow = lambda b, g, s: jnp.maximum(row(b, g, s) * (rows // SUBLANES) - 1, 0)
    return pl.pallas_call(
        _ssd_kernel,
        grid=(BATCH, ng, steps),
        in_specs=[
            pl.BlockSpec((rows, gw), lambda b, g, s: (row(b, g, s), COL_XS // gw + g)),
            pl.BlockSpec((SUBLANES, gw), lambda b, g, s: (hrow(b, g, s), COL_XS // gw + g)),
            pl.BlockSpec((rows, n), lambda b, g, s: (row(b, g, s), COL_B // n + g)),
            pl.BlockSpec((SUBLANES, n), lambda b, g, s: (hrow(b, g, s), COL_B // n + g)),
            pl.BlockSpec((rows, n), lambda b, g, s: (row(b, g, s), COL_C // n + g)),
            pl.BlockSpec((SUBLANES, n), lambda b, g, s: (hrow(b, g, s), COL_C // n + g)),
            pl.BlockSpec((None, SSM_CONV, cwid), lambda b, g, s: (g, 0, 0)),
            pl.BlockSpec((None, 1, cwid), lambda b, g, s: (g, 0, 0)),
            pl.BlockSpec((rows, gw), lambda b, g, s: (row(b, g, s), COL_Z // gw + g)),
            pl.BlockSpec((None, None, rows, hg), lambda b, g, s: (b, g, s, 0)),
            pl.BlockSpec((hg, rows), lambda b, g, s: (g, row(b, g, s))),
            pl.BlockSpec((None, 1, hg), lambda b, g, s: (g, 0, 0)),
            pl.BlockSpec((None, hg, 1), lambda b, g, s: (g, 0, 0)),
            pl.BlockSpec((None, 1, hg), lambda b, g, s: (g, 0, 0)),
            pl.BlockSpec((None, hg, 1), lambda b, g, s: (g, 0, 0)),
            pl.BlockSpec((1, gw), lambda b, g, s: (0, g)),
            pl.BlockSpec((1, gw), lambda b, g, s: (0, g)),
            pl.BlockSpec((SHADOW_TM, D_MODEL), lambda b, g, s: (tile_i(b, g, s), 0)),
            pl.BlockSpec((D_MODEL, SHADOW_TN), lambda b, g, s: (0, tile_j(b, g, s))),
        ],
        out_specs=[pl.BlockSpec((rows, gw), lambda b, g, s: (row(b, g, s), g)),
                   pl.BlockSpec((SHADOW_TM, SHADOW_TN),
                                lambda b, g, s: (tile_i(b, g, s), tile_j(b, g, s)))],
        out_shape=[jax.ShapeDtypeStruct((TOKENS, SSM_D_INNER), BF16),
                   jax.ShapeDtypeStruct((TOKENS, PROJ_B_COLS), F32)],
        scratch_shapes=[pltpu.VMEM((hg, n, SSM_HEAD_DIM), F32),
                        pltpu.VMEM((rows + SUBLANES, cwid), F32),
                        pltpu.VMEM((rows, cwid), F32),
                        pltpu.VMEM((rows, gw), F32)],
        compiler_params=_params("parallel", "parallel", "arbitrary"),
        name=name,
    )(proj, proj, proj, proj, proj, proj,
      per_group(conv_w, SSM_CONV), per_group(conv_b.reshape(1, SSM_CONV_DIM), 1),
      proj, dt_col, dt_row,
      dt_bias.reshape(ng, 1, hg), dt_bias.reshape(ng, hg, 1),
      a_log.reshape(ng, 1, hg), a_log.reshape(ng, hg, 1),
      jnp.repeat(d_skip, SSM_HEAD_DIM).reshape(1, SSM_D_INNER),
      norm_g.reshape(1, SSM_D_INNER), u, w_b)


KEY_CHUNK = 256
PREP_ROWS = KEY_CHUNK
PREP_QBLOCKS = PREP_ROWS // Q_BLOCK
N_KEY_CHUNKS = SEQ // KEY_CHUNK
HEAD_PAIRS = ATTN_HEADS // 2
IDX_PAIRS = IDX_HEADS // 2
Q_PRESCALE = ATTN_HEAD_DIM ** -0.5 * math.log2(math.e)
W_IDX_SCALE = IDX_HEADS ** -0.5 * IDX_HEAD_DIM ** -0.5
M_INIT = -1e30
PA_SLICES = 8
PA_SLICE_COLS = D_MODEL // PA_SLICES


def _rope(x, c, sn, sp, half):
    return x * c + pltpu.roll(x, LANES - half, 1) * sn + pltpu.roll(x, half, 1) * sp


def _prep_kernel(q_ref, k_ref, v_ref, qi_ref, smk_ref, smw_ref, ca_ref, sna_ref, spa_ref,
                 ci_ref, sni_ref, spi_ref, gk_ref,
                 qt_ref, ko_ref, vt_ref, qit_ref, kio_ref, wt_ref):
    ca, sna, spa = ca_ref[...], sna_ref[...], spa_ref[...]
    ci, sni, spi = ci_ref[...], sni_ref[...], spi_ref[...]
    ha = ATTN_ROT_DIM // 2
    hi = IDX_ROT_DIM // 2
    hd = ATTN_HEAD_DIM
    zeros_half = jnp.zeros((LANES - IDX_HEAD_DIM, Q_BLOCK), F32)
    for qq in range(PREP_QBLOCKS):
        rows = slice(qq * Q_BLOCK, (qq + 1) * Q_BLOCK)
        for h in range(ATTN_HEADS):
            qr = _rope(q_ref[rows, h * hd:(h + 1) * hd], ca[rows], sna[rows], spa[rows], ha)
            qt_ref[qq, h // 2, :, (h % 2) * Q_BLOCK:(h % 2 + 1) * Q_BLOCK] = (
                (qr * Q_PRESCALE).T.astype(BF16))
        for j in range(IDX_PAIRS):
            xt = _rope(qi_ref[rows, j * LANES:(j + 1) * LANES],
                       ci[rows], sni[rows], spi[rows], hi).T
            even = jnp.concatenate([xt[:IDX_HEAD_DIM], zeros_half], axis=0)
            odd = jnp.concatenate([xt[IDX_HEAD_DIM:], zeros_half], axis=0)
            qit_ref[qq, j, :, 0:Q_BLOCK] = even.astype(BF16)
            qit_ref[qq, j, :, Q_BLOCK:] = odd.astype(BF16)
    for g in range(ATTN_KV_HEADS):
        cols = slice(g * hd, (g + 1) * hd)
        ko_ref[:, cols] = _rope(k_ref[:, cols], ca, sna, spa, ha).astype(BF16)
        vt_ref[g, 0] = v_ref[:, cols].T.astype(BF16)
    xk = smk_ref[...]
    ms = jnp.sum(xk * xk, axis=-1, keepdims=True) * (1.0 / IDX_HEAD_DIM)
    xk = (xk * lax.rsqrt(ms + EPS)) * gk_ref[...]
    kio_ref[...] = _rope(xk, ci, sni, spi, hi).astype(BF16)
    smt = smw_ref[...].T
    w0 = SMALL_W - SMALL_DT
    for qq in range(PREP_QBLOCKS):
        wt_ref[qq] = smt[w0:w0 + IDX_HEADS, qq * Q_BLOCK:(qq + 1) * Q_BLOCK] * W_IDX_SCALE


def dsa_prep(proj, small, tabs, idx_k_norm, name):
    r = PREP_ROWS
    nqb = PREP_QBLOCKS
    gk = jnp.concatenate([idx_k_norm, jnp.zeros((LANES - IDX_HEAD_DIM,), F32)]).reshape(1, LANES)
    tab_spec = pl.BlockSpec((r, LANES), lambda i: (i, 0))
    return pl.pallas_call(
        _prep_kernel,
        grid=(TOKENS // r,),
        in_specs=[
            pl.BlockSpec((r, ATTN_WIDTH), lambda i: (i, COL_Q // ATTN_WIDTH)),
            pl.BlockSpec((r, ATTN_KV_WIDTH), lambda i: (i, COL_K // ATTN_KV_WIDTH)),
            pl.BlockSpec((r, ATTN_KV_WIDTH), lambda i: (i, COL_V // ATTN_KV_WIDTH)),
            pl.BlockSpec((r, IDX_WIDTH), lambda i: (i, COL_QI // IDX_WIDTH)),
            pl.BlockSpec((r, LANES), lambda i: (i, 0)),
            pl.BlockSpec((r, LANES), lambda i: (i, SMALL_DT // LANES)),
            tab_spec, tab_spec, tab_spec, tab_spec, tab_spec, tab_spec,
            pl.BlockSpec((1, LANES), lambda i: (0, 0)),
        ],
        out_specs=[
            pl.BlockSpec((nqb, HEAD_PAIRS, ATTN_HEAD_DIM, 2 * Q_BLOCK), lambda i: (i, 0, 0, 0)),
            pl.BlockSpec((r, ATTN_KV_WIDTH), lambda i: (i, 0)),
            pl.BlockSpec((ATTN_KV_HEADS, 1, ATTN_HEAD_DIM, r), lambda i: (0, i, 0, 0)),
            pl.BlockSpec((nqb, IDX_PAIRS, LANES, 2 * Q_BLOCK), lambda i: (i, 0, 0, 0)),
            pl.BlockSpec((r, LANES), lambda i: (i, 0)),
            pl.BlockSpec((nqb, IDX_HEADS, Q_BLOCK), lambda i: (i, 0, 0)),
        ],
        out_shape=[
            jax.ShapeDtypeStruct((TOKENS // Q_BLOCK, HEAD_PAIRS, ATTN_HEAD_DIM, 2 * Q_BLOCK), BF16),
            jax.ShapeDtypeStruct((TOKENS, ATTN_KV_WIDTH), BF16),
            jax.ShapeDtypeStruct((ATTN_KV_HEADS, TOKENS // r, ATTN_HEAD_DIM, r), BF16),
            jax.ShapeDtypeStruct((TOKENS // Q_BLOCK, IDX_PAIRS, LANES, 2 * Q_BLOCK), BF16),
            jax.ShapeDtypeStruct((TOKENS, LANES), BF16),
            jax.ShapeDtypeStruct((TOKENS // Q_BLOCK, IDX_HEADS, Q_BLOCK), F32),
        ],
        compiler_params=_params("parallel"),
        name=name,
    )(proj, proj, proj, proj, small, small, *tabs, gk)


def _col_reduce8(x, op):
    rows, lanes = x.shape
    g = x.reshape(4, rows // (4 * SUBLANES), SUBLANES, lanes)
    p = op(g, axis=1)
    red = jnp.maximum if op is jnp.max else jnp.add
    return red(red(p[0], p[1]), red(p[2], p[3]))


def _dsa_kernel(qt_ref, k_ref, vt_ref, qit_ref, ki_ref, wt_ref, ya_ref, wpa_ref, o_ref, pa_ref,
                key_ref, madd_ref, acc_ref, m_ref, l_ref, alpha_ref, s_ref, pe_ref, thr_ref):
    tq = Q_BLOCK
    kc = KEY_CHUNK
    hd = ATTN_HEAD_DIM
    qb = pl.program_id(1)
    n_chunks = qb // (kc // tq) + 1
    neg_inf = -jnp.inf
    q_pos = qb * tq + lax.broadcasted_iota(jnp.int32, (kc, tq), 1)
    k_off = lax.broadcasted_iota(jnp.int32, (kc, tq), 0)

    def causal(c):
        return c * kc + k_off <= q_pos

    def key_rows(c):
        return pl.ds(pl.multiple_of(c * kc, kc), kc)

    wt = wt_ref[...]

    def score_chunk(c, carry):
        ki = ki_ref[key_rows(c), :]
        score = jnp.zeros((kc, tq), F32)
        for j in range(IDX_PAIRS):
            lt = _dot(ki, qit_ref[j])
            score = score + jnp.maximum(lt[:, :tq], 0.0) * wt[2 * j:2 * j + 1, :]
            score = score + jnp.maximum(lt[:, tq:], 0.0) * wt[2 * j + 1:2 * j + 2, :]
        score = jnp.where(score == 0.0, 0.0, score)
        bits = lax.bitcast_convert_type(jnp.where(causal(c), score, neg_inf), jnp.int32)
        key_ref[c] = jnp.where(bits >= 0, bits, bits ^ jnp.int32(0x7FFFFFFF))
        return carry

    lax.fori_loop(0, n_chunks, score_chunk, 0)

    def pa_slice(k):
        cols = slice(k * PA_SLICE_COLS, (k + 1) * PA_SLICE_COLS)
        pa_ref[:, cols] = _dot(ya_ref[...], wpa_ref[:, cols])

    def radix_select(chunks_used):
        def pad_chunk(c, carry):
            key_ref[c] = jnp.full((kc, tq), INT_MIN, jnp.int32)
            return carry

        lax.fori_loop(n_chunks, chunks_used, pad_chunk, 0)
        t = jnp.zeros((1, tq), jnp.int32)
        for i in range(32):
            cand = t | lax.shift_left(jnp.int32(1), 31 - i)
            cand_key = cand ^ jnp.int32(INT_MIN)
            cnt8 = jnp.zeros((SUBLANES, tq), F32)
            for c in range(chunks_used):
                hit = jnp.where(key_ref[c] >= cand_key, 1.0, 0.0)
                cnt8 = cnt8 + _col_reduce8(hit, jnp.sum)
            cnt = jnp.sum(cnt8, axis=0, keepdims=True)
            t = jnp.where(cnt >= float(TOP_K), cand, t)
            if i % (32 // PA_SLICES) == 0:
                pa_slice(i // (32 // PA_SLICES))
        thr_ref[...] = t ^ jnp.int32(INT_MIN)

    half_chunks = N_KEY_CHUNKS // 2

    @pl.when(n_chunks <= half_chunks)
    def _():
        radix_select(half_chunks)

    @pl.when(n_chunks > half_chunks)
    def _():
        radix_select(N_KEY_CHUNKS)

    thr = thr_ref[...]

    def mask_chunk(c, cnt):
        ge = key_ref[c] >= thr
        ok = causal(c)
        madd_ref[c] = jnp.where(ok, jnp.where(ge, 0.0, neg_inf), neg_inf)
        return cnt + jnp.sum(jnp.where(ok, jnp.where(ge, 1.0, 0.0), 0.0), axis=0, keepdims=True)

    cnt_ge = lax.fori_loop(0, n_chunks, mask_chunk, jnp.zeros((1, tq), F32))

    @pl.when(jnp.max(cnt_ge) > float(TOP_K))
    def _():
        def count_gt(c, cnt):
            return cnt + jnp.sum(jnp.where(key_ref[c] > thr, 1.0, 0.0), axis=0, keepdims=True)

        need = float(TOP_K) - lax.fori_loop(0, n_chunks, count_gt, jnp.zeros((1, tq), F32))
        a_i = lax.broadcasted_iota(jnp.int32, (kc, kc), 0)
        a_j = lax.broadcasted_iota(jnp.int32, (kc, kc), 1)
        earlier = jnp.where(a_j < a_i, 1.0, 0.0).astype(BF16)

        def tie_chunk(c, run):
            key = key_ref[c]
            gt = key > thr
            eq = key == thr
            e = jnp.where(eq, 1.0, 0.0)
            rank = _dot(earlier, e.astype(BF16)) + run
            keep_eq = jnp.where(rank < need, 0.0, neg_inf)
            val = jnp.where(gt, 0.0, jnp.where(eq, keep_eq, neg_inf))
            madd_ref[c] = jnp.where(causal(c), val, neg_inf)
            return run + jnp.sum(e, axis=0, keepdims=True)

        lax.fori_loop(0, n_chunks, tie_chunk, jnp.zeros((1, tq), F32))

    m_ref[...] = jnp.full(m_ref.shape, M_INIT, F32)
    l_ref[...] = jnp.zeros(l_ref.shape, F32)
    acc_ref[...] = jnp.zeros(acc_ref.shape, F32)

    def attn_chunk(c, carry):
        half_rows = kc // 2
        for p in range(HEAD_PAIRS):
            g = p // 2
            top8 = None
            for rh in range(2):
                r0 = pl.multiple_of(c * kc + rh * half_rows, half_rows)
                sub = slice(rh * half_rows, (rh + 1) * half_rows)
                s = _dot(k_ref[pl.ds(r0, half_rows), g * hd:(g + 1) * hd], qt_ref[p])
                madd = madd_ref[c, sub, :]
                s = s + jnp.concatenate([madd, madd], axis=1)
                s_ref[p, sub, :] = s
                t8 = _col_reduce8(s, jnp.max)
                top8 = t8 if top8 is None else jnp.maximum(top8, t8)
            m_old = m_ref[p]
            m_new = jnp.maximum(m_old, jnp.max(top8, axis=0, keepdims=True))
            m_ref[p] = m_new
            alpha_ref[p] = jnp.exp2(m_old - m_new)
        for p in range(HEAD_PAIRS):
            sum8 = None
            for rh in range(2):
                sub = slice(rh * half_rows, (rh + 1) * half_rows)
                pe = jnp.exp2(s_ref[p, sub, :] - m_ref[p])
                pe_ref[p, sub, :] = pe.astype(BF16)
                s8 = _col_reduce8(pe, jnp.sum)
                sum8 = s8 if sum8 is None else sum8 + s8
            l_ref[p] = l_ref[p] * alpha_ref[p] + jnp.sum(sum8, axis=0, keepdims=True)
        for p in range(HEAD_PAIRS):
            acc_ref[p] = acc_ref[p] * alpha_ref[p] + _dot(vt_ref[p // 2, c], pe_ref[p])
        return carry

    lax.fori_loop(0, n_chunks, attn_chunk, 0)

    for p in range(HEAD_PAIRS):
        o = acc_ref[p] / l_ref[p]
        for half in range(2):
            h = 2 * p + half
            o_ref[:, h * hd:(h + 1) * hd] = o[:, half * tq:(half + 1) * tq].T.astype(o_ref.dtype)


def dsa_attention(q_t, k_r, v_t, qi_t, ki, w_t, y_a, w_pa, name):
    nq = SEQ // Q_BLOCK
    row = lambda b, i: b * nq + i
    return pl.pallas_call(
        _dsa_kernel,
        grid=(BATCH, nq),
        in_specs=[
            pl.BlockSpec((None, HEAD_PAIRS, ATTN_HEAD_DIM, 2 * Q_BLOCK),
                         lambda b, i: (row(b, i), 0, 0, 0)),
            pl.BlockSpec((SEQ, ATTN_KV_WIDTH), lambda b, i: (b, 0)),
            pl.BlockSpec((ATTN_KV_HEADS, N_KEY_CHUNKS, ATTN_HEAD_DIM, KEY_CHUNK),
                         lambda b, i: (0, b, 0, 0)),
            pl.BlockSpec((None, IDX_PAIRS, LANES, 2 * Q_BLOCK), lambda b, i: (row(b, i), 0, 0, 0)),
            pl.BlockSpec((SEQ, LANES), lambda b, i: (b, 0)),
            pl.BlockSpec((None, IDX_HEADS, Q_BLOCK), lambda b, i: (row(b, i), 0, 0)),
            pl.BlockSpec((Q_BLOCK, SSM_D_INNER), lambda b, i: (row(b, i), 0)),
            pl.BlockSpec((SSM_D_INNER, D_MODEL), lambda b, i: (0, 0),
                         pipeline_mode=pl.Buffered(1)),
        ],
        out_specs=[pl.BlockSpec((Q_BLOCK, ATTN_WIDTH), lambda b, i: (row(b, i), 0)),
                   pl.BlockSpec((Q_BLOCK, D_MODEL), lambda b, i: (row(b, i), 0))],
        out_shape=[jax.ShapeDtypeStruct((TOKENS, ATTN_WIDTH), BF16),
                   jax.ShapeDtypeStruct((TOKENS, D_MODEL), F32)],
        scratch_shapes=[pltpu.VMEM((N_KEY_CHUNKS, KEY_CHUNK, Q_BLOCK), jnp.int32),
                        pltpu.VMEM((N_KEY_CHUNKS, KEY_CHUNK, Q_BLOCK), F32),
                        pltpu.VMEM((HEAD_PAIRS, ATTN_HEAD_DIM, 2 * Q_BLOCK), F32),
                        pltpu.VMEM((HEAD_PAIRS, 1, 2 * Q_BLOCK), F32),
                        pltpu.VMEM((HEAD_PAIRS, 1, 2 * Q_BLOCK), F32),
                        pltpu.VMEM((HEAD_PAIRS, 1, 2 * Q_BLOCK), F32),
                        pltpu.VMEM((HEAD_PAIRS, KEY_CHUNK, 2 * Q_BLOCK), F32),
                        pltpu.VMEM((HEAD_PAIRS, KEY_CHUNK, 2 * Q_BLOCK), BF16),
                        pltpu.VMEM((1, Q_BLOCK), jnp.int32)],
        compiler_params=_params("parallel", "arbitrary"),
        name=name,
    )(q_t, k_r, v_t, qi_t, ki, w_t, y_a, w_pa)


def _rope_lane_tables(positions):
    pos = positions.astype(F32).reshape(TOKENS, 1)

    def tables(rot_dim, width):
        inv_freq = ROPE_THETA ** (-(jnp.arange(0, rot_dim, 2, dtype=F32) / rot_dim))
        ang = pos * inv_freq
        cos, sin = jnp.cos(ang), jnp.sin(ang)
        half = rot_dim // 2
        ones = jnp.ones((TOKENS, width - rot_dim), F32)
        zeros = lambda n: jnp.zeros((TOKENS, n), F32)
        c = jnp.concatenate([cos, cos, ones], axis=1)
        sn = jnp.concatenate([-sin, zeros(width - half)], axis=1)
        sp = jnp.concatenate([zeros(half), sin, zeros(width - rot_dim)], axis=1)
        reps = LANES // width
        return tuple(jnp.tile(t, (1, reps)) for t in (c, sn, sp))

    return tables(ATTN_ROT_DIM, ATTN_HEAD_DIM) + tables(IDX_ROT_DIM, IDX_HEAD_DIM)


def _regroup_w_in(w):
    sizes = (SSM_HEADS, ATTN_WIDTH, ATTN_KV_WIDTH, ATTN_KV_WIDTH, IDX_WIDTH, IDX_HEAD_DIM,
             IDX_HEADS, D_MODEL, D_MODEL)
    parts = []
    off = PROJ_A_COLS
    for s in sizes:
        parts.append(w[:, off:off + s])
        off += s
    dt, q, k, v, qi, ki, wi, ga, gb = parts
    w_a = w[:, :PROJ_A_COLS].astype(BF16)
    w_b = jnp.concatenate([q, k, v, qi, ga, gb], axis=1).astype(BF16)
    zc = lambda n: jnp.zeros((D_MODEL, n), F32)
    small = jnp.concatenate(
        [ki, zc(LANES - IDX_HEAD_DIM), dt, wi, zc(SMALL_COLS - SMALL_W - IDX_HEADS)],
        axis=1).astype(BF16)
    return w_a, w_b, small


def kernel(x, positions, mix_norm, w_in, conv_w, conv_b, dt_bias, a_log, d_skip, ssm_norm,
           idx_k_norm, w_proj_a, w_proj_b, w_out, ffn_norm, w_ffn_gate, w_ffn_up, w_ffn_down,
           final_norm):
    tabs = _rope_lane_tables(positions)
    xf = x.reshape(TOKENS, D_MODEL)
    for i in range(DEPTH):
        w_a, w_b, w_small = _regroup_w_in(w_in[i])
        u = rmsnorm(xf, mix_norm[i], BF16, f"mix_norm_{i}")
        proj_a = matmul_bf16w(u, w_a, 1024, 1024, F32, f"in_proj_a_{i}")
        small = matmul_bf16w(u, w_small, 1024, SMALL_COLS, F32, f"in_proj_small_{i}")

        dt_raw = small[:, SMALL_DT:SMALL_W]
        dt_col = jnp.transpose(
            dt_raw.reshape(BATCH, SEQ, SSM_GROUPS, SSM_HEADS_PER_GROUP), (0, 2, 1, 3))
        y_a, proj_b = ssd_branch(proj_a, dt_col, dt_raw.T, conv_w[i], conv_b[i], dt_bias[i],
                                 a_log[i], d_skip[i], ssm_norm[i], u, w_b, f"ssd_{i}")

        q_t, k_r, v_t, qi_t, ki, w_t = dsa_prep(proj_b, small, tabs, idx_k_norm[i],
                                                f"dsa_prep_{i}")
        y_b, pa = dsa_attention(q_t, k_r, v_t, qi_t, ki, w_t, y_a, w_proj_a[i].astype(BF16),
                                f"dsa_{i}")

        merged = merge_branches(pa, y_b, w_proj_b, i, proj_b, 512, 1024, f"merge_{i}")
        xf = matmul_residual(merged, w_out, i, xf, 1024, 1024, f"out_proj_{i}")

        h = rmsnorm(xf, ffn_norm[i], BF16, f"ffn_norm_{i}")
        ff = ffn_up(h, w_ffn_gate, w_ffn_up, i, 1024, 512, f"ffn_up_{i}")
        xf = matmul_residual(ff, w_ffn_down, i, xf, 512, 512, f"ffn_down_{i}")
    out = rmsnorm(xf, final_norm, F32, "final_norm")
    return out.reshape(BATCH, SEQ, D_MODEL)
```

```python
import math

import jax
import jax.numpy as jnp
from jax import lax
from jax.experimental import pallas as pl
from jax.experimental.pallas import tpu as pltpu

F32 = jnp.float32
BF16 = jnp.bfloat16

D_MODEL = 2048
BATCH = 4
SEQ = 2048
DEPTH = 4
TOKENS = BATCH * SEQ
EPS = 1e-6

SSM_D_INNER = 4096
SSM_HEAD_DIM = 64
SSM_HEADS = 64
SSM_GROUPS = 8
SSM_HEADS_PER_GROUP = SSM_HEADS // SSM_GROUPS
SSM_GROUP_WIDTH = SSM_D_INNER // SSM_GROUPS
SSM_STATE = 128
SSM_CONV = 4
SSM_CHUNK = 128
SSM_CONV_DIM = SSM_D_INNER + 2 * SSM_GROUPS * SSM_STATE

ATTN_HEADS = 16
ATTN_HEAD_DIM = 128
ATTN_KV_HEADS = 4
ATTN_WIDTH = ATTN_HEADS * ATTN_HEAD_DIM
ATTN_KV_WIDTH = ATTN_KV_HEADS * ATTN_HEAD_DIM
IDX_HEADS = 16
IDX_HEAD_DIM = 64
IDX_WIDTH = IDX_HEADS * IDX_HEAD_DIM
TOP_K = min(256, SEQ // 4)
Q_BLOCK = 128
ROPE_THETA = 500000.0
ATTN_ROT_DIM = ATTN_HEAD_DIM // 4
IDX_ROT_DIM = IDX_HEAD_DIM // 4
FFN_HIDDEN = 5632

COL_Z = 0
COL_XS = COL_Z + SSM_D_INNER
COL_B = COL_XS + SSM_D_INNER
COL_C = COL_B + SSM_GROUPS * SSM_STATE
PROJ_A_COLS = COL_C + SSM_GROUPS * SSM_STATE
COL_Q = 0
COL_K = COL_Q + ATTN_WIDTH
COL_V = COL_K + ATTN_KV_WIDTH
COL_QI = COL_V + ATTN_KV_WIDTH
COL_GA = COL_QI + IDX_WIDTH
COL_GB = COL_GA + D_MODEL
PROJ_B_COLS = COL_GB + D_MODEL
SMALL_COLS = 256
SMALL_DT = 128
SMALL_W = SMALL_DT + SSM_HEADS

LANES = 128
SUBLANES = 8
VMEM_LIMIT_BYTES = 56 * 1024 * 1024

INT_MIN = -(2 ** 31)


def _params(*sem):
    return pltpu.CompilerParams(dimension_semantics=sem, vmem_limit_bytes=VMEM_LIMIT_BYTES)


def _dot(a, b):
    return jnp.dot(a, b, preferred_element_type=F32)


def _sigmoid(x):
    return 1.0 / (1.0 + jnp.exp2(x * (-math.log2(math.e))))


def _rmsnorm_kernel(x_ref, g_ref, o_ref):
    x = x_ref[...]
    ms = jnp.mean(x * x, axis=-1, keepdims=True)
    o_ref[...] = ((x * lax.rsqrt(ms + EPS)) * g_ref[...]).astype(o_ref.dtype)


def rmsnorm(x, g, out_dtype, name, tm=512):
    m, d = x.shape
    return pl.pallas_call(
        _rmsnorm_kernel,
        grid=(m // tm,),
        in_specs=[pl.BlockSpec((tm, d), lambda i: (i, 0)),
                  pl.BlockSpec((1, d), lambda i: (0, 0))],
        out_specs=pl.BlockSpec((tm, d), lambda i: (i, 0)),
        out_shape=jax.ShapeDtypeStruct((m, d), out_dtype),
        compiler_params=_params("parallel"),
        name=name,
    )(x, g.reshape(1, d))


def _cast_on_first_m_step(pairs):
    @pl.when(pl.program_id(1) == 0)
    def _():
        for w_ref, wb_ref in pairs:
            wb_ref[...] = w_ref[...].astype(BF16)


def _weight_spec(k, tn, layer):
    return pl.BlockSpec((None, k, tn), lambda j, i: (layer, 0, j))


def _mm_kernel(a_ref, w_ref, o_ref):
    o_ref[...] = _dot(a_ref[...], w_ref[...]).astype(o_ref.dtype)


def matmul_bf16w(a, w, tm, tn, out_dtype, name):
    m, k = a.shape
    n = w.shape[1]
    return pl.pallas_call(
        _mm_kernel,
        grid=(n // tn, m // tm),
        in_specs=[pl.BlockSpec((tm, k), lambda j, i: (i, 0)),
                  pl.BlockSpec((k, tn), lambda j, i: (0, j))],
        out_specs=pl.BlockSpec((tm, tn), lambda j, i: (i, j)),
        out_shape=jax.ShapeDtypeStruct((m, n), out_dtype),
        compiler_params=_params("parallel", "parallel"),
        name=name,
    )(a, w)


def _mm_residual_kernel(a_ref, w_ref, r_ref, o_ref, wb_ref):
    _cast_on_first_m_step([(w_ref, wb_ref)])
    o_ref[...] = r_ref[...] + _dot(a_ref[...], wb_ref[...])


def matmul_residual(a, w_stack, layer, r, tm, tn, name):
    m, k = a.shape
    n = w_stack.shape[2]
    return pl.pallas_call(
        _mm_residual_kernel,
        grid=(n // tn, m // tm),
        in_specs=[pl.BlockSpec((tm, k), lambda j, i: (i, 0)),
                  _weight_spec(k, tn, layer),
                  pl.BlockSpec((tm, tn), lambda j, i: (i, j))],
        out_specs=pl.BlockSpec((tm, tn), lambda j, i: (i, j)),
        out_shape=jax.ShapeDtypeStruct((m, n), F32),
        scratch_shapes=[pltpu.VMEM((k, tn), BF16)],
        compiler_params=_params("parallel", "arbitrary"),
        name=name,
    )(a, w_stack, r)


def _ffn_up_kernel(a_ref, wg_ref, wu_ref, o_ref, wgb_ref, wub_ref):
    _cast_on_first_m_step([(wg_ref, wgb_ref), (wu_ref, wub_ref)])
    a = a_ref[...]
    g = _dot(a, wgb_ref[...])
    u = _dot(a, wub_ref[...])
    o_ref[...] = ((g * _sigmoid(g)) * u).astype(o_ref.dtype)


def ffn_up(a, wg_stack, wu_stack, layer, tm, tn, name):
    m, k = a.shape
    n = wg_stack.shape[2]
    return pl.pallas_call(
        _ffn_up_kernel,
        grid=(n // tn, m // tm),
        in_specs=[pl.BlockSpec((tm, k), lambda j, i: (i, 0)),
                  _weight_spec(k, tn, layer),
                  _weight_spec(k, tn, layer)],
        out_specs=pl.BlockSpec((tm, tn), lambda j, i: (i, j)),
        out_shape=jax.ShapeDtypeStruct((m, n), BF16),
        scratch_shapes=[pltpu.VMEM((k, tn), BF16), pltpu.VMEM((k, tn), BF16)],
        compiler_params=_params("parallel", "arbitrary"),
        name=name,
    )(a, wg_stack, wu_stack)


def _merge_kernel(pa_ref, yb_ref, wb_ref, ga_ref, gb_ref, o_ref, wbb_ref):
    _cast_on_first_m_step([(wb_ref, wbb_ref)])
    pb = _dot(yb_ref[...], wbb_ref[...])
    o_ref[...] = (_sigmoid(ga_ref[...]) * pa_ref[...]
                  + _sigmoid(gb_ref[...]) * pb).astype(o_ref.dtype)


def merge_branches(pa, yb, wb_stack, layer, proj, tm, tn, name):
    m = yb.shape[0]
    kb = wb_stack.shape[1]
    n = wb_stack.shape[2]
    ga0 = COL_GA // tn
    gb0 = COL_GB // tn
    return pl.pallas_call(
        _merge_kernel,
        grid=(n // tn, m // tm),
        in_specs=[pl.BlockSpec((tm, tn), lambda j, i: (i, j)),
                  pl.BlockSpec((tm, kb), lambda j, i: (i, 0)),
                  _weight_spec(kb, tn, layer),
                  pl.BlockSpec((tm, tn), lambda j, i: (i, ga0 + j)),
                  pl.BlockSpec((tm, tn), lambda j, i: (i, gb0 + j))],
        out_specs=pl.BlockSpec((tm, tn), lambda j, i: (i, j)),
        out_shape=jax.ShapeDtypeStruct((m, n), BF16),
        scratch_shapes=[pltpu.VMEM((kb, tn), BF16)],
        compiler_params=_params("parallel", "arbitrary"),
        name=name,
    )(pa, yb, wb_stack, proj, proj)


SSD_STEP_CHUNKS = 4
SSD_STEP_ROWS = SSD_STEP_CHUNKS * SSM_CHUNK
SSD_CONV_WIDTH = SSM_GROUP_WIDTH + 2 * SSM_STATE


def _softplus(x):
    return jnp.maximum(x, 0.0) + jnp.log1p(jnp.exp(-jnp.abs(x)))


def _split3(x):
    hi = x.astype(BF16)
    r1 = x - hi.astype(F32)
    mid = r1.astype(BF16)
    lo = (r1 - mid.astype(F32)).astype(BF16)
    return hi, mid, lo


def _ssd_kernel(x_ref, xh_ref, b_ref, bh_ref, c_ref, ch_ref, cw_ref, cbias_ref, z_ref,
                dtc_ref, dtr_ref, biasr_ref, biasc_ref, alogr_ref, alogc_ref, dsk_ref, ng_ref,
                u_ref, wb_ref, o_ref, pb_ref, st_ref, pad_ref, xc_ref, y_ref):
    def shadow_slice(k):
        rows_k = slice(k * SHADOW_SLICE_ROWS, (k + 1) * SHADOW_SLICE_ROWS)
        pb_ref[rows_k, :] = _dot(u_ref[rows_k, :], wb_ref[...])

    step = pl.program_id(2)

    @pl.when(step == 0)
    def _():
        st_ref[...] = jnp.zeros_like(st_ref)

    gw = SSM_GROUP_WIDTH
    n = SSM_STATE
    lc = SSM_CHUNK
    hp = SSM_HEAD_DIM
    halo = SUBLANES

    def halo_rows(h_ref):
        first = jnp.full(h_ref.shape, step, jnp.int32) == 0
        return jnp.where(first, 0.0, h_ref[...])

    pad_ref[0:halo, 0:gw] = halo_rows(xh_ref)
    pad_ref[0:halo, gw:gw + n] = halo_rows(bh_ref)
    pad_ref[0:halo, gw + n:] = halo_rows(ch_ref)
    pad_ref[halo:, 0:gw] = x_ref[...]
    pad_ref[halo:, gw:gw + n] = b_ref[...]
    pad_ref[halo:, gw + n:] = c_ref[...]
    cw = cw_ref[...]
    cbias = cbias_ref[...]
    for r0 in range(0, SSD_STEP_ROWS, lc):
        acc = cbias + cw[SSM_CONV - 1:SSM_CONV, :] * pad_ref[halo + r0:halo + r0 + lc, :]
        for back in range(1, SSM_CONV):
            lo = halo + r0 - back
            acc = acc + cw[SSM_CONV - 1 - back:SSM_CONV - back, :] * pad_ref[lo:lo + lc, :]
        xc_ref[r0:r0 + lc, :] = acc * _sigmoid(acc)

    log2e = math.log2(math.e)
    dtr_all = _softplus(dtr_ref[...] + biasc_ref[...])
    adt_r_all = dtr_all * (-log2e * jnp.exp(alogc_ref[...]))
    adt_c_all = _softplus(dtc_ref[...] + biasr_ref[...]) * (-log2e * jnp.exp(alogr_ref[...]))

    ii = lax.broadcasted_iota(jnp.int32, (lc, lc), 0)
    jj = lax.broadcasted_iota(jnp.int32, (lc, lc), 1)
    causal = jj <= ii
    tri = jnp.where(causal, 1.0, 0.0).astype(BF16)
    tri_t = jnp.where(ii <= jj, 1.0, 0.0).astype(BF16)

    for cc in range(SSD_STEP_CHUNKS):
        r0 = cc * lc
        cs_c = sum(_dot(tri, p) for p in _split3(adt_c_all[r0:r0 + lc, :]))
        cs_r = sum(_dot(p, tri_t) for p in _split3(adt_r_all[:, r0:r0 + lc]))
        ecl = jnp.exp2(cs_c[lc - 1:lc, :])
        dtr = dtr_all[:, r0:r0 + lc]
        ddr = dtr * jnp.exp2(cs_r[:, lc - 1:lc] - cs_r)

        x16 = xc_ref[r0:r0 + lc, 0:gw].astype(BF16)
        bc = xc_ref[r0:r0 + lc, gw:gw + n]
        cc32 = xc_ref[r0:r0 + lc, gw + n:]
        cb = lax.dot_general(cc32.astype(BF16), bc.astype(BF16), (((1,), (1,)), ((), ())),
                             preferred_element_type=F32)
        bt = bc.T
        for h in range(SSM_HEADS_PER_GROUP):
            if h % (SSM_HEADS_PER_GROUP // SHADOW_SLICES_PER_CHUNK) == 0:
                shadow_slice(cc * SHADOW_SLICES_PER_CHUNK
                             + h // (SSM_HEADS_PER_GROUP // SHADOW_SLICES_PER_CHUNK))
            lo = h * hp
            cs_i = jnp.broadcast_to(cs_c[:, h:h + 1], (lc, lc))
            lmat = jnp.exp2(jnp.where(causal, cs_i - cs_r[h:h + 1, :], -jnp.inf))
            intra = (cb * lmat) * dtr[h:h + 1, :]
            inter = cc32 * jnp.exp2(cs_i)
            prev = st_ref[h]
            xh = x16[:, lo:lo + hp]
            y_ref[r0:r0 + lc, lo:lo + hp] = _dot(
                jnp.concatenate([intra, inter], axis=1).astype(BF16),
                jnp.concatenate([xh, prev.astype(BF16)], axis=0))
            st_new = _dot((bt * ddr[h:h + 1, :]).astype(BF16), xh)
            st_ref[h] = prev * ecl[:, h:h + 1] + st_new

    y = y_ref[...] + dsk_ref[...] * xc_ref[:, 0:gw]
    z = z_ref[...]
    y = y * (z * _sigmoid(z))
    ms = jnp.mean(y * y, axis=-1, keepdims=True)
    o_ref[...] = ((y * lax.rsqrt(ms + EPS)) * ng_ref[...]).astype(o_ref.dtype)


SHADOW_TM = 1024
SHADOW_SLICES_PER_CHUNK = 1
SHADOW_SLICE_ROWS = SHADOW_TM // (SSD_STEP_CHUNKS * SHADOW_SLICES_PER_CHUNK)
SHADOW_TN = PROJ_B_COLS * TOKENS // SHADOW_TM // (BATCH * SSM_GROUPS * (SEQ // SSD_STEP_ROWS))


def ssd_branch(proj, dt_col, dt_row, conv_w, conv_b, dt_bias, a_log, d_skip, norm_g, u, w_b,
               name):
    rows = SSD_STEP_ROWS
    m_tiles = TOKENS // SHADOW_TM
    tile_i = lambda b, g, s: ((b * SSM_GROUPS + g) * (SEQ // rows) + s) % m_tiles
    tile_j = lambda b, g, s: ((b * SSM_GROUPS + g) * (SEQ // rows) + s) // m_tiles
    steps = SEQ // rows
    gw = SSM_GROUP_WIDTH
    n = SSM_STATE
    hg = SSM_HEADS_PER_GROUP
    ng = SSM_GROUPS
    cwid = SSD_CONV_WIDTH

    def per_group(p, lead):
        xs = p[:, :SSM_D_INNER].reshape(lead, ng, gw)
        bs = p[:, SSM_D_INNER:SSM_D_INNER + ng * n].reshape(lead, ng, n)
        cs = p[:, SSM_D_INNER + ng * n:].reshape(lead, ng, n)
        return jnp.transpose(jnp.concatenate([xs, bs, cs], axis=2), (1, 0, 2))

    row = lambda b, g, s: b * steps + s
    hrow = lambda b, g, s: jnp.maximum(row(b, g, s) * (rows // SUBLANES) - 1, 0)
    return pl.pallas_call(
        _ssd_kernel,
        grid=(BATCH, ng, steps),
        in_specs=[
            pl.BlockSpec((rows, gw), lambda b, g, s: (row(b, g, s), COL_XS // gw + g)),
            pl.BlockSpec((SUBLANES, gw), lambda b, g, s: (hrow(b, g, s), COL_XS // gw + g)),
            pl.BlockSpec((rows, n), lambda b, g, s: (row(b, g, s), COL_B // n + g)),
            pl.BlockSpec((SUBLANES, n), lambda b, g, s: (hrow(b, g, s), COL_B // n + g)),
            pl.BlockSpec((rows, n), lambda b, g, s: (row(b, g, s), COL_C // n + g)),
            pl.BlockSpec((SUBLANES, n), lambda b, g, s: (hrow(b, g, s), COL_C // n + g)),
            pl.BlockSpec((None, SSM_CONV, cwid), lambda b, g, s: (g, 0, 0)),
            pl.BlockSpec((None, 1, cwid), lambda b, g, s: (g, 0, 0)),
            pl.BlockSpec((rows, gw), lambda b, g, s: (row(b, g, s), COL_Z // gw + g)),
            pl.BlockSpec((None, None, rows, hg), lambda b, g, s: (b, g, s, 0)),
            pl.BlockSpec((hg, rows), lambda b, g, s: (g, row(b, g, s))),
            pl.BlockSpec((None, 1, hg), lambda b, g, s: (g, 0, 0)),
            pl.BlockSpec((None, hg, 1), lambda b, g, s: (g, 0, 0)),
            pl.BlockSpec((None, 1, hg), lambda b, g, s: (g, 0, 0)),
            pl.BlockSpec((None, hg, 1), lambda b, g, s: (g, 0, 0)),
            pl.BlockSpec((1, gw), lambda b, g, s: (0, g)),
            pl.BlockSpec((1, gw), lambda b, g, s: (0, g)),
            pl.BlockSpec((SHADOW_TM, D_MODEL), lambda b, g, s: (tile_i(b, g, s), 0)),
            pl.BlockSpec((D_MODEL, SHADOW_TN), lambda b, g, s: (0, tile_j(b, g, s))),
        ],
        out_specs=[pl.BlockSpec((rows, gw), lambda b, g, s: (row(b, g, s), g)),
                   pl.BlockSpec((SHADOW_TM, SHADOW_TN),
                                lambda b, g, s: (tile_i(b, g, s), tile_j(b, g, s)))],
        out_shape=[jax.ShapeDtypeStruct((TOKENS, SSM_D_INNER), BF16),
                   jax.ShapeDtypeStruct((TOKENS, PROJ_B_COLS), F32)],
        scratch_shapes=[pltpu.VMEM((hg, n, SSM_HEAD_DIM), F32),
                        pltpu.VMEM((rows + SUBLANES, cwid), F32),
                        pltpu.VMEM((rows, cwid), F32),
                        pltpu.VMEM((rows, gw), F32)],
        compiler_params=_params("parallel", "parallel", "arbitrary"),
        name=name,
    )(proj, proj, proj, proj, proj, proj,
      per_group(conv_w, SSM_CONV), per_group(conv_b.reshape(1, SSM_CONV_DIM), 1),
      proj, dt_col, dt_row,
      dt_bias.reshape(ng, 1, hg), dt_bias.reshape(ng, hg, 1),
      a_log.reshape(ng, 1, hg), a_log.reshape(ng, hg, 1),
      jnp.repeat(d_skip, SSM_HEAD_DIM).reshape(1, SSM_D_INNER),
      norm_g.reshape(1, SSM_D_INNER), u, w_b)


KEY_CHUNK = 256
PREP_ROWS = KEY_CHUNK
PREP_QBLOCKS = PREP_ROWS // Q_BLOCK
N_KEY_CHUNKS = SEQ // KEY_CHUNK
HEAD_PAIRS = ATTN_HEADS // 2
IDX_PAIRS = IDX_HEADS // 2
Q_PRESCALE = ATTN_HEAD_DIM ** -0.5 * math.log2(math.e)
W_IDX_SCALE = IDX_HEADS ** -0.5 * IDX_HEAD_DIM ** -0.5
M_INIT = -1e30
PA_TILE = 512
PA_SLICE = 256


def _rope(x, c, sn, sp, half):
    return x * c + pltpu.roll(x, LANES - half, 1) * sn + pltpu.roll(x, half, 1) * sp


def _prep_kernel(q_ref, k_ref, v_ref, qi_ref, smk_ref, smw_ref, ca_ref, sna_ref, spa_ref,
                 ci_ref, sni_ref, spi_ref, gk_ref,
                 qt_ref, ko_ref, vt_ref, qit_ref, kio_ref, wt_ref):
    ca, sna, spa = ca_ref[...], sna_ref[...], spa_ref[...]
    ci, sni, spi = ci_ref[...], sni_ref[...], spi_ref[...]
    ha = ATTN_ROT_DIM // 2
    hi = IDX_ROT_DIM // 2
    hd = ATTN_HEAD_DIM
    zeros_half = jnp.zeros((LANES - IDX_HEAD_DIM, Q_BLOCK), F32)
    for qq in range(PREP_QBLOCKS):
        rows = slice(qq * Q_BLOCK, (qq + 1) * Q_BLOCK)
        for h in range(ATTN_HEADS):
            qr = _rope(q_ref[rows, h * hd:(h + 1) * hd], ca[rows], sna[rows], spa[rows], ha)
            qt_ref[qq, h // 2, :, (h % 2) * Q_BLOCK:(h % 2 + 1) * Q_BLOCK] = (
                (qr * Q_PRESCALE).T.astype(BF16))
        for j in range(IDX_PAIRS):
            xt = _rope(qi_ref[rows, j * LANES:(j + 1) * LANES],
                       ci[rows], sni[rows], spi[rows], hi).T
            even = jnp.concatenate([xt[:IDX_HEAD_DIM], zeros_half], axis=0)
            odd = jnp.concatenate([xt[IDX_HEAD_DIM:], zeros_half], axis=0)
            qit_ref[qq, j, :, 0:Q_BLOCK] = even.astype(BF16)
            qit_ref[qq, j, :, Q_BLOCK:] = odd.astype(BF16)
    for g in range(ATTN_KV_HEADS):
        cols = slice(g * hd, (g + 1) * hd)
        ko_ref[:, cols] = _rope(k_ref[:, cols], ca, sna, spa, ha).astype(BF16)
        vt_ref[g, 0] = v_ref[:, cols].T.astype(BF16)
    xk = smk_ref[...]
    ms = jnp.sum(xk * xk, axis=-1, keepdims=True) * (1.0 / IDX_HEAD_DIM)
    xk = (xk * lax.rsqrt(ms + EPS)) * gk_ref[...]
    kio_ref[...] = _rope(xk, ci, sni, spi, hi).astype(BF16)
    smt = smw_ref[...].T
    w0 = SMALL_W - SMALL_DT
    for qq in range(PREP_QBLOCKS):
        wt_ref[qq] = smt[w0:w0 + IDX_HEADS, qq * Q_BLOCK:(qq + 1) * Q_BLOCK] * W_IDX_SCALE


def dsa_prep(proj, small, tabs, idx_k_norm, name):
    r = PREP_ROWS
    nqb = PREP_QBLOCKS
    gk = jnp.concatenate([idx_k_norm, jnp.zeros((LANES - IDX_HEAD_DIM,), F32)]).reshape(1, LANES)
    tab_spec = pl.BlockSpec((r, LANES), lambda i: (i, 0))
    return pl.pallas_call(
        _prep_kernel,
        grid=(TOKENS // r,),
        in_specs=[
            pl.BlockSpec((r, ATTN_WIDTH), lambda i: (i, COL_Q // ATTN_WIDTH)),
            pl.BlockSpec((r, ATTN_KV_WIDTH), lambda i: (i, COL_K // ATTN_KV_WIDTH)),
            pl.BlockSpec((r, ATTN_KV_WIDTH), lambda i: (i, COL_V // ATTN_KV_WIDTH)),
            pl.BlockSpec((r, IDX_WIDTH), lambda i: (i, COL_QI // IDX_WIDTH)),
            pl.BlockSpec((r, LANES), lambda i: (i, 0)),
            pl.BlockSpec((r, LANES), lambda i: (i, SMALL_DT // LANES)),
            tab_spec, tab_spec, tab_spec, tab_spec, tab_spec, tab_spec,
            pl.BlockSpec((1, LANES), lambda i: (0, 0)),
        ],
        out_specs=[
            pl.BlockSpec((nqb, HEAD_PAIRS, ATTN_HEAD_DIM, 2 * Q_BLOCK), lambda i: (i, 0, 0, 0)),
            pl.BlockSpec((r, ATTN_KV_WIDTH), lambda i: (i, 0)),
            pl.BlockSpec((ATTN_KV_HEADS, 1, ATTN_HEAD_DIM, r), lambda i: (0, i, 0, 0)),
            pl.BlockSpec((nqb, IDX_PAIRS, LANES, 2 * Q_BLOCK), lambda i: (i, 0, 0, 0)),
            pl.BlockSpec((r, LANES), lambda i: (i, 0)),
            pl.BlockSpec((nqb, IDX_HEADS, Q_BLOCK), lambda i: (i, 0, 0)),
        ],
        out_shape=[
            jax.ShapeDtypeStruct((TOKENS // Q_BLOCK, HEAD_PAIRS, ATTN_HEAD_DIM, 2 * Q_BLOCK), BF16),
            jax.ShapeDtypeStruct((TOKENS, ATTN_KV_WIDTH), BF16),
            jax.ShapeDtypeStruct((ATTN_KV_HEADS, TOKENS // r, ATTN_HEAD_DIM, r), BF16),
            jax.ShapeDtypeStruct((TOKENS // Q_BLOCK, IDX_PAIRS, LANES, 2 * Q_BLOCK), BF16),
            jax.ShapeDtypeStruct((TOKENS, LANES), BF16),
            jax.ShapeDtypeStruct((TOKENS // Q_BLOCK, IDX_HEADS, Q_BLOCK), F32),
        ],
        compiler_params=_params("parallel"),
        name=name,
    )(proj, proj, proj, proj, small, small, *tabs, gk)


def _col_reduce8(x, op):
    rows, lanes = x.shape
    g = x.reshape(4, rows // (4 * SUBLANES), SUBLANES, lanes)
    p = op(g, axis=1)
    red = jnp.maximum if op is jnp.max else jnp.add
    return red(red(p[0], p[1]), red(p[2], p[3]))


def _dsa_kernel(qt_ref, k_ref, vt_ref, qit_ref, ki_ref, wt_ref, ya_ref, wpa_ref, o_ref, pa_ref,
                key_ref, madd_ref, acc_ref, m_ref, l_ref, alpha_ref, s_ref, pe_ref, thr_ref):
    tq = Q_BLOCK
    kc = KEY_CHUNK
    hd = ATTN_HEAD_DIM
    qb = pl.program_id(1)
    n_chunks = qb // (kc // tq) + 1
    neg_inf = -jnp.inf
    q_pos = qb * tq + lax.broadcasted_iota(jnp.int32, (kc, tq), 1)
    k_off = lax.broadcasted_iota(jnp.int32, (kc, tq), 0)

    def causal(c):
        return c * kc + k_off <= q_pos

    def key_rows(c):
        return pl.ds(pl.multiple_of(c * kc, kc), kc)

    wt = wt_ref[...]

    def score_chunk(c, carry):
        ki = ki_ref[key_rows(c), :]
        score = jnp.zeros((kc, tq), F32)
        for j in range(IDX_PAIRS):
            lt = _dot(ki, qit_ref[j])
            score = score + jnp.maximum(lt[:, :tq], 0.0) * wt[2 * j:2 * j + 1, :]
            score = score + jnp.maximum(lt[:, tq:], 0.0) * wt[2 * j + 1:2 * j + 2, :]
        score = jnp.where(score == 0.0, 0.0, score)
        bits = lax.bitcast_convert_type(jnp.where(causal(c), score, neg_inf), jnp.int32)
        key_ref[c] = jnp.where(bits >= 0, bits, bits ^ jnp.int32(0x7FFFFFFF))
        return carry

    lax.fori_loop(0, n_chunks, score_chunk, 0)

    pa_parts = PA_TILE // PA_SLICE

    def pa_slice(k):
        rows_k = slice((k % pa_parts) * PA_SLICE, (k % pa_parts + 1) * PA_SLICE)
        cols_k = slice((k // pa_parts) * PA_SLICE, (k // pa_parts + 1) * PA_SLICE)
        pa_ref[rows_k, cols_k] = _dot(ya_ref[rows_k, :], wpa_ref[:, cols_k])

    def radix_select(chunks_used):
        def pad_chunk(c, carry):
            key_ref[c] = jnp.full((kc, tq), INT_MIN, jnp.int32)
            return carry

        lax.fori_loop(n_chunks, chunks_used, pad_chunk, 0)
        t = jnp.zeros((1, tq), jnp.int32)
        for i in range(32):
            cand = t | lax.shift_left(jnp.int32(1), 31 - i)
            cand_key = cand ^ jnp.int32(INT_MIN)
            cnt8 = jnp.zeros((SUBLANES, tq), F32)
            for c in range(chunks_used):
                hit = jnp.where(key_ref[c] >= cand_key, 1.0, 0.0)
                cnt8 = cnt8 + _col_reduce8(hit, jnp.sum)
            cnt = jnp.sum(cnt8, axis=0, keepdims=True)
            t = jnp.where(cnt >= float(TOP_K), cand, t)
            if i % (32 // pa_parts ** 2) == 0:
                pa_slice(i // (32 // pa_parts ** 2))
        thr_ref[...] = t ^ jnp.int32(INT_MIN)

    half_chunks = N_KEY_CHUNKS // 2

    @pl.when(n_chunks <= half_chunks)
    def _():
        radix_select(half_chunks)

    @pl.when(n_chunks > half_chunks)
    def _():
        radix_select(N_KEY_CHUNKS)

    thr = thr_ref[...]

    def mask_chunk(c, cnt):
        ge = key_ref[c] >= thr
        ok = causal(c)
        madd_ref[c] = jnp.where(ok, jnp.where(ge, 0.0, neg_inf), neg_inf)
        return cnt + jnp.sum(jnp.where(ok, jnp.where(ge, 1.0, 0.0), 0.0), axis=0, keepdims=True)

    cnt_ge = lax.fori_loop(0, n_chunks, mask_chunk, jnp.zeros((1, tq), F32))

    @pl.when(jnp.max(cnt_ge) > float(TOP_K))
    def _():
        def count_gt(c, cnt):
            return cnt + jnp.sum(jnp.where(key_ref[c] > thr, 1.0, 0.0), axis=0, keepdims=True)

        need = float(TOP_K) - lax.fori_loop(0, n_chunks, count_gt, jnp.zeros((1, tq), F32))
        a_i = lax.broadcasted_iota(jnp.int32, (kc, kc), 0)
        a_j = lax.broadcasted_iota(jnp.int32, (kc, kc), 1)
        earlier = jnp.where(a_j < a_i, 1.0, 0.0).astype(BF16)

        def tie_chunk(c, run):
            key = key_ref[c]
            gt = key > thr
            eq = key == thr
            e = jnp.where(eq, 1.0, 0.0)
            rank = _dot(earlier, e.astype(BF16)) + run
            keep_eq = jnp.where(rank < need, 0.0, neg_inf)
            val = jnp.where(gt, 0.0, jnp.where(eq, keep_eq, neg_inf))
            madd_ref[c] = jnp.where(causal(c), val, neg_inf)
            return run + jnp.sum(e, axis=0, keepdims=True)

        lax.fori_loop(0, n_chunks, tie_chunk, jnp.zeros((1, tq), F32))

    m_ref[...] = jnp.full(m_ref.shape, M_INIT, F32)
    l_ref[...] = jnp.zeros(l_ref.shape, F32)
    acc_ref[...] = jnp.zeros(acc_ref.shape, F32)

    def attn_chunk(c, carry):
        half_rows = kc // 2
        for p in range(HEAD_PAIRS):
            g = p // 2
            top8 = None
            for rh in range(2):
                r0 = pl.multiple_of(c * kc + rh * half_rows, half_rows)
                sub = slice(rh * half_rows, (rh + 1) * half_rows)
                s = _dot(k_ref[pl.ds(r0, half_rows), g * hd:(g + 1) * hd], qt_ref[p])
                madd = madd_ref[c, sub, :]
                s = s + jnp.concatenate([madd, madd], axis=1)
                s_ref[p, sub, :] = s
                t8 = _col_reduce8(s, jnp.max)
                top8 = t8 if top8 is None else jnp.maximum(top8, t8)
            m_old = m_ref[p]
            m_new = jnp.maximum(m_old, jnp.max(top8, axis=0, keepdims=True))
            m_ref[p] = m_new
            alpha_ref[p] = jnp.exp2(m_old - m_new)
        for p in range(HEAD_PAIRS):
            sum8 = None
            for rh in range(2):
                sub = slice(rh * half_rows, (rh + 1) * half_rows)
                pe = jnp.exp2(s_ref[p, sub, :] - m_ref[p])
                pe_ref[p, sub, :] = pe.astype(BF16)
                s8 = _col_reduce8(pe, jnp.sum)
                sum8 = s8 if sum8 is None else sum8 + s8
            l_ref[p] = l_ref[p] * alpha_ref[p] + jnp.sum(sum8, axis=0, keepdims=True)
        for p in range(HEAD_PAIRS):
            acc_ref[p] = acc_ref[p] * alpha_ref[p] + _dot(vt_ref[p // 2, c], pe_ref[p])
        return carry

    lax.fori_loop(0, n_chunks, attn_chunk, 0)

    for p in range(HEAD_PAIRS):
        o = acc_ref[p] / l_ref[p]
        for half in range(2):
            h = 2 * p + half
            o_ref[:, h * hd:(h + 1) * hd] = o[:, half * tq:(half + 1) * tq].T.astype(o_ref.dtype)


def dsa_attention(q_t, k_r, v_t, qi_t, ki, w_t, y_a, w_pa, name):
    nq = SEQ // Q_BLOCK
    row = lambda b, i: b * nq + i
    pa_m = TOKENS // PA_TILE
    assert pa_m * (D_MODEL // PA_TILE) == BATCH * nq
    return pl.pallas_call(
        _dsa_kernel,
        grid=(BATCH, nq),
        in_specs=[
            pl.BlockSpec((None, HEAD_PAIRS, ATTN_HEAD_DIM, 2 * Q_BLOCK),
                         lambda b, i: (row(b, i), 0, 0, 0)),
            pl.BlockSpec((SEQ, ATTN_KV_WIDTH), lambda b, i: (b, 0)),
            pl.BlockSpec((ATTN_KV_HEADS, N_KEY_CHUNKS, ATTN_HEAD_DIM, KEY_CHUNK),
                         lambda b, i: (0, b, 0, 0)),
            pl.BlockSpec((None, IDX_PAIRS, LANES, 2 * Q_BLOCK), lambda b, i: (row(b, i), 0, 0, 0)),
            pl.BlockSpec((SEQ, LANES), lambda b, i: (b, 0)),
            pl.BlockSpec((None, IDX_HEADS, Q_BLOCK), lambda b, i: (row(b, i), 0, 0)),
            pl.BlockSpec((PA_TILE, SSM_D_INNER), lambda b, i: (row(b, i) % pa_m, 0)),
            pl.BlockSpec((SSM_D_INNER, PA_TILE), lambda b, i: (0, row(b, i) // pa_m)),
        ],
        out_specs=[pl.BlockSpec((Q_BLOCK, ATTN_WIDTH), lambda b, i: (row(b, i), 0)),
                   pl.BlockSpec((PA_TILE, PA_TILE),
                                lambda b, i: (row(b, i) % pa_m, row(b, i) // pa_m))],
        out_shape=[jax.ShapeDtypeStruct((TOKENS, ATTN_WIDTH), BF16),
                   jax.ShapeDtypeStruct((TOKENS, D_MODEL), F32)],
        scratch_shapes=[pltpu.VMEM((N_KEY_CHUNKS, KEY_CHUNK, Q_BLOCK), jnp.int32),
                        pltpu.VMEM((N_KEY_CHUNKS, KEY_CHUNK, Q_BLOCK), F32),
                        pltpu.VMEM((HEAD_PAIRS, ATTN_HEAD_DIM, 2 * Q_BLOCK), F32),
                        pltpu.VMEM((HEAD_PAIRS, 1, 2 * Q_BLOCK), F32),
                        pltpu.VMEM((HEAD_PAIRS, 1, 2 * Q_BLOCK), F32),
                        pltpu.VMEM((HEAD_PAIRS, 1, 2 * Q_BLOCK), F32),
                        pltpu.VMEM((HEAD_PAIRS, KEY_CHUNK, 2 * Q_BLOCK), F32),
                        pltpu.VMEM((HEAD_PAIRS, KEY_CHUNK, 2 * Q_BLOCK), BF16),
                        pltpu.VMEM((1, Q_BLOCK), jnp.int32)],
        compiler_params=_params("parallel", "arbitrary"),
        name=name,
    )(q_t, k_r, v_t, qi_t, ki, w_t, y_a, w_pa)


def _rope_lane_tables(positions):
    pos = positions.astype(F32).reshape(TOKENS, 1)

    def tables(rot_dim, width):
        inv_freq = ROPE_THETA ** (-(jnp.arange(0, rot_dim, 2, dtype=F32) / rot_dim))
        ang = pos * inv_freq
        cos, sin = jnp.cos(ang), jnp.sin(ang)
        half = rot_dim // 2
        ones = jnp.ones((TOKENS, width - rot_dim), F32)
        zeros = lambda n: jnp.zeros((TOKENS, n), F32)
        c = jnp.concatenate([cos, cos, ones], axis=1)
        sn = jnp.concatenate([-sin, zeros(width - half)], axis=1)
        sp = jnp.concatenate([zeros(half), sin, zeros(width - rot_dim)], axis=1)
        reps = LANES // width
        return tuple(jnp.tile(t, (1, reps)) for t in (c, sn, sp))

    return tables(ATTN_ROT_DIM, ATTN_HEAD_DIM) + tables(IDX_ROT_DIM, IDX_HEAD_DIM)


def _regroup_w_in(w):
    sizes = (SSM_HEADS, ATTN_WIDTH, ATTN_KV_WIDTH, ATTN_KV_WIDTH, IDX_WIDTH, IDX_HEAD_DIM,
             IDX_HEADS, D_MODEL, D_MODEL)
    parts = []
    off = PROJ_A_COLS
    for s in sizes:
        parts.append(w[:, off:off + s])
        off += s
    dt, q, k, v, qi, ki, wi, ga, gb = parts
    w_a = w[:, :PROJ_A_COLS].astype(BF16)
    w_b = jnp.concatenate([q, k, v, qi, ga, gb], axis=1).astype(BF16)
    zc = lambda n: jnp.zeros((D_MODEL, n), F32)
    small = jnp.concatenate(
        [ki, zc(LANES - IDX_HEAD_DIM), dt, wi, zc(SMALL_COLS - SMALL_W - IDX_HEADS)],
        axis=1).astype(BF16)
    return w_a, w_b, small


def kernel(x, positions, mix_norm, w_in, conv_w, conv_b, dt_bias, a_log, d_skip, ssm_norm,
           idx_k_norm, w_proj_a, w_proj_b, w_out, ffn_norm, w_ffn_gate, w_ffn_up, w_ffn_down,
           final_norm):
    tabs = _rope_lane_tables(positions)
    xf = x.reshape(TOKENS, D_MODEL)
    for i in range(DEPTH):
        w_a, w_b, w_small = _regroup_w_in(w_in[i])
        u = rmsnorm(xf, mix_norm[i], BF16, f"mix_norm_{i}")
        proj_a = matmul_bf16w(u, w_a, 1024, 1024, F32, f"in_proj_a_{i}")
        small = matmul_bf16w(u, w_small, 1024, SMALL_COLS, F32, f"in_proj_small_{i}")

        dt_raw = small[:, SMALL_DT:SMALL_W]
        dt_col = jnp.transpose(
            dt_raw.reshape(BATCH, SEQ, SSM_GROUPS, SSM_HEADS_PER_GROUP), (0, 2, 1, 3))
        y_a, proj_b = ssd_branch(proj_a, dt_col, dt_raw.T, conv_w[i], conv_b[i], dt_bias[i],
                                 a_log[i], d_skip[i], ssm_norm[i], u, w_b, f"ssd_{i}")

        q_t, k_r, v_t, qi_t, ki, w_t = dsa_prep(proj_b, small, tabs, idx_k_norm[i],
                                                f"dsa_prep_{i}")
        y_b, pa = dsa_attention(q_t, k_r, v_t, qi_t, ki, w_t, y_a, w_proj_a[i].astype(BF16),
                                f"dsa_{i}")

        merged = merge_branches(pa, y_b, w_proj_b, i, proj_b, 512, 1024, f"merge_{i}")
        xf = matmul_residual(merged, w_out, i, xf, 1024, 1024, f"out_proj_{i}")

        h = rmsnorm(xf, ffn_norm[i], BF16, f"ffn_norm_{i}")
        ff = ffn_up(h, w_ffn_gate, w_ffn_up, i, 1024, 512, f"ffn_up_{i}")
        xf = matmul_residual(ff, w_ffn_down, i, xf, 512, 512, f"ffn_down_{i}")
    out = rmsnorm(xf, final_norm, F32, "final_norm")
    return out.reshape(BATCH, SEQ, D_MODEL)
```

```python
import math

import jax
import jax.numpy as jnp
from jax import lax
from jax.experimental import pallas as pl
from jax.experimental.pallas import tpu as pltpu

F32 = jnp.float32
BF16 = jnp.bfloat16

D_MODEL = 2048
BATCH = 4
SEQ = 2048
DEPTH = 4
TOKENS = BATCH * SEQ
EPS = 1e-6

SSM_D_INNER = 4096
SSM_HEAD_DIM = 64
SSM_HEADS = 64
SSM_GROUPS = 8
SSM_HEADS_PER_GROUP = SSM_HEADS // SSM_GROUPS
SSM_GROUP_WIDTH = SSM_D_INNER // SSM_GROUPS
SSM_STATE = 128
SSM_CONV = 4
SSM_CHUNK = 128
SSM_CONV_DIM = SSM_D_INNER + 2 * SSM_GROUPS * SSM_STATE

ATTN_HEADS = 16
ATTN_HEAD_DIM = 128
ATTN_KV_HEADS = 4
ATTN_WIDTH = ATTN_HEADS * ATTN_HEAD_DIM
ATTN_KV_WIDTH = ATTN_KV_HEADS * ATTN_HEAD_DIM
IDX_HEADS = 16
IDX_HEAD_DIM = 64
IDX_WIDTH = IDX_HEADS * IDX_HEAD_DIM
TOP_K = min(256, SEQ // 4)
Q_BLOCK = 128
ROPE_THETA = 500000.0
ATTN_ROT_DIM = ATTN_HEAD_DIM // 4
IDX_ROT_DIM = IDX_HEAD_DIM // 4
FFN_HIDDEN = 5632

COL_Z = 0
COL_XS = COL_Z + SSM_D_INNER
COL_B = COL_XS + SSM_D_INNER
COL_C = COL_B + SSM_GROUPS * SSM_STATE
PROJ_A_COLS = COL_C + SSM_GROUPS * SSM_STATE
COL_Q = 0
COL_K = COL_Q + ATTN_WIDTH
COL_V = COL_K + ATTN_KV_WIDTH
COL_QI = COL_V + ATTN_KV_WIDTH
COL_GA = COL_QI + IDX_WIDTH
COL_GB = COL_GA + D_MODEL
PROJ_B_COLS = COL_GB + D_MODEL
SMALL_COLS = 256
SMALL_DT = 128
SMALL_W = SMALL_DT + SSM_HEADS

LANES = 128
SUBLANES = 8
VMEM_LIMIT_BYTES = 56 * 1024 * 1024

INT_MIN = -(2 ** 31)


def _params(*sem):
    return pltpu.CompilerParams(dimension_semantics=sem, vmem_limit_bytes=VMEM_LIMIT_BYTES)


def _dot(a, b):
    return jnp.dot(a, b, preferred_element_type=F32)


def _sigmoid(x):
    return 1.0 / (1.0 + jnp.exp2(x * (-math.log2(math.e))))


def _rmsnorm_kernel(x_ref, g_ref, o_ref):
    x = x_ref[...]
    ms = jnp.mean(x * x, axis=-1, keepdims=True)
    o_ref[...] = ((x * lax.rsqrt(ms + EPS)) * g_ref[...]).astype(o_ref.dtype)


def rmsnorm(x, g, out_dtype, name, tm=512):
    m, d = x.shape
    return pl.pallas_call(
        _rmsnorm_kernel,
        grid=(m // tm,),
        in_specs=[pl.BlockSpec((tm, d), lambda i: (i, 0)),
                  pl.BlockSpec((1, d), lambda i: (0, 0))],
        out_specs=pl.BlockSpec((tm, d), lambda i: (i, 0)),
        out_shape=jax.ShapeDtypeStruct((m, d), out_dtype),
        compiler_params=_params("parallel"),
        name=name,
    )(x, g.reshape(1, d))


def _cast_on_first_m_step(pairs):
    @pl.when(pl.program_id(1) == 0)
    def _():
        for w_ref, wb_ref in pairs:
            wb_ref[...] = w_ref[...].astype(BF16)


def _weight_spec(k, tn, layer):
    return pl.BlockSpec((None, k, tn), lambda j, i: (layer, 0, j))


def _mm_kernel(a_ref, w_ref, o_ref):
    o_ref[...] = _dot(a_ref[...], w_ref[...]).astype(o_ref.dtype)


def matmul_bf16w(a, w, tm, tn, out_dtype, name):
    m, k = a.shape
    n = w.shape[1]
    return pl.pallas_call(
        _mm_kernel,
        grid=(n // tn, m // tm),
        in_specs=[pl.BlockSpec((tm, k), lambda j, i: (i, 0)),
                  pl.BlockSpec((k, tn), lambda j, i: (0, j))],
        out_specs=pl.BlockSpec((tm, tn), lambda j, i: (i, j)),
        out_shape=jax.ShapeDtypeStruct((m, n), out_dtype),
        compiler_params=_params("parallel", "parallel"),
        name=name,
    )(a, w)


def _mm_residual_kernel(a_ref, w_ref, r_ref, o_ref, wb_ref):
    _cast_on_first_m_step([(w_ref, wb_ref)])
    o_ref[...] = r_ref[...] + _dot(a_ref[...], wb_ref[...])


def matmul_residual(a, w_stack, layer, r, tm, tn, name):
    m, k = a.shape
    n = w_stack.shape[2]
    return pl.pallas_call(
        _mm_residual_kernel,
        grid=(n // tn, m // tm),
        in_specs=[pl.BlockSpec((tm, k), lambda j, i: (i, 0)),
                  _weight_spec(k, tn, layer),
                  pl.BlockSpec((tm, tn), lambda j, i: (i, j))],
        out_specs=pl.BlockSpec((tm, tn), lambda j, i: (i, j)),
        out_shape=jax.ShapeDtypeStruct((m, n), F32),
        scratch_shapes=[pltpu.VMEM((k, tn), BF16)],
        compiler_params=_params("parallel", "arbitrary"),
        name=name,
    )(a, w_stack, r)


def _ffn_up_kernel(a_ref, wg_ref, wu_ref, o_ref, wgb_ref, wub_ref):
    _cast_on_first_m_step([(wg_ref, wgb_ref), (wu_ref, wub_ref)])
    a = a_ref[...]
    g = _dot(a, wgb_ref[...])
    u = _dot(a, wub_ref[...])
    o_ref[...] = ((g * _sigmoid(g)) * u).astype(o_ref.dtype)


def ffn_up(a, wg_stack, wu_stack, layer, tm, tn, name):
    m, k = a.shape
    n = wg_stack.shape[2]
    return pl.pallas_call(
        _ffn_up_kernel,
        grid=(n // tn, m // tm),
        in_specs=[pl.BlockSpec((tm, k), lambda j, i: (i, 0)),
                  _weight_spec(k, tn, layer),
                  _weight_spec(k, tn, layer)],
        out_specs=pl.BlockSpec((tm, tn), lambda j, i: (i, j)),
        out_shape=jax.ShapeDtypeStruct((m, n), BF16),
        scratch_shapes=[pltpu.VMEM((k, tn), BF16), pltpu.VMEM((k, tn), BF16)],
        compiler_params=_params("parallel", "arbitrary"),
        name=name,
    )(a, wg_stack, wu_stack)


def _merge_kernel(pa_ref, yb_ref, wb_ref, ga_ref, gb_ref, o_ref, wbb_ref):
    _cast_on_first_m_step([(wb_ref, wbb_ref)])
    pb = _dot(yb_ref[...], wbb_ref[...])
    o_ref[...] = (_sigmoid(ga_ref[...]) * pa_ref[...]
                  + _sigmoid(gb_ref[...]) * pb).astype(o_ref.dtype)


def merge_branches(pa, yb, wb_stack, layer, proj, tm, tn, name):
    m = yb.shape[0]
    kb = wb_stack.shape[1]
    n = wb_stack.shape[2]
    ga0 = COL_GA // tn
    gb0 = COL_GB // tn
    return pl.pallas_call(
        _merge_kernel,
        grid=(n // tn, m // tm),
        in_specs=[pl.BlockSpec((tm, tn), lambda j, i: (i, j)),
                  pl.BlockSpec((tm, kb), lambda j, i: (i, 0)),
                  _weight_spec(kb, tn, layer),
                  pl.BlockSpec((tm, tn), lambda j, i: (i, ga0 + j)),
                  pl.BlockSpec((tm, tn), lambda j, i: (i, gb0 + j))],
        out_specs=pl.BlockSpec((tm, tn), lambda j, i: (i, j)),
        out_shape=jax.ShapeDtypeStruct((m, n), BF16),
        scratch_shapes=[pltpu.VMEM((kb, tn), BF16)],
        compiler_params=_params("parallel", "arbitrary"),
        name=name,
    )(pa, yb, wb_stack, proj, proj)


SSD_STEP_CHUNKS = 4
SSD_STEP_ROWS = SSD_STEP_CHUNKS * SSM_CHUNK
SSD_CONV_WIDTH = SSM_GROUP_WIDTH + 2 * SSM_STATE


def _softplus(x):
    return jnp.maximum(x, 0.0) + jnp.log1p(jnp.exp(-jnp.abs(x)))


def _split3(x):
    hi = x.astype(BF16)
    r1 = x - hi.astype(F32)
    mid = r1.astype(BF16)
    lo = (r1 - mid.astype(F32)).astype(BF16)
    return hi, mid, lo


def _ssd_kernel(x_ref, xh_ref, b_ref, bh_ref, c_ref, ch_ref, cw_ref, cbias_ref, z_ref,
                dtr_ref, biasc_ref, alogc_ref, dsk_ref, ng_ref,
                u_ref, wb_ref, o_ref, pb_ref, st_ref, pad_ref, xc_ref, y_ref):
    def shadow_slice(k):
        rows_k = slice(k * SHADOW_SLICE_ROWS, (k + 1) * SHADOW_SLICE_ROWS)
        pb_ref[rows_k, :] = _dot(u_ref[rows_k, :], wb_ref[...])

    step = pl.program_id(2)

    @pl.when(step == 0)
    def _():
        st_ref[...] = jnp.zeros_like(st_ref)

    gw = SSM_GROUP_WIDTH
    n = SSM_STATE
    lc = SSM_CHUNK
    hp = SSM_HEAD_DIM
    halo = SUBLANES

    def halo_rows(h_ref):
        first = jnp.full(h_ref.shape, step, jnp.int32) == 0
        return jnp.where(first, 0.0, h_ref[...])

    pad_ref[0:halo, 0:gw] = halo_rows(xh_ref)
    pad_ref[0:halo, gw:gw + n] = halo_rows(bh_ref)
    pad_ref[0:halo, gw + n:] = halo_rows(ch_ref)
    pad_ref[halo:, 0:gw] = x_ref[...]
    pad_ref[halo:, gw:gw + n] = b_ref[...]
    pad_ref[halo:, gw + n:] = c_ref[...]
    cw = cw_ref[...]
    cbias = cbias_ref[...]
    for r0 in range(0, SSD_STEP_ROWS, lc):
        acc = cbias + cw[SSM_CONV - 1:SSM_CONV, :] * pad_ref[halo + r0:halo + r0 + lc, :]
        for back in range(1, SSM_CONV):
            lo = halo + r0 - back
            acc = acc + cw[SSM_CONV - 1 - back:SSM_CONV - back, :] * pad_ref[lo:lo + lc, :]
        xc_ref[r0:r0 + lc, :] = acc * _sigmoid(acc)

    log2e = math.log2(math.e)
    dtr_all = _softplus(dtr_ref[...] + biasc_ref[...])
    adt_r_all = dtr_all * (-log2e * jnp.exp(alogc_ref[...]))

    ii = lax.broadcasted_iota(jnp.int32, (lc, lc), 0)
    jj = lax.broadcasted_iota(jnp.int32, (lc, lc), 1)
    tri_t = jnp.where(ii <= jj, 1.0, 0.0).astype(BF16)
    pad_heads = jnp.zeros((lc - SSM_HEADS_PER_GROUP, lc), F32)

    for cc in range(SSD_STEP_CHUNKS):
        r0 = cc * lc
        cs_r = sum(_dot(p, tri_t) for p in _split3(adt_r_all[:, r0:r0 + lc]))
        cs_c = jnp.concatenate([cs_r, pad_heads], axis=0).T
        ecs_r = jnp.exp2(cs_r)
        ecl = ecs_r[:, lc - 1:lc]
        dtr = dtr_all[:, r0:r0 + lc]
        ddr = dtr * jnp.exp2(cs_r[:, lc - 1:lc] - cs_r)

        x_t = xc_ref[r0:r0 + lc, 0:gw].T
        b16 = xc_ref[r0:r0 + lc, gw:gw + n].astype(BF16)
        cc32 = xc_ref[r0:r0 + lc, gw + n:]
        c_t = cc32.T
        cb_t = lax.dot_general(b16, cc32.astype(BF16), (((1,), (1,)), ((), ())),
                               preferred_element_type=F32)
        for h in range(SSM_HEADS_PER_GROUP):
            if h % (SSM_HEADS_PER_GROUP // SHADOW_SLICES_PER_CHUNK) == 0:
                shadow_slice(cc * SHADOW_SLICES_PER_CHUNK
                             + h // (SSM_HEADS_PER_GROUP // SHADOW_SLICES_PER_CHUNK))
            lo = h * hp
            cs_s = jnp.broadcast_to(cs_c[:, h:h + 1], (lc, lc))
            decay_t = jnp.exp2(jnp.where(ii <= jj, cs_r[h:h + 1, :] - cs_s, -jnp.inf))
            intra_t = cb_t * decay_t
            inter_t = c_t * ecs_r[h:h + 1, :]
            prev = st_ref[h]
            x_h = x_t[lo:lo + hp, :]
            y_t = _dot(
                jnp.concatenate([x_h * dtr[h:h + 1, :], prev], axis=1).astype(BF16),
                jnp.concatenate([intra_t, inter_t], axis=0).astype(BF16))
            if h % 2 == 0:
                y_even = y_t
            else:
                y_ref[r0:r0 + lc, lo - hp:lo + hp] = jnp.concatenate([y_even, y_t], axis=0).T
            st_new = _dot((x_h * ddr[h:h + 1, :]).astype(BF16), b16)
            st_ref[h] = prev * ecl[h:h + 1, :] + st_new

    y = y_ref[...] + dsk_ref[...] * xc_ref[:, 0:gw]
    z = z_ref[...]
    y = y * (z * _sigmoid(z))
    ms = jnp.mean(y * y, axis=-1, keepdims=True)
    o_ref[...] = ((y * lax.rsqrt(ms + EPS)) * ng_ref[...]).astype(o_ref.dtype)


SHADOW_TM = 1024
SHADOW_SLICES_PER_CHUNK = 1
SHADOW_SLICE_ROWS = SHADOW_TM // (SSD_STEP_CHUNKS * SHADOW_SLICES_PER_CHUNK)
SHADOW_TN = PROJ_B_COLS * TOKENS // SHADOW_TM // (BATCH * SSM_GROUPS * (SEQ // SSD_STEP_ROWS))


def ssd_branch(proj, dt_row, conv_w, conv_b, dt_bias, a_log, d_skip, norm_g, u, w_b, name):
    rows = SSD_STEP_ROWS
    m_tiles = TOKENS // SHADOW_TM
    tile_i = lambda b, g, s: ((b * SSM_GROUPS + g) * (SEQ // rows) + s) % m_tiles
    tile_j = lambda b, g, s: ((b * SSM_GROUPS + g) * (SEQ // rows) + s) // m_tiles
    steps = SEQ // rows
    gw = SSM_GROUP_WIDTH
    n = SSM_STATE
    hg = SSM_HEADS_PER_GROUP
    ng = SSM_GROUPS
    cwid = SSD_CONV_WIDTH

    def per_group(p, lead):
        xs = p[:, :SSM_D_INNER].reshape(lead, ng, gw)
        bs = p[:, SSM_D_INNER:SSM_D_INNER + ng * n].reshape(lead, ng, n)
        cs = p[:, SSM_D_INNER + ng * n:].reshape(lead, ng, n)
        return jnp.transpose(jnp.concatenate([xs, bs, cs], axis=2), (1, 0, 2))

    row = lambda b, g, s: b * steps + s
    hrow = lambda b, g, s: jnp.maximum(row(b, g, s) * (rows // SUBLANES) - 1, 0)
    return pl.pallas_call(
        _ssd_kernel,
        grid=(BATCH, ng, steps),
        in_specs=[
            pl.BlockSpec((rows, gw), lambda b, g, s: (row(b, g, s), COL_XS // gw + g)),
            pl.BlockSpec((SUBLANES, gw), lambda b, g, s: (hrow(b, g, s), COL_XS // gw + g)),
            pl.BlockSpec((rows, n), lambda b, g, s: (row(b, g, s), COL_B // n + g)),
            pl.BlockSpec((SUBLANES, n), lambda b, g, s: (hrow(b, g, s), COL_B // n + g)),
            pl.BlockSpec((rows, n), lambda b, g, s: (row(b, g, s), COL_C // n + g)),
            pl.BlockSpec((SUBLANES, n), lambda b, g, s: (hrow(b, g, s), COL_C // n + g)),
            pl.BlockSpec((None, SSM_CONV, cwid), lambda b, g, s: (g, 0, 0)),
            pl.BlockSpec((None, 1, cwid), lambda b, g, s: (g, 0, 0)),
            pl.BlockSpec((rows, gw), lambda b, g, s: (row(b, g, s), COL_Z // gw + g)),
            pl.BlockSpec((hg, rows), lambda b, g, s: (g, row(b, g, s))),
            pl.BlockSpec((None, hg, 1), lambda b, g, s: (g, 0, 0)),
            pl.BlockSpec((None, hg, 1), lambda b, g, s: (g, 0, 0)),
            pl.BlockSpec((1, gw), lambda b, g, s: (0, g)),
            pl.BlockSpec((1, gw), lambda b, g, s: (0, g)),
            pl.BlockSpec((SHADOW_TM, D_MODEL), lambda b, g, s: (tile_i(b, g, s), 0)),
            pl.BlockSpec((D_MODEL, SHADOW_TN), lambda b, g, s: (0, tile_j(b, g, s))),
        ],
        out_specs=[pl.BlockSpec((rows, gw), lambda b, g, s: (row(b, g, s), g)),
                   pl.BlockSpec((SHADOW_TM, SHADOW_TN),
                                lambda b, g, s: (tile_i(b, g, s), tile_j(b, g, s)))],
        out_shape=[jax.ShapeDtypeStruct((TOKENS, SSM_D_INNER), BF16),
                   jax.ShapeDtypeStruct((TOKENS, PROJ_B_COLS), F32)],
        scratch_shapes=[pltpu.VMEM((hg, SSM_HEAD_DIM, n), F32),
                        pltpu.VMEM((rows + SUBLANES, cwid), F32),
                        pltpu.VMEM((rows, cwid), F32),
                        pltpu.VMEM((rows, gw), F32)],
        compiler_params=_params("parallel", "parallel", "arbitrary"),
        name=name,
    )(proj, proj, proj, proj, proj, proj,
      per_group(conv_w, SSM_CONV), per_group(conv_b.reshape(1, SSM_CONV_DIM), 1),
      proj, dt_row, dt_bias.reshape(ng, hg, 1), a_log.reshape(ng, hg, 1),
      jnp.repeat(d_skip, SSM_HEAD_DIM).reshape(1, SSM_D_INNER),
      norm_g.reshape(1, SSM_D_INNER), u, w_b)


KEY_CHUNK = 256
PREP_ROWS = KEY_CHUNK
PREP_QBLOCKS = PREP_ROWS // Q_BLOCK
N_KEY_CHUNKS = SEQ // KEY_CHUNK
HEAD_PAIRS = ATTN_HEADS // 2
IDX_PAIRS = IDX_HEADS // 2
Q_PRESCALE = ATTN_HEAD_DIM ** -0.5 * math.log2(math.e)
W_IDX_SCALE = IDX_HEADS ** -0.5 * IDX_HEAD_DIM ** -0.5
M_INIT = -1e30
PA_TILE = 512
PA_SLICE = 256


def _rope(x, c, sn, sp, half):
    return x * c + pltpu.roll(x, LANES - half, 1) * sn + pltpu.roll(x, half, 1) * sp


def _prep_kernel(q_ref, k_ref, v_ref, qi_ref, smk_ref, smw_ref, ca_ref, sna_ref, spa_ref,
                 ci_ref, sni_ref, spi_ref, gk_ref,
                 qt_ref, ko_ref, vt_ref, qit_ref, kio_ref, wt_ref):
    ca, sna, spa = ca_ref[...], sna_ref[...], spa_ref[...]
    ci, sni, spi = ci_ref[...], sni_ref[...], spi_ref[...]
    ha = ATTN_ROT_DIM // 2
    hi = IDX_ROT_DIM // 2
    hd = ATTN_HEAD_DIM
    zeros_half = jnp.zeros((LANES - IDX_HEAD_DIM, Q_BLOCK), F32)
    for qq in range(PREP_QBLOCKS):
        rows = slice(qq * Q_BLOCK, (qq + 1) * Q_BLOCK)
        for h in range(ATTN_HEADS):
            qr = _rope(q_ref[rows, h * hd:(h + 1) * hd], ca[rows], sna[rows], spa[rows], ha)
            qt_ref[qq, h // 2, :, (h % 2) * Q_BLOCK:(h % 2 + 1) * Q_BLOCK] = (
                (qr * Q_PRESCALE).T.astype(BF16))
        for j in range(IDX_PAIRS):
            xt = _rope(qi_ref[rows, j * LANES:(j + 1) * LANES],
                       ci[rows], sni[rows], spi[rows], hi).T
            even = jnp.concatenate([xt[:IDX_HEAD_DIM], zeros_half], axis=0)
            odd = jnp.concatenate([xt[IDX_HEAD_DIM:], zeros_half], axis=0)
            qit_ref[qq, j, :, 0:Q_BLOCK] = even.astype(BF16)
            qit_ref[qq, j, :, Q_BLOCK:] = odd.astype(BF16)
    for g in range(ATTN_KV_HEADS):
        cols = slice(g * hd, (g + 1) * hd)
        ko_ref[:, cols] = _rope(k_ref[:, cols], ca, sna, spa, ha).astype(BF16)
        vt_ref[g, 0] = v_ref[:, cols].T.astype(BF16)
    xk = smk_ref[...]
    ms = jnp.sum(xk * xk, axis=-1, keepdims=True) * (1.0 / IDX_HEAD_DIM)
    xk = (xk * lax.rsqrt(ms + EPS)) * gk_ref[...]
    kio_ref[...] = _rope(xk, ci, sni, spi, hi).astype(BF16)
    smt = smw_ref[...].T
    w0 = SMALL_W - SMALL_DT
    for qq in range(PREP_QBLOCKS):
        wt_ref[qq] = smt[w0:w0 + IDX_HEADS, qq * Q_BLOCK:(qq + 1) * Q_BLOCK] * W_IDX_SCALE


def dsa_prep(proj, small, tabs, idx_k_norm, name):
    r = PREP_ROWS
    nqb = PREP_QBLOCKS
    gk = jnp.concatenate([idx_k_norm, jnp.zeros((LANES - IDX_HEAD_DIM,), F32)]).reshape(1, LANES)
    tab_spec = pl.BlockSpec((r, LANES), lambda i: (i, 0))
    return pl.pallas_call(
        _prep_kernel,
        grid=(TOKENS // r,),
        in_specs=[
            pl.BlockSpec((r, ATTN_WIDTH), lambda i: (i, COL_Q // ATTN_WIDTH)),
            pl.BlockSpec((r, ATTN_KV_WIDTH), lambda i: (i, COL_K // ATTN_KV_WIDTH)),
            pl.BlockSpec((r, ATTN_KV_WIDTH), lambda i: (i, COL_V // ATTN_KV_WIDTH)),
            pl.BlockSpec((r, IDX_WIDTH), lambda i: (i, COL_QI // IDX_WIDTH)),
            pl.BlockSpec((r, LANES), lambda i: (i, 0)),
            pl.BlockSpec((r, LANES), lambda i: (i, SMALL_DT // LANES)),
            tab_spec, tab_spec, tab_spec, tab_spec, tab_spec, tab_spec,
            pl.BlockSpec((1, LANES), lambda i: (0, 0)),
        ],
        out_specs=[
            pl.BlockSpec((nqb, HEAD_PAIRS, ATTN_HEAD_DIM, 2 * Q_BLOCK), lambda i: (i, 0, 0, 0)),
            pl.BlockSpec((r, ATTN_KV_WIDTH), lambda i: (i, 0)),
            pl.BlockSpec((ATTN_KV_HEADS, 1, ATTN_HEAD_DIM, r), lambda i: (0, i, 0, 0)),
            pl.BlockSpec((nqb, IDX_PAIRS, LANES, 2 * Q_BLOCK), lambda i: (i, 0, 0, 0)),
            pl.BlockSpec((r, LANES), lambda i: (i, 0)),
            pl.BlockSpec((nqb, IDX_HEADS, Q_BLOCK), lambda i: (i, 0, 0)),
        ],
        out_shape=[
            jax.ShapeDtypeStruct((TOKENS // Q_BLOCK, HEAD_PAIRS, ATTN_HEAD_DIM, 2 * Q_BLOCK), BF16),
            jax.ShapeDtypeStruct((TOKENS, ATTN_KV_WIDTH), BF16),
            jax.ShapeDtypeStruct((ATTN_KV_HEADS, TOKENS // r, ATTN_HEAD_DIM, r), BF16),
            jax.ShapeDtypeStruct((TOKENS // Q_BLOCK, IDX_PAIRS, LANES, 2 * Q_BLOCK), BF16),
            jax.ShapeDtypeStruct((TOKENS, LANES), BF16),
            jax.ShapeDtypeStruct((TOKENS // Q_BLOCK, IDX_HEADS, Q_BLOCK), F32),
        ],
        compiler_params=_params("parallel"),
        name=name,
    )(proj, proj, proj, proj, small, small, *tabs, gk)


def _col_reduce8(x, op):
    rows, lanes = x.shape
    g = x.reshape(4, rows // (4 * SUBLANES), SUBLANES, lanes)
    p = op(g, axis=1)
    red = jnp.maximum if op is jnp.max else jnp.add
    return red(red(p[0], p[1]), red(p[2], p[3]))


def _dsa_kernel(qt_ref, k_ref, vt_ref, qit_ref, ki_ref, wt_ref, ya_ref, wpa_ref, o_ref, pa_ref,
                key_ref, madd_ref, acc_ref, m_ref, l_ref, alpha_ref, s_ref, pe_ref, thr_ref):
    tq = Q_BLOCK
    kc = KEY_CHUNK
    hd = ATTN_HEAD_DIM
    qb = pl.program_id(1)
    n_chunks = qb // (kc // tq) + 1
    neg_inf = -jnp.inf
    q_pos = qb * tq + lax.broadcasted_iota(jnp.int32, (kc, tq), 1)
    k_off = lax.broadcasted_iota(jnp.int32, (kc, tq), 0)

    def causal(c):
        return c * kc + k_off <= q_pos

    def key_rows(c):
        return pl.ds(pl.multiple_of(c * kc, kc), kc)

    wt = wt_ref[...]

    def score_chunk(c, carry):
        ki = ki_ref[key_rows(c), :]
        score = jnp.zeros((kc, tq), F32)
        for j in range(IDX_PAIRS):
            lt = _dot(ki, qit_ref[j])
            score = score + jnp.maximum(lt[:, :tq], 0.0) * wt[2 * j:2 * j + 1, :]
            score = score + jnp.maximum(lt[:, tq:], 0.0) * wt[2 * j + 1:2 * j + 2, :]
        score = jnp.where(score == 0.0, 0.0, score)
        bits = lax.bitcast_convert_type(jnp.where(causal(c), score, neg_inf), jnp.int32)
        key_ref[c] = jnp.where(bits >= 0, bits, bits ^ jnp.int32(0x7FFFFFFF))
        return carry

    lax.fori_loop(0, n_chunks, score_chunk, 0)

    pa_parts = PA_TILE // PA_SLICE

    def pa_slice(k):
        rows_k = slice((k % pa_parts) * PA_SLICE, (k % pa_parts + 1) * PA_SLICE)
        cols_k = slice((k // pa_parts) * PA_SLICE, (k // pa_parts + 1) * PA_SLICE)
        pa_ref[rows_k, cols_k] = _dot(ya_ref[rows_k, :], wpa_ref[:, cols_k])

    def radix_select(chunks_used):
        def pad_chunk(c, carry):
            key_ref[c] = jnp.full((kc, tq), INT_MIN, jnp.int32)
            return carry

        lax.fori_loop(n_chunks, chunks_used, pad_chunk, 0)
        t = jnp.zeros((1, tq), jnp.int32)
        for i in range(32):
            cand = t | lax.shift_left(jnp.int32(1), 31 - i)
            cand_key = cand ^ jnp.int32(INT_MIN)
            cnt8 = jnp.zeros((SUBLANES, tq), F32)
            for c in range(chunks_used):
                hit = jnp.where(key_ref[c] >= cand_key, 1.0, 0.0)
                cnt8 = cnt8 + _col_reduce8(hit, jnp.sum)
            cnt = jnp.sum(cnt8, axis=0, keepdims=True)
            t = jnp.where(cnt >= float(TOP_K), cand, t)
            if i % (32 // pa_parts ** 2) == 0:
                pa_slice(i // (32 // pa_parts ** 2))
        thr_ref[...] = t ^ jnp.int32(INT_MIN)

    half_chunks = N_KEY_CHUNKS // 2

    @pl.when(n_chunks <= half_chunks)
    def _():
        radix_select(half_chunks)

    @pl.when(n_chunks > half_chunks)
    def _():
        radix_select(N_KEY_CHUNKS)

    thr = thr_ref[...]

    def mask_chunk(c, cnt):
        ge = key_ref[c] >= thr
        ok = causal(c)
        madd_ref[c] = jnp.where(ok, jnp.where(ge, 0.0, neg_inf), neg_inf)
        return cnt + jnp.sum(jnp.where(ok, jnp.where(ge, 1.0, 0.0), 0.0), axis=0, keepdims=True)

    cnt_ge = lax.fori_loop(0, n_chunks, mask_chunk, jnp.zeros((1, tq), F32))

    @pl.when(jnp.max(cnt_ge) > float(TOP_K))
    def _():
        def count_gt(c, cnt):
            return cnt + jnp.sum(jnp.where(key_ref[c] > thr, 1.0, 0.0), axis=0, keepdims=True)

        need = float(TOP_K) - lax.fori_loop(0, n_chunks, count_gt, jnp.zeros((1, tq), F32))
        a_i = lax.broadcasted_iota(jnp.int32, (kc, kc), 0)
        a_j = lax.broadcasted_iota(jnp.int32, (kc, kc), 1)
        earlier = jnp.where(a_j < a_i, 1.0, 0.0).astype(BF16)

        def tie_chunk(c, run):
            key = key_ref[c]
            gt = key > thr
            eq = key == thr
            e = jnp.where(eq, 1.0, 0.0)
            rank = _dot(earlier, e.astype(BF16)) + run
            keep_eq = jnp.where(rank < need, 0.0, neg_inf)
            val = jnp.where(gt, 0.0, jnp.where(eq, keep_eq, neg_inf))
            madd_ref[c] = jnp.where(causal(c), val, neg_inf)
            return run + jnp.sum(e, axis=0, keepdims=True)

        lax.fori_loop(0, n_chunks, tie_chunk, jnp.zeros((1, tq), F32))

    m_ref[...] = jnp.full(m_ref.shape, M_INIT, F32)
    l_ref[...] = jnp.zeros(l_ref.shape, F32)
    acc_ref[...] = jnp.zeros(acc_ref.shape, F32)

    def attn_chunk(c, carry):
        half_rows = kc // 2
        for p in range(HEAD_PAIRS):
            g = p // 2
            top8 = None
            for rh in range(2):
                r0 = pl.multiple_of(c * kc + rh * half_rows, half_rows)
                sub = slice(rh * half_rows, (rh + 1) * half_rows)
                s = _dot(k_ref[pl.ds(r0, half_rows), g * hd:(g + 1) * hd], qt_ref[p])
                madd = madd_ref[c, sub, :]
                s = s + jnp.concatenate([madd, madd], axis=1)
                s_ref[p, sub, :] = s
                t8 = _col_reduce8(s, jnp.max)
                top8 = t8 if top8 is None else jnp.maximum(top8, t8)
            m_old = m_ref[p]
            m_new = jnp.maximum(m_old, jnp.max(top8, axis=0, keepdims=True))
            m_ref[p] = m_new
            alpha_ref[p] = jnp.exp2(m_old - m_new)
        for p in range(HEAD_PAIRS):
            sum8 = None
            for rh in range(2):
                sub = slice(rh * half_rows, (rh + 1) * half_rows)
                pe = jnp.exp2(s_ref[p, sub, :] - m_ref[p])
                pe_ref[p, sub, :] = pe.astype(BF16)
                s8 = _col_reduce8(pe, jnp.sum)
                sum8 = s8 if sum8 is None else sum8 + s8
            l_ref[p] = l_ref[p] * alpha_ref[p] + jnp.sum(sum8, axis=0, keepdims=True)
        for p in range(HEAD_PAIRS):
            acc_ref[p] = acc_ref[p] * alpha_ref[p] + _dot(vt_ref[p // 2, c], pe_ref[p])
        return carry

    lax.fori_loop(0, n_chunks, attn_chunk, 0)

    for p in range(HEAD_PAIRS):
        o = acc_ref[p] / l_ref[p]
        for half in range(2):
            h = 2 * p + half
            o_ref[:, h * hd:(h + 1) * hd] = o[:, half * tq:(half + 1) * tq].T.astype(o_ref.dtype)


def dsa_attention(q_t, k_r, v_t, qi_t, ki, w_t, y_a, w_pa, name):
    nq = SEQ // Q_BLOCK
    row = lambda b, i: b * nq + i
    pa_m = TOKENS // PA_TILE
    assert pa_m * (D_MODEL // PA_TILE) == BATCH * nq
    return pl.pallas_call(
        _dsa_kernel,
        grid=(BATCH, nq),
        in_specs=[
            pl.BlockSpec((None, HEAD_PAIRS, ATTN_HEAD_DIM, 2 * Q_BLOCK),
                         lambda b, i: (row(b, i), 0, 0, 0)),
            pl.BlockSpec((SEQ, ATTN_KV_WIDTH), lambda b, i: (b, 0)),
            pl.BlockSpec((ATTN_KV_HEADS, N_KEY_CHUNKS, ATTN_HEAD_DIM, KEY_CHUNK),
                         lambda b, i: (0, b, 0, 0)),
            pl.BlockSpec((None, IDX_PAIRS, LANES, 2 * Q_BLOCK), lambda b, i: (row(b, i), 0, 0, 0)),
            pl.BlockSpec((SEQ, LANES), lambda b, i: (b, 0)),
            pl.BlockSpec((None, IDX_HEADS, Q_BLOCK), lambda b, i: (row(b, i), 0, 0)),
            pl.BlockSpec((PA_TILE, SSM_D_INNER), lambda b, i: (row(b, i) % pa_m, 0)),
            pl.BlockSpec((SSM_D_INNER, PA_TILE), lambda b, i: (0, row(b, i) // pa_m)),
        ],
        out_specs=[pl.BlockSpec((Q_BLOCK, ATTN_WIDTH), lambda b, i: (row(b, i), 0)),
                   pl.BlockSpec((PA_TILE, PA_TILE),
                                lambda b, i: (row(b, i) % pa_m, row(b, i) // pa_m))],
        out_shape=[jax.ShapeDtypeStruct((TOKENS, ATTN_WIDTH), BF16),
                   jax.ShapeDtypeStruct((TOKENS, D_MODEL), F32)],
        scratch_shapes=[pltpu.VMEM((N_KEY_CHUNKS, KEY_CHUNK, Q_BLOCK), jnp.int32),
                        pltpu.VMEM((N_KEY_CHUNKS, KEY_CHUNK, Q_BLOCK), F32),
                        pltpu.VMEM((HEAD_PAIRS, ATTN_HEAD_DIM, 2 * Q_BLOCK), F32),
                        pltpu.VMEM((HEAD_PAIRS, 1, 2 * Q_BLOCK), F32),
                        pltpu.VMEM((HEAD_PAIRS, 1, 2 * Q_BLOCK), F32),
                        pltpu.VMEM((HEAD_PAIRS, 1, 2 * Q_BLOCK), F32),
                        pltpu.VMEM((HEAD_PAIRS, KEY_CHUNK, 2 * Q_BLOCK), F32),
                        pltpu.VMEM((HEAD_PAIRS, KEY_CHUNK, 2 * Q_BLOCK), BF16),
                        pltpu.VMEM((1, Q_BLOCK), jnp.int32)],
        compiler_params=_params("parallel", "arbitrary"),
        name=name,
    )(q_t, k_r, v_t, qi_t, ki, w_t, y_a, w_pa)


def _rope_lane_tables(positions):
    pos = positions.astype(F32).reshape(TOKENS, 1)

    def tables(rot_dim, width):
        inv_freq = ROPE_THETA ** (-(jnp.arange(0, rot_dim, 2, dtype=F32) / rot_dim))
        ang = pos * inv_freq
        cos, sin = jnp.cos(ang), jnp.sin(ang)
        half = rot_dim // 2
        ones = jnp.ones((TOKENS, width - rot_dim), F32)
        zeros = lambda n: jnp.zeros((TOKENS, n), F32)
        c = jnp.concatenate([cos, cos, ones], axis=1)
        sn = jnp.concatenate([-sin, zeros(width - half)], axis=1)
        sp = jnp.concatenate([zeros(half), sin, zeros(width - rot_dim)], axis=1)
        reps = LANES // width
        return tuple(jnp.tile(t, (1, reps)) for t in (c, sn, sp))

    return tables(ATTN_ROT_DIM, ATTN_HEAD_DIM) + tables(IDX_ROT_DIM, IDX_HEAD_DIM)


def _regroup_w_in(w):
    sizes = (SSM_HEADS, ATTN_WIDTH, ATTN_KV_WIDTH, ATTN_KV_WIDTH, IDX_WIDTH, IDX_HEAD_DIM,
             IDX_HEADS, D_MODEL, D_MODEL)
    parts = []
    off = PROJ_A_COLS
    for s in sizes:
        parts.append(w[:, off:off + s])
        off += s
    dt, q, k, v, qi, ki, wi, ga, gb = parts
    w_a = w[:, :PROJ_A_COLS].astype(BF16)
    w_b = jnp.concatenate([q, k, v, qi, ga, gb], axis=1).astype(BF16)
    zc = lambda n: jnp.zeros((D_MODEL, n), F32)
    small = jnp.concatenate(
        [ki, zc(LANES - IDX_HEAD_DIM), dt, wi, zc(SMALL_COLS - SMALL_W - IDX_HEADS)],
        axis=1).astype(BF16)
    return w_a, w_b, small


def kernel(x, positions, mix_norm, w_in, conv_w, conv_b, dt_bias, a_log, d_skip, ssm_norm,
           idx_k_norm, w_proj_a, w_proj_b, w_out, ffn_norm, w_ffn_gate, w_ffn_up, w_ffn_down,
           final_norm):
    tabs = _rope_lane_tables(positions)
    xf = x.reshape(TOKENS, D_MODEL)
    for i in range(DEPTH):
        w_a, w_b, w_small = _regroup_w_in(w_in[i])
        u = rmsnorm(xf, mix_norm[i], BF16, f"mix_norm_{i}")
        proj_a = matmul_bf16w(u, w_a, 1024, 1024, F32, f"in_proj_a_{i}")
        small = matmul_bf16w(u, w_small, 1024, SMALL_COLS, F32, f"in_proj_small_{i}")

        dt_row = small[:, SMALL_DT:SMALL_W].T
        y_a, proj_b = ssd_branch(proj_a, dt_row, conv_w[i], conv_b[i], dt_bias[i], a_log[i],
                                 d_skip[i], ssm_norm[i], u, w_b, f"ssd_{i}")

        q_t, k_r, v_t, qi_t, ki, w_t = dsa_prep(proj_b, small, tabs, idx_k_norm[i],
                                                f"dsa_prep_{i}")
        y_b, pa = dsa_attention(q_t, k_r, v_t, qi_t, ki, w_t, y_a, w_proj_a[i].astype(BF16),
                                f"dsa_{i}")

        merged = merge_branches(pa, y_b, w_proj_b, i, proj_b, 512, 1024, f"merge_{i}")
        xf = matmul_residual(merged, w_out, i, xf, 1024, 1024, f"out_proj_{i}")

        h = rmsnorm(xf, ffn_norm[i], BF16, f"ffn_norm_{i}")
        ff = ffn_up(h, w_ffn_gate, w_ffn_up, i, 1024, 512, f"ffn_up_{i}")
        xf = matmul_residual(ff, w_ffn_down, i, xf, 512, 512, f"ffn_down_{i}")
    out = rmsnorm(xf, final_norm, F32, "final_norm")
    return out.reshape(BATCH, SEQ, D_MODEL)
```

```python
import math

import jax
import jax.numpy as jnp
from jax import lax
from jax.experimental import pallas as pl
from jax.experimental.pallas import tpu as pltpu

F32 = jnp.float32
BF16 = jnp.bfloat16

D_MODEL = 2048
BATCH = 4
SEQ = 2048
DEPTH = 4
TOKENS = BATCH * SEQ
EPS = 1e-6

SSM_D_INNER = 4096
SSM_HEAD_DIM = 64
SSM_HEADS = 64
SSM_GROUPS = 8
SSM_HEADS_PER_GROUP = SSM_HEADS // SSM_GROUPS
SSM_GROUP_WIDTH = SSM_D_INNER // SSM_GROUPS
SSM_STATE = 128
SSM_CONV = 4
SSM_CHUNK = 128
SSM_CONV_DIM = SSM_D_INNER + 2 * SSM_GROUPS * SSM_STATE

ATTN_HEADS = 16
ATTN_HEAD_DIM = 128
ATTN_KV_HEADS = 4
ATTN_WIDTH = ATTN_HEADS * ATTN_HEAD_DIM
ATTN_KV_WIDTH = ATTN_KV_HEADS * ATTN_HEAD_DIM
IDX_HEADS = 16
IDX_HEAD_DIM = 64
IDX_WIDTH = IDX_HEADS * IDX_HEAD_DIM
TOP_K = min(256, SEQ // 4)
Q_BLOCK = 128
ROPE_THETA = 500000.0
ATTN_ROT_DIM = ATTN_HEAD_DIM // 4
IDX_ROT_DIM = IDX_HEAD_DIM // 4
FFN_HIDDEN = 5632

COL_Z = 0
COL_XS = COL_Z + SSM_D_INNER
COL_B = COL_XS + SSM_D_INNER
COL_C = COL_B + SSM_GROUPS * SSM_STATE
PROJ_A_COLS = COL_C + SSM_GROUPS * SSM_STATE
COL_Q = 0
COL_K = COL_Q + ATTN_WIDTH
COL_V = COL_K + ATTN_KV_WIDTH
COL_QI = COL_V + ATTN_KV_WIDTH
COL_GA = COL_QI + IDX_WIDTH
COL_GB = COL_GA + D_MODEL
PROJ_B_COLS = COL_GB + D_MODEL
SMALL_COLS = 256
SMALL_DT = 128
SMALL_W = SMALL_DT + SSM_HEADS

LANES = 128
SUBLANES = 8
VMEM_LIMIT_BYTES = 56 * 1024 * 1024

INT_MIN = -(2 ** 31)


def _params(*sem):
    return pltpu.CompilerParams(dimension_semantics=sem, vmem_limit_bytes=VMEM_LIMIT_BYTES)


def _dot(a, b):
    return jnp.dot(a, b, preferred_element_type=F32)


def _sigmoid(x):
    return 1.0 / (1.0 + jnp.exp2(x * (-math.log2(math.e))))


def _rmsnorm_kernel(x_ref, g_ref, o_ref):
    x = x_ref[...]
    ms = jnp.mean(x * x, axis=-1, keepdims=True)
    o_ref[...] = ((x * lax.rsqrt(ms + EPS)) * g_ref[...]).astype(o_ref.dtype)


def rmsnorm(x, g, out_dtype, name, tm=512):
    m, d = x.shape
    return pl.pallas_call(
        _rmsnorm_kernel,
        grid=(m // tm,),
        in_specs=[pl.BlockSpec((tm, d), lambda i: (i, 0)),
                  pl.BlockSpec((1, d), lambda i: (0, 0))],
        out_specs=pl.BlockSpec((tm, d), lambda i: (i, 0)),
        out_shape=jax.ShapeDtypeStruct((m, d), out_dtype),
        compiler_params=_params("parallel"),
        name=name,
    )(x, g.reshape(1, d))


def _cast_on_first_m_step(pairs):
    @pl.when(pl.program_id(1) == 0)
    def _():
        for w_ref, wb_ref in pairs:
            wb_ref[...] = w_ref[...].astype(BF16)


def _weight_spec(k, tn, layer):
    return pl.BlockSpec((None, k, tn), lambda j, i: (layer, 0, j))


def _mm_kernel(a_ref, w_ref, o_ref):
    o_ref[...] = _dot(a_ref[...], w_ref[...]).astype(o_ref.dtype)


def matmul_bf16w(a, w, tm, tn, out_dtype, name):
    m, k = a.shape
    n = w.shape[1]
    return pl.pallas_call(
        _mm_kernel,
        grid=(n // tn, m // tm),
        in_specs=[pl.BlockSpec((tm, k), lambda j, i: (i, 0)),
                  pl.BlockSpec((k, tn), lambda j, i: (0, j))],
        out_specs=pl.BlockSpec((tm, tn), lambda j, i: (i, j)),
        out_shape=jax.ShapeDtypeStruct((m, n), out_dtype),
        compiler_params=_params("parallel", "parallel"),
        name=name,
    )(a, w)


def _mm_small_kernel(a_ref, w_ref, o_ref, ot_ref):
    acc = _dot(a_ref[...], w_ref[...])
    o_ref[...] = acc
    ot_ref[...] = acc[:, SMALL_DT:].T


def small_projection(a, w, tm, name):
    m, k = a.shape
    return pl.pallas_call(
        _mm_small_kernel,
        grid=(m // tm,),
        in_specs=[pl.BlockSpec((tm, k), lambda i: (i, 0)),
                  pl.BlockSpec((k, SMALL_COLS), lambda i: (0, 0))],
        out_specs=[pl.BlockSpec((tm, SMALL_COLS), lambda i: (i, 0)),
                   pl.BlockSpec((SMALL_COLS - SMALL_DT, tm), lambda i: (0, i))],
        out_shape=[jax.ShapeDtypeStruct((m, SMALL_COLS), F32),
                   jax.ShapeDtypeStruct((SMALL_COLS - SMALL_DT, m), F32)],
        compiler_params=_params("parallel"),
        name=name,
    )(a, w)


def _mm_residual_kernel(a_ref, w_ref, r_ref, o_ref, wb_ref):
    _cast_on_first_m_step([(w_ref, wb_ref)])
    o_ref[...] = r_ref[...] + _dot(a_ref[...], wb_ref[...])


def matmul_residual(a, w_stack, layer, r, tm, tn, name):
    m, k = a.shape
    n = w_stack.shape[2]
    return pl.pallas_call(
        _mm_residual_kernel,
        grid=(n // tn, m // tm),
        in_specs=[pl.BlockSpec((tm, k), lambda j, i: (i, 0)),
                  _weight_spec(k, tn, layer),
                  pl.BlockSpec((tm, tn), lambda j, i: (i, j))],
        out_specs=pl.BlockSpec((tm, tn), lambda j, i: (i, j)),
        out_shape=jax.ShapeDtypeStruct((m, n), F32),
        scratch_shapes=[pltpu.VMEM((k, tn), BF16)],
        compiler_params=_params("parallel", "arbitrary"),
        name=name,
    )(a, w_stack, r)


def _ffn_up_kernel(a_ref, wg_ref, wu_ref, o_ref, wgb_ref, wub_ref):
    _cast_on_first_m_step([(wg_ref, wgb_ref), (wu_ref, wub_ref)])
    a = a_ref[...]
    g = _dot(a, wgb_ref[...])
    u = _dot(a, wub_ref[...])
    o_ref[...] = ((g * _sigmoid(g)) * u).astype(o_ref.dtype)


def ffn_up(a, wg_stack, wu_stack, layer, tm, tn, name):
    m, k = a.shape
    n = wg_stack.shape[2]
    return pl.pallas_call(
        _ffn_up_kernel,
        grid=(n // tn, m // tm),
        in_specs=[pl.BlockSpec((tm, k), lambda j, i: (i, 0)),
                  _weight_spec(k, tn, layer),
                  _weight_spec(k, tn, layer)],
        out_specs=pl.BlockSpec((tm, tn), lambda j, i: (i, j)),
        out_shape=jax.ShapeDtypeStruct((m, n), BF16),
        scratch_shapes=[pltpu.VMEM((k, tn), BF16), pltpu.VMEM((k, tn), BF16)],
        compiler_params=_params("parallel", "arbitrary"),
        name=name,
    )(a, wg_stack, wu_stack)


def _merge_kernel(pa_ref, yb_ref, wb_ref, ga_ref, gb_ref, o_ref, wbb_ref):
    _cast_on_first_m_step([(wb_ref, wbb_ref)])
    pb = _dot(yb_ref[...], wbb_ref[...])
    o_ref[...] = (_sigmoid(ga_ref[...]) * pa_ref[...]
                  + _sigmoid(gb_ref[...]) * pb).astype(o_ref.dtype)


def merge_branches(pa, yb, wb_stack, layer, proj, tm, tn, name):
    m = yb.shape[0]
    kb = wb_stack.shape[1]
    n = wb_stack.shape[2]
    ga0 = COL_GA // tn
    gb0 = COL_GB // tn
    return pl.pallas_call(
        _merge_kernel,
        grid=(n // tn, m // tm),
        in_specs=[pl.BlockSpec((tm, tn), lambda j, i: (i, j)),
                  pl.BlockSpec((tm, kb), lambda j, i: (i, 0)),
                  _weight_spec(kb, tn, layer),
                  pl.BlockSpec((tm, tn), lambda j, i: (i, ga0 + j)),
                  pl.BlockSpec((tm, tn), lambda j, i: (i, gb0 + j))],
        out_specs=pl.BlockSpec((tm, tn), lambda j, i: (i, j)),
        out_shape=jax.ShapeDtypeStruct((m, n), BF16),
        scratch_shapes=[pltpu.VMEM((kb, tn), BF16)],
        compiler_params=_params("parallel", "arbitrary"),
        name=name,
    )(pa, yb, wb_stack, proj, proj)


SSD_STEP_CHUNKS = 4
SSD_STEP_ROWS = SSD_STEP_CHUNKS * SSM_CHUNK
SSD_CONV_WIDTH = SSM_GROUP_WIDTH + 2 * SSM_STATE


def _softplus(x):
    return jnp.maximum(x, 0.0) + jnp.log1p(jnp.exp(-jnp.abs(x)))


def _split3(x):
    hi = x.astype(BF16)
    r1 = x - hi.astype(F32)
    mid = r1.astype(BF16)
    lo = (r1 - mid.astype(F32)).astype(BF16)
    return hi, mid, lo


def _ssd_kernel(x_ref, xh_ref, b_ref, bh_ref, c_ref, ch_ref, cw_ref, cbias_ref, z_ref,
                dtr_ref, biasc_ref, alogc_ref, dsk_ref, ng_ref,
                u_ref, wb_ref, o_ref, pb_ref, st_ref, pad_ref, xc_ref, y_ref):
    def shadow_slice(k):
        rows_k = slice(k * SHADOW_SLICE_ROWS, (k + 1) * SHADOW_SLICE_ROWS)
        pb_ref[rows_k, :] = _dot(u_ref[rows_k, :], wb_ref[...])

    step = pl.program_id(2)

    @pl.when(step == 0)
    def _():
        st_ref[...] = jnp.zeros_like(st_ref)

    gw = SSM_GROUP_WIDTH
    n = SSM_STATE
    lc = SSM_CHUNK
    hp = SSM_HEAD_DIM
    halo = SUBLANES

    def halo_rows(h_ref):
        first = jnp.full(h_ref.shape, step, jnp.int32) == 0
        return jnp.where(first, 0.0, h_ref[...])

    pad_ref[0:halo, 0:gw] = halo_rows(xh_ref)
    pad_ref[0:halo, gw:gw + n] = halo_rows(bh_ref)
    pad_ref[0:halo, gw + n:] = halo_rows(ch_ref)
    pad_ref[halo:, 0:gw] = x_ref[...]
    pad_ref[halo:, gw:gw + n] = b_ref[...]
    pad_ref[halo:, gw + n:] = c_ref[...]
    cw = cw_ref[...]
    cbias = cbias_ref[...]
    for r0 in range(0, SSD_STEP_ROWS, lc):
        acc = cbias + cw[SSM_CONV - 1:SSM_CONV, :] * pad_ref[halo + r0:halo + r0 + lc, :]
        for back in range(1, SSM_CONV):
            lo = halo + r0 - back
            acc = acc + cw[SSM_CONV - 1 - back:SSM_CONV - back, :] * pad_ref[lo:lo + lc, :]
        xc_ref[r0:r0 + lc, :] = acc * _sigmoid(acc)

    log2e = math.log2(math.e)
    dtr_all = _softplus(dtr_ref[...] + biasc_ref[...])
    adt_r_all = dtr_all * (-log2e * jnp.exp(alogc_ref[...]))

    ii = lax.broadcasted_iota(jnp.int32, (lc, lc), 0)
    jj = lax.broadcasted_iota(jnp.int32, (lc, lc), 1)
    tri_t = jnp.where(ii <= jj, 1.0, 0.0).astype(BF16)
    pad_heads = jnp.zeros((lc - SSM_HEADS_PER_GROUP, lc), F32)

    for cc in range(SSD_STEP_CHUNKS):
        r0 = cc * lc
        cs_r = sum(_dot(p, tri_t) for p in _split3(adt_r_all[:, r0:r0 + lc]))
        cs_c = jnp.concatenate([cs_r, pad_heads], axis=0).T
        ecs_r = jnp.exp2(cs_r)
        ecl = ecs_r[:, lc - 1:lc]
        dtr = dtr_all[:, r0:r0 + lc]
        ddr = dtr * jnp.exp2(cs_r[:, lc - 1:lc] - cs_r)

        x_t = xc_ref[r0:r0 + lc, 0:gw].T
        b16 = xc_ref[r0:r0 + lc, gw:gw + n].astype(BF16)
        cc32 = xc_ref[r0:r0 + lc, gw + n:]
        c_t = cc32.T
        cb_t = lax.dot_general(b16, cc32.astype(BF16), (((1,), (1,)), ((), ())),
                               preferred_element_type=F32)
        for h in range(SSM_HEADS_PER_GROUP):
            if h % (SSM_HEADS_PER_GROUP // SHADOW_SLICES_PER_CHUNK) == 0:
                shadow_slice(cc * SHADOW_SLICES_PER_CHUNK
                             + h // (SSM_HEADS_PER_GROUP // SHADOW_SLICES_PER_CHUNK))
            lo = h * hp
            cs_s = jnp.broadcast_to(cs_c[:, h:h + 1], (lc, lc))
            decay_t = jnp.exp2(jnp.where(ii <= jj, cs_r[h:h + 1, :] - cs_s, -jnp.inf))
            intra_t = cb_t * decay_t
            inter_t = c_t * ecs_r[h:h + 1, :]
            prev = st_ref[h]
            x_h = x_t[lo:lo + hp, :]
            y_t = _dot(
                jnp.concatenate([x_h * dtr[h:h + 1, :], prev], axis=1).astype(BF16),
                jnp.concatenate([intra_t, inter_t], axis=0).astype(BF16))
            if h % 2 == 0:
                y_even = y_t
            else:
                y_ref[r0:r0 + lc, lo - hp:lo + hp] = jnp.concatenate([y_even, y_t], axis=0).T
            st_new = _dot((x_h * ddr[h:h + 1, :]).astype(BF16), b16)
            st_ref[h] = prev * ecl[h:h + 1, :] + st_new

    y = y_ref[...] + dsk_ref[...] * xc_ref[:, 0:gw]
    z = z_ref[...]
    y = y * (z * _sigmoid(z))
    ms = jnp.mean(y * y, axis=-1, keepdims=True)
    o_ref[...] = ((y * lax.rsqrt(ms + EPS)) * ng_ref[...]).astype(o_ref.dtype)


SHADOW_TM = 1024
SHADOW_SLICES_PER_CHUNK = 1
SHADOW_SLICE_ROWS = SHADOW_TM // (SSD_STEP_CHUNKS * SHADOW_SLICES_PER_CHUNK)
SHADOW_TN = PROJ_B_COLS * TOKENS // SHADOW_TM // (BATCH * SSM_GROUPS * (SEQ // SSD_STEP_ROWS))


def ssd_branch(proj, dt_row, conv_w, conv_b, dt_bias, a_log, d_skip, norm_g, u, w_b, name):
    rows = SSD_STEP_ROWS
    m_tiles = TOKENS // SHADOW_TM
    tile_i = lambda b, g, s: ((b * SSM_GROUPS + g) * (SEQ // rows) + s) % m_tiles
    tile_j = lambda b, g, s: ((b * SSM_GROUPS + g) * (SEQ // rows) + s) // m_tiles
    steps = SEQ // rows
    gw = SSM_GROUP_WIDTH
    n = SSM_STATE
    hg = SSM_HEADS_PER_GROUP
    ng = SSM_GROUPS
    cwid = SSD_CONV_WIDTH

    def per_group(p, lead):
        xs = p[:, :SSM_D_INNER].reshape(lead, ng, gw)
        bs = p[:, SSM_D_INNER:SSM_D_INNER + ng * n].reshape(lead, ng, n)
        cs = p[:, SSM_D_INNER + ng * n:].reshape(lead, ng, n)
        return jnp.transpose(jnp.concatenate([xs, bs, cs], axis=2), (1, 0, 2))

    row = lambda b, g, s: b * steps + s
    hrow = lambda b, g, s: jnp.maximum(row(b, g, s) * (rows // SUBLANES) - 1, 0)
    return pl.pallas_call(
        _ssd_kernel,
        grid=(BATCH, ng, steps),
        in_specs=[
            pl.BlockSpec((rows, gw), lambda b, g, s: (row(b, g, s), COL_XS // gw + g)),
            pl.BlockSpec((SUBLANES, gw), lambda b, g, s: (hrow(b, g, s), COL_XS // gw + g)),
            pl.BlockSpec((rows, n), lambda b, g, s: (row(b, g, s), COL_B // n + g)),
            pl.BlockSpec((SUBLANES, n), lambda b, g, s: (hrow(b, g, s), COL_B // n + g)),
            pl.BlockSpec((rows, n), lambda b, g, s: (row(b, g, s), COL_C // n + g)),
            pl.BlockSpec((SUBLANES, n), lambda b, g, s: (hrow(b, g, s), COL_C // n + g)),
            pl.BlockSpec((None, SSM_CONV, cwid), lambda b, g, s: (g, 0, 0)),
            pl.BlockSpec((None, 1, cwid), lambda b, g, s: (g, 0, 0)),
            pl.BlockSpec((rows, gw), lambda b, g, s: (row(b, g, s), COL_Z // gw + g)),
            pl.BlockSpec((hg, rows), lambda b, g, s: (g, row(b, g, s))),
            pl.BlockSpec((None, hg, 1), lambda b, g, s: (g, 0, 0)),
            pl.BlockSpec((None, hg, 1), lambda b, g, s: (g, 0, 0)),
            pl.BlockSpec((1, gw), lambda b, g, s: (0, g)),
            pl.BlockSpec((1, gw), lambda b, g, s: (0, g)),
            pl.BlockSpec((SHADOW_TM, D_MODEL), lambda b, g, s: (tile_i(b, g, s), 0)),
            pl.BlockSpec((D_MODEL, SHADOW_TN), lambda b, g, s: (0, tile_j(b, g, s))),
        ],
        out_specs=[pl.BlockSpec((rows, gw), lambda b, g, s: (row(b, g, s), g)),
                   pl.BlockSpec((SHADOW_TM, SHADOW_TN),
                                lambda b, g, s: (tile_i(b, g, s), tile_j(b, g, s)))],
        out_shape=[jax.ShapeDtypeStruct((TOKENS, SSM_D_INNER), BF16),
                   jax.ShapeDtypeStruct((TOKENS, PROJ_B_COLS), F32)],
        scratch_shapes=[pltpu.VMEM((hg, SSM_HEAD_DIM, n), F32),
                        pltpu.VMEM((rows + SUBLANES, cwid), F32),
                        pltpu.VMEM((rows, cwid), F32),
                        pltpu.VMEM((rows, gw), F32)],
        compiler_params=_params("parallel", "parallel", "arbitrary"),
        name=name,
    )(proj, proj, proj, proj, proj, proj,
      per_group(conv_w, SSM_CONV), per_group(conv_b.reshape(1, SSM_CONV_DIM), 1),
      proj, dt_row, dt_bias.reshape(ng, hg, 1), a_log.reshape(ng, hg, 1),
      jnp.repeat(d_skip, SSM_HEAD_DIM).reshape(1, SSM_D_INNER),
      norm_g.reshape(1, SSM_D_INNER), u, w_b)


KEY_CHUNK = 256
PREP_ROWS = KEY_CHUNK
PREP_QBLOCKS = PREP_ROWS // Q_BLOCK
N_KEY_CHUNKS = SEQ // KEY_CHUNK
HEAD_PAIRS = ATTN_HEADS // 2
IDX_PAIRS = IDX_HEADS // 2
Q_PRESCALE = ATTN_HEAD_DIM ** -0.5 * math.log2(math.e)
W_IDX_SCALE = IDX_HEADS ** -0.5 * IDX_HEAD_DIM ** -0.5
M_INIT = -1e30
PA_TILE = 512
PA_SLICE = 256


def _rope(x, c, sn, sp, half):
    return x * c + pltpu.roll(x, LANES - half, 1) * sn + pltpu.roll(x, half, 1) * sp


def _prep_kernel(q_ref, k_ref, v_ref, qi_ref, smk_ref, smw_ref, ca_ref, sna_ref, spa_ref,
                 ci_ref, sni_ref, spi_ref, gk_ref,
                 qt_ref, ko_ref, vt_ref, qit_ref, kio_ref, wt_ref):
    ca, sna, spa = ca_ref[...], sna_ref[...], spa_ref[...]
    ci, sni, spi = ci_ref[...], sni_ref[...], spi_ref[...]
    ha = ATTN_ROT_DIM // 2
    hi = IDX_ROT_DIM // 2
    hd = ATTN_HEAD_DIM
    zeros_half = jnp.zeros((LANES - IDX_HEAD_DIM, Q_BLOCK), F32)
    for qq in range(PREP_QBLOCKS):
        rows = slice(qq * Q_BLOCK, (qq + 1) * Q_BLOCK)
        for h in range(ATTN_HEADS):
            qr = _rope(q_ref[rows, h * hd:(h + 1) * hd], ca[rows], sna[rows], spa[rows], ha)
            qt_ref[qq, h // 2, :, (h % 2) * Q_BLOCK:(h % 2 + 1) * Q_BLOCK] = (
                (qr * Q_PRESCALE).T.astype(BF16))
        for j in range(IDX_PAIRS):
            xt = _rope(qi_ref[rows, j * LANES:(j + 1) * LANES],
                       ci[rows], sni[rows], spi[rows], hi).T
            even = jnp.concatenate([xt[:IDX_HEAD_DIM], zeros_half], axis=0)
            odd = jnp.concatenate([xt[IDX_HEAD_DIM:], zeros_half], axis=0)
            qit_ref[qq, j, :, 0:Q_BLOCK] = even.astype(BF16)
            qit_ref[qq, j, :, Q_BLOCK:] = odd.astype(BF16)
    for g in range(ATTN_KV_HEADS):
        cols = slice(g * hd, (g + 1) * hd)
        ko_ref[:, cols] = _rope(k_ref[:, cols], ca, sna, spa, ha).astype(BF16)
        vt_ref[g, 0] = v_ref[:, cols].T.astype(BF16)
    xk = smk_ref[...]
    ms = jnp.sum(xk * xk, axis=-1, keepdims=True) * (1.0 / IDX_HEAD_DIM)
    xk = (xk * lax.rsqrt(ms + EPS)) * gk_ref[...]
    kio_ref[...] = _rope(xk, ci, sni, spi, hi).astype(BF16)
    smt = smw_ref[...].T
    w0 = SMALL_W - SMALL_DT
    for qq in range(PREP_QBLOCKS):
        wt_ref[qq] = smt[w0:w0 + IDX_HEADS, qq * Q_BLOCK:(qq + 1) * Q_BLOCK] * W_IDX_SCALE


def dsa_prep(proj, small, tabs, idx_k_norm, name):
    r = PREP_ROWS
    nqb = PREP_QBLOCKS
    gk = jnp.concatenate([idx_k_norm, jnp.zeros((LANES - IDX_HEAD_DIM,), F32)]).reshape(1, LANES)
    tab_spec = pl.BlockSpec((r, LANES), lambda i: (i, 0))
    return pl.pallas_call(
        _prep_kernel,
        grid=(TOKENS // r,),
        in_specs=[
            pl.BlockSpec((r, ATTN_WIDTH), lambda i: (i, COL_Q // ATTN_WIDTH)),
            pl.BlockSpec((r, ATTN_KV_WIDTH), lambda i: (i, COL_K // ATTN_KV_WIDTH)),
            pl.BlockSpec((r, ATTN_KV_WIDTH), lambda i: (i, COL_V // ATTN_KV_WIDTH)),
            pl.BlockSpec((r, IDX_WIDTH), lambda i: (i, COL_QI // IDX_WIDTH)),
            pl.BlockSpec((r, LANES), lambda i: (i, 0)),
            pl.BlockSpec((r, LANES), lambda i: (i, SMALL_DT // LANES)),
            tab_spec, tab_spec, tab_spec, tab_spec, tab_spec, tab_spec,
            pl.BlockSpec((1, LANES), lambda i: (0, 0)),
        ],
        out_specs=[
            pl.BlockSpec((nqb, HEAD_PAIRS, ATTN_HEAD_DIM, 2 * Q_BLOCK), lambda i: (i, 0, 0, 0)),
            pl.BlockSpec((r, ATTN_KV_WIDTH), lambda i: (i, 0)),
            pl.BlockSpec((ATTN_KV_HEADS, 1, ATTN_HEAD_DIM, r), lambda i: (0, i, 0, 0)),
            pl.BlockSpec((nqb, IDX_PAIRS, LANES, 2 * Q_BLOCK), lambda i: (i, 0, 0, 0)),
            pl.BlockSpec((r, LANES), lambda i: (i, 0)),
            pl.BlockSpec((nqb, IDX_HEADS, Q_BLOCK), lambda i: (i, 0, 0)),
        ],
        out_shape=[
            jax.ShapeDtypeStruct((TOKENS // Q_BLOCK, HEAD_PAIRS, ATTN_HEAD_DIM, 2 * Q_BLOCK), BF16),
            jax.ShapeDtypeStruct((TOKENS, ATTN_KV_WIDTH), BF16),
            jax.ShapeDtypeStruct((ATTN_KV_HEADS, TOKENS // r, ATTN_HEAD_DIM, r), BF16),
            jax.ShapeDtypeStruct((TOKENS // Q_BLOCK, IDX_PAIRS, LANES, 2 * Q_BLOCK), BF16),
            jax.ShapeDtypeStruct((TOKENS, LANES), BF16),
            jax.ShapeDtypeStruct((TOKENS // Q_BLOCK, IDX_HEADS, Q_BLOCK), F32),
        ],
        compiler_params=_params("parallel"),
        name=name,
    )(proj, proj, proj, proj, small, small, *tabs, gk)


def _col_reduce8(x, op):
    rows, lanes = x.shape
    g = x.reshape(4, rows // (4 * SUBLANES), SUBLANES, lanes)
    p = op(g, axis=1)
    red = jnp.maximum if op is jnp.max else jnp.add
    return red(red(p[0], p[1]), red(p[2], p[3]))


def _dsa_kernel(qt_ref, k_ref, vt_ref, qit_ref, ki_ref, wt_ref, ya_ref, wpa_ref, o_ref, pa_ref,
                key_ref, madd_ref, acc_ref, m_ref, l_ref, alpha_ref, s_ref, pe_ref, thr_ref,
                wpa16_ref):
    tq = Q_BLOCK
    kc = KEY_CHUNK
    hd = ATTN_HEAD_DIM
    qb = pl.program_id(1)
    n_chunks = qb // (kc // tq) + 1
    neg_inf = -jnp.inf
    q_pos = qb * tq + lax.broadcasted_iota(jnp.int32, (kc, tq), 1)
    k_off = lax.broadcasted_iota(jnp.int32, (kc, tq), 0)

    def causal(c):
        return c * kc + k_off <= q_pos

    def key_rows(c):
        return pl.ds(pl.multiple_of(c * kc, kc), kc)

    wt = wt_ref[...]

    def score_chunk(c, carry):
        ki = ki_ref[key_rows(c), :]
        score = jnp.zeros((kc, tq), F32)
        for j in range(IDX_PAIRS):
            lt = _dot(ki, qit_ref[j])
            score = score + jnp.maximum(lt[:, :tq], 0.0) * wt[2 * j:2 * j + 1, :]
            score = score + jnp.maximum(lt[:, tq:], 0.0) * wt[2 * j + 1:2 * j + 2, :]
        score = jnp.where(score == 0.0, 0.0, score)
        bits = lax.bitcast_convert_type(jnp.where(causal(c), score, neg_inf), jnp.int32)
        key_ref[c] = jnp.where(bits >= 0, bits, bits ^ jnp.int32(0x7FFFFFFF))
        return carry

    lax.fori_loop(0, n_chunks, score_chunk, 0)

    pa_parts = PA_TILE // PA_SLICE

    def pa_slice(k):
        rows_k = slice((k % pa_parts) * PA_SLICE, (k % pa_parts + 1) * PA_SLICE)
        cols_k = slice((k // pa_parts) * PA_SLICE, (k // pa_parts + 1) * PA_SLICE)
        pa_ref[rows_k, cols_k] = _dot(ya_ref[rows_k, :], wpa16_ref[:, cols_k])

    @pl.when(qb == 0)
    def _():
        wpa16_ref[...] = wpa_ref[...].astype(BF16)

    def radix_select(chunks_used):
        def pad_chunk(c, carry):
            key_ref[c] = jnp.full((kc, tq), INT_MIN, jnp.int32)
            return carry

        lax.fori_loop(n_chunks, chunks_used, pad_chunk, 0)
        t = jnp.zeros((1, tq), jnp.int32)
        for i in range(32):
            cand = t | lax.shift_left(jnp.int32(1), 31 - i)
            cand_key = cand ^ jnp.int32(INT_MIN)
            cnt8 = jnp.zeros((SUBLANES, tq), F32)
            for c in range(chunks_used):
                hit = jnp.where(key_ref[c] >= cand_key, 1.0, 0.0)
                cnt8 = cnt8 + _col_reduce8(hit, jnp.sum)
            cnt = jnp.sum(cnt8, axis=0, keepdims=True)
            t = jnp.where(cnt >= float(TOP_K), cand, t)
            if i % (32 // pa_parts ** 2) == 0:
                pa_slice(i // (32 // pa_parts ** 2))
        thr_ref[...] = t ^ jnp.int32(INT_MIN)

    half_chunks = N_KEY_CHUNKS // 2

    @pl.when(n_chunks <= half_chunks)
    def _():
        radix_select(half_chunks)

    @pl.when(n_chunks > half_chunks)
    def _():
        radix_select(N_KEY_CHUNKS)

    thr = thr_ref[...]

    def mask_chunk(c, cnt):
        ge = key_ref[c] >= thr
        ok = causal(c)
        madd_ref[c] = jnp.where(ok, jnp.where(ge, 0.0, neg_inf), neg_inf)
        return cnt + jnp.sum(jnp.where(ok, jnp.where(ge, 1.0, 0.0), 0.0), axis=0, keepdims=True)

    cnt_ge = lax.fori_loop(0, n_chunks, mask_chunk, jnp.zeros((1, tq), F32))

    @pl.when(jnp.max(cnt_ge) > float(TOP_K))
    def _():
        def count_gt(c, cnt):
            return cnt + jnp.sum(jnp.where(key_ref[c] > thr, 1.0, 0.0), axis=0, keepdims=True)

        need = float(TOP_K) - lax.fori_loop(0, n_chunks, count_gt, jnp.zeros((1, tq), F32))
        a_i = lax.broadcasted_iota(jnp.int32, (kc, kc), 0)
        a_j = lax.broadcasted_iota(jnp.int32, (kc, kc), 1)
        earlier = jnp.where(a_j < a_i, 1.0, 0.0).astype(BF16)

        def tie_chunk(c, run):
            key = key_ref[c]
            gt = key > thr
            eq = key == thr
            e = jnp.where(eq, 1.0, 0.0)
            rank = _dot(earlier, e.astype(BF16)) + run
            keep_eq = jnp.where(rank < need, 0.0, neg_inf)
            val = jnp.where(gt, 0.0, jnp.where(eq, keep_eq, neg_inf))
            madd_ref[c] = jnp.where(causal(c), val, neg_inf)
            return run + jnp.sum(e, axis=0, keepdims=True)

        lax.fori_loop(0, n_chunks, tie_chunk, jnp.zeros((1, tq), F32))

    m_ref[...] = jnp.full(m_ref.shape, M_INIT, F32)
    l_ref[...] = jnp.zeros(l_ref.shape, F32)
    acc_ref[...] = jnp.zeros(acc_ref.shape, F32)

    def attn_chunk(c, carry):
        half_rows = kc // 2
        for p in range(HEAD_PAIRS):
            g = p // 2
            top8 = None
            for rh in range(2):
                r0 = pl.multiple_of(c * kc + rh * half_rows, half_rows)
                sub = slice(rh * half_rows, (rh + 1) * half_rows)
                s = _dot(k_ref[pl.ds(r0, half_rows), g * hd:(g + 1) * hd], qt_ref[p])
                madd = madd_ref[c, sub, :]
                s = s + jnp.concatenate([madd, madd], axis=1)
                s_ref[p, sub, :] = s
                t8 = _col_reduce8(s, jnp.max)
                top8 = t8 if top8 is None else jnp.maximum(top8, t8)
            m_old = m_ref[p]
            m_new = jnp.maximum(m_old, jnp.max(top8, axis=0, keepdims=True))
            m_ref[p] = m_new
            alpha_ref[p] = jnp.exp2(m_old - m_new)
        for p in range(HEAD_PAIRS):
            sum8 = None
            for rh in range(2):
                sub = slice(rh * half_rows, (rh + 1) * half_rows)
                pe = jnp.exp2(s_ref[p, sub, :] - m_ref[p])
                pe_ref[p, sub, :] = pe.astype(BF16)
                s8 = _col_reduce8(pe, jnp.sum)
                sum8 = s8 if sum8 is None else sum8 + s8
            l_ref[p] = l_ref[p] * alpha_ref[p] + jnp.sum(sum8, axis=0, keepdims=True)
        for p in range(HEAD_PAIRS):
            acc_ref[p] = acc_ref[p] * alpha_ref[p] + _dot(vt_ref[p // 2, c], pe_ref[p])
        return carry

    lax.fori_loop(0, n_chunks, attn_chunk, 0)

    for p in range(HEAD_PAIRS):
        o = acc_ref[p] / l_ref[p]
        for half in range(2):
            h = 2 * p + half
            o_ref[:, h * hd:(h + 1) * hd] = o[:, half * tq:(half + 1) * tq].T.astype(o_ref.dtype)


def dsa_attention(q_t, k_r, v_t, qi_t, ki, w_t, y_a, w_pa_stack, layer, name):
    nq = SEQ // Q_BLOCK
    row = lambda b, i: b * nq + i
    pa_m = TOKENS // PA_TILE
    assert pa_m == nq and D_MODEL // PA_TILE == BATCH
    return pl.pallas_call(
        _dsa_kernel,
        grid=(BATCH, nq),
        in_specs=[
            pl.BlockSpec((None, HEAD_PAIRS, ATTN_HEAD_DIM, 2 * Q_BLOCK),
                         lambda b, i: (row(b, i), 0, 0, 0)),
            pl.BlockSpec((SEQ, ATTN_KV_WIDTH), lambda b, i: (b, 0)),
            pl.BlockSpec((ATTN_KV_HEADS, N_KEY_CHUNKS, ATTN_HEAD_DIM, KEY_CHUNK),
                         lambda b, i: (0, b, 0, 0)),
            pl.BlockSpec((None, IDX_PAIRS, LANES, 2 * Q_BLOCK), lambda b, i: (row(b, i), 0, 0, 0)),
            pl.BlockSpec((SEQ, LANES), lambda b, i: (b, 0)),
            pl.BlockSpec((None, IDX_HEADS, Q_BLOCK), lambda b, i: (row(b, i), 0, 0)),
            pl.BlockSpec((PA_TILE, SSM_D_INNER), lambda b, i: (row(b, i) % pa_m, 0)),
            pl.BlockSpec((None, SSM_D_INNER, PA_TILE), lambda b, i: (layer, 0, b)),
        ],
        out_specs=[pl.BlockSpec((Q_BLOCK, ATTN_WIDTH), lambda b, i: (row(b, i), 0)),
                   pl.BlockSpec((PA_TILE, PA_TILE),
                                lambda b, i: (row(b, i) % pa_m, row(b, i) // pa_m))],
        out_shape=[jax.ShapeDtypeStruct((TOKENS, ATTN_WIDTH), BF16),
                   jax.ShapeDtypeStruct((TOKENS, D_MODEL), F32)],
        scratch_shapes=[pltpu.VMEM((N_KEY_CHUNKS, KEY_CHUNK, Q_BLOCK), jnp.int32),
                        pltpu.VMEM((N_KEY_CHUNKS, KEY_CHUNK, Q_BLOCK), F32),
                        pltpu.VMEM((HEAD_PAIRS, ATTN_HEAD_DIM, 2 * Q_BLOCK), F32),
                        pltpu.VMEM((HEAD_PAIRS, 1, 2 * Q_BLOCK), F32),
                        pltpu.VMEM((HEAD_PAIRS, 1, 2 * Q_BLOCK), F32),
                        pltpu.VMEM((HEAD_PAIRS, 1, 2 * Q_BLOCK), F32),
                        pltpu.VMEM((HEAD_PAIRS, KEY_CHUNK, 2 * Q_BLOCK), F32),
                        pltpu.VMEM((HEAD_PAIRS, KEY_CHUNK, 2 * Q_BLOCK), BF16),
                        pltpu.VMEM((1, Q_BLOCK), jnp.int32),
                        pltpu.VMEM((SSM_D_INNER, PA_TILE), BF16)],
        compiler_params=_params("parallel", "arbitrary"),
        name=name,
    )(q_t, k_r, v_t, qi_t, ki, w_t, y_a, w_pa_stack)


def _rope_lane_tables(positions):
    pos = positions.astype(F32).reshape(TOKENS, 1)

    def tables(rot_dim, width):
        inv_freq = ROPE_THETA ** (-(jnp.arange(0, rot_dim, 2, dtype=F32) / rot_dim))
        ang = pos * inv_freq
        cos, sin = jnp.cos(ang), jnp.sin(ang)
        half = rot_dim // 2
        ones = jnp.ones((TOKENS, width - rot_dim), F32)
        zeros = lambda n: jnp.zeros((TOKENS, n), F32)
        c = jnp.concatenate([cos, cos, ones], axis=1)
        sn = jnp.concatenate([-sin, zeros(width - half)], axis=1)
        sp = jnp.concatenate([zeros(half), sin, zeros(width - rot_dim)], axis=1)
        reps = LANES // width
        return tuple(jnp.tile(t, (1, reps)) for t in (c, sn, sp))

    return tables(ATTN_ROT_DIM, ATTN_HEAD_DIM) + tables(IDX_ROT_DIM, IDX_HEAD_DIM)


def _regroup_w_in(w):
    sizes = (SSM_HEADS, ATTN_WIDTH, ATTN_KV_WIDTH, ATTN_KV_WIDTH, IDX_WIDTH, IDX_HEAD_DIM,
             IDX_HEADS, D_MODEL, D_MODEL)
    parts = []
    off = PROJ_A_COLS
    for s in sizes:
        parts.append(w[:, off:off + s])
        off += s
    dt, q, k, v, qi, ki, wi, ga, gb = parts
    w_a = w[:, :PROJ_A_COLS].astype(BF16)
    w_b = jnp.concatenate([q, k, v, qi, ga, gb], axis=1).astype(BF16)
    zc = lambda n: jnp.zeros((D_MODEL, n), F32)
    small = jnp.concatenate(
        [ki, zc(LANES - IDX_HEAD_DIM), dt, wi, zc(SMALL_COLS - SMALL_W - IDX_HEADS)],
        axis=1).astype(BF16)
    return w_a, w_b, small


def kernel(x, positions, mix_norm, w_in, conv_w, conv_b, dt_bias, a_log, d_skip, ssm_norm,
           idx_k_norm, w_proj_a, w_proj_b, w_out, ffn_norm, w_ffn_gate, w_ffn_up, w_ffn_down,
           final_norm):
    tabs = _rope_lane_tables(positions)
    xf = x.reshape(TOKENS, D_MODEL)
    for i in range(DEPTH):
        w_a, w_b, w_small = _regroup_w_in(w_in[i])
        u = rmsnorm(xf, mix_norm[i], BF16, f"mix_norm_{i}")
        proj_a = matmul_bf16w(u, w_a, 1024, 1024, F32, f"in_proj_a_{i}")
        small, small_t = small_projection(u, w_small, 1024, f"in_proj_small_{i}")

        y_a, proj_b = ssd_branch(proj_a, small_t, conv_w[i], conv_b[i], dt_bias[i], a_log[i],
                                 d_skip[i], ssm_norm[i], u, w_b, f"ssd_{i}")

        q_t, k_r, v_t, qi_t, ki, w_t = dsa_prep(proj_b, small, tabs, idx_k_norm[i],
                                                f"dsa_prep_{i}")
        y_b, pa = dsa_attention(q_t, k_r, v_t, qi_t, ki, w_t, y_a, w_proj_a, i, f"dsa_{i}")

        merged = merge_branches(pa, y_b, w_proj_b, i, proj_b, 512, 1024, f"merge_{i}")
        xf = matmul_residual(merged, w_out, i, xf, 1024, 1024, f"out_proj_{i}")

        h = rmsnorm(xf, ffn_norm[i], BF16, f"ffn_norm_{i}")
        ff = ffn_up(h, w_ffn_gate, w_ffn_up, i, 1024, 512, f"ffn_up_{i}")
        xf = matmul_residual(ff, w_ffn_down, i, xf, 512, 512, f"ffn_down_{i}")
    out = rmsnorm(xf, final_norm, F32, "final_norm")
    return out.reshape(BATCH, SEQ, D_MODEL)
```

```python
import math

import jax
import jax.numpy as jnp
from jax import lax
from jax.experimental import pallas as pl
from jax.experimental.pallas import tpu as pltpu

F32 = jnp.float32
BF16 = jnp.bfloat16

D_MODEL = 2048
BATCH = 4
SEQ = 2048
DEPTH = 4
TOKENS = BATCH * SEQ
EPS = 1e-6

SSM_D_INNER = 4096
SSM_HEAD_DIM = 64
SSM_HEADS = 64
SSM_GROUPS = 8
SSM_HEADS_PER_GROUP = SSM_HEADS // SSM_GROUPS
SSM_GROUP_WIDTH = SSM_D_INNER // SSM_GROUPS
SSM_STATE = 128
SSM_CONV = 4
SSM_CHUNK = 128
SSM_CONV_DIM = SSM_D_INNER + 2 * SSM_GROUPS * SSM_STATE

ATTN_HEADS = 16
ATTN_HEAD_DIM = 128
ATTN_KV_HEADS = 4
ATTN_WIDTH = ATTN_HEADS * ATTN_HEAD_DIM
ATTN_KV_WIDTH = ATTN_KV_HEADS * ATTN_HEAD_DIM
IDX_HEADS = 16
IDX_HEAD_DIM = 64
IDX_WIDTH = IDX_HEADS * IDX_HEAD_DIM
TOP_K = min(256, SEQ // 4)
Q_BLOCK = 128
ROPE_THETA = 500000.0
ATTN_ROT_DIM = ATTN_HEAD_DIM // 4
IDX_ROT_DIM = IDX_HEAD_DIM // 4
FFN_HIDDEN = 5632

COL_Z = 0
COL_XS = COL_Z + SSM_D_INNER
COL_B = COL_XS + SSM_D_INNER
COL_C = COL_B + SSM_GROUPS * SSM_STATE
PROJ_A_COLS = COL_C + SSM_GROUPS * SSM_STATE
COL_Q = 0
COL_K = COL_Q + ATTN_WIDTH
COL_V = COL_K + ATTN_KV_WIDTH
COL_QI = COL_V + ATTN_KV_WIDTH
COL_GA = COL_QI + IDX_WIDTH
COL_GB = COL_GA + D_MODEL
PROJ_B_COLS = COL_GB + D_MODEL
SMALL_COLS = 256
SMALL_DT = 128
SMALL_W = SMALL_DT + SSM_HEADS

LANES = 128
SUBLANES = 8
VMEM_LIMIT_BYTES = 56 * 1024 * 1024

INT_MIN = -(2 ** 31)


def _params(*sem):
    return pltpu.CompilerParams(dimension_semantics=sem, vmem_limit_bytes=VMEM_LIMIT_BYTES)


def _dot(a, b):
    return jnp.dot(a, b, preferred_element_type=F32)


def _sigmoid(x):
    return 1.0 / (1.0 + jnp.exp2(x * (-math.log2(math.e))))


def _rmsnorm_kernel(x_ref, g_ref, o_ref):
    x = x_ref[...]
    ms = jnp.mean(x * x, axis=-1, keepdims=True)
    o_ref[...] = ((x * lax.rsqrt(ms + EPS)) * g_ref[...]).astype(o_ref.dtype)


def rmsnorm(x, g, out_dtype, name, tm=512):
    m, d = x.shape
    return pl.pallas_call(
        _rmsnorm_kernel,
        grid=(m // tm,),
        in_specs=[pl.BlockSpec((tm, d), lambda i: (i, 0)),
                  pl.BlockSpec((1, d), lambda i: (0, 0))],
        out_specs=pl.BlockSpec((tm, d), lambda i: (i, 0)),
        out_shape=jax.ShapeDtypeStruct((m, d), out_dtype),
        compiler_params=_params("parallel"),
        name=name,
    )(x, g.reshape(1, d))


def _cast_on_first_m_step(pairs):
    @pl.when(pl.program_id(1) == 0)
    def _():
        for w_ref, wb_ref in pairs:
            wb_ref[...] = w_ref[...].astype(BF16)


def _weight_spec(k, tn, layer):
    return pl.BlockSpec((None, k, tn), lambda j, i: (layer, 0, j))


def _mm_kernel(a_ref, w_ref, o_ref):
    o_ref[...] = _dot(a_ref[...], w_ref[...]).astype(o_ref.dtype)


def matmul_bf16w(a, w, tm, tn, out_dtype, name):
    m, k = a.shape
    n = w.shape[1]
    return pl.pallas_call(
        _mm_kernel,
        grid=(n // tn, m // tm),
        in_specs=[pl.BlockSpec((tm, k), lambda j, i: (i, 0)),
                  pl.BlockSpec((k, tn), lambda j, i: (0, j))],
        out_specs=pl.BlockSpec((tm, tn), lambda j, i: (i, j)),
        out_shape=jax.ShapeDtypeStruct((m, n), out_dtype),
        compiler_params=_params("parallel", "parallel"),
        name=name,
    )(a, w)


def _mm_small_kernel(a_ref, w_ref, o_ref, ot_ref):
    acc = _dot(a_ref[...], w_ref[...])
    o_ref[...] = acc
    ot_ref[...] = acc[:, SMALL_DT:].T


def small_projection(a, w, tm, name):
    m, k = a.shape
    return pl.pallas_call(
        _mm_small_kernel,
        grid=(m // tm,),
        in_specs=[pl.BlockSpec((tm, k), lambda i: (i, 0)),
                  pl.BlockSpec((k, SMALL_COLS), lambda i: (0, 0))],
        out_specs=[pl.BlockSpec((tm, SMALL_COLS), lambda i: (i, 0)),
                   pl.BlockSpec((SMALL_COLS - SMALL_DT, tm), lambda i: (0, i))],
        out_shape=[jax.ShapeDtypeStruct((m, SMALL_COLS), F32),
                   jax.ShapeDtypeStruct((SMALL_COLS - SMALL_DT, m), F32)],
        compiler_params=_params("parallel"),
        name=name,
    )(a, w)


def _mm_residual_kernel(a_ref, w_ref, r_ref, o_ref, wb_ref):
    _cast_on_first_m_step([(w_ref, wb_ref)])
    o_ref[...] = r_ref[...] + _dot(a_ref[...], wb_ref[...])


def matmul_residual(a, w_stack, layer, r, tm, tn, name):
    m, k = a.shape
    n = w_stack.shape[2]
    return pl.pallas_call(
        _mm_residual_kernel,
        grid=(n // tn, m // tm),
        in_specs=[pl.BlockSpec((tm, k), lambda j, i: (i, 0)),
                  _weight_spec(k, tn, layer),
                  pl.BlockSpec((tm, tn), lambda j, i: (i, j))],
        out_specs=pl.BlockSpec((tm, tn), lambda j, i: (i, j)),
        out_shape=jax.ShapeDtypeStruct((m, n), F32),
        scratch_shapes=[pltpu.VMEM((k, tn), BF16)],
        compiler_params=_params("parallel", "arbitrary"),
        name=name,
    )(a, w_stack, r)


def _ffn_up_kernel(a_ref, wg_ref, wu_ref, o_ref, wgb_ref, wub_ref):
    _cast_on_first_m_step([(wg_ref, wgb_ref), (wu_ref, wub_ref)])
    a = a_ref[...]
    g = _dot(a, wgb_ref[...])
    u = _dot(a, wub_ref[...])
    o_ref[...] = ((g * _sigmoid(g)) * u).astype(o_ref.dtype)


def ffn_up(a, wg_stack, wu_stack, layer, tm, tn, name):
    m, k = a.shape
    n = wg_stack.shape[2]
    return pl.pallas_call(
        _ffn_up_kernel,
        grid=(n // tn, m // tm),
        in_specs=[pl.BlockSpec((tm, k), lambda j, i: (i, 0)),
                  _weight_spec(k, tn, layer),
                  _weight_spec(k, tn, layer)],
        out_specs=pl.BlockSpec((tm, tn), lambda j, i: (i, j)),
        out_shape=jax.ShapeDtypeStruct((m, n), BF16),
        scratch_shapes=[pltpu.VMEM((k, tn), BF16), pltpu.VMEM((k, tn), BF16)],
        compiler_params=_params("parallel", "arbitrary"),
        name=name,
    )(a, wg_stack, wu_stack)


def _merge_kernel(pa_ref, yb_ref, wb_ref, ga_ref, gb_ref, o_ref, wbb_ref):
    _cast_on_first_m_step([(wb_ref, wbb_ref)])
    pb = _dot(yb_ref[...], wbb_ref[...])
    o_ref[...] = (_sigmoid(ga_ref[...]) * pa_ref[...]
                  + _sigmoid(gb_ref[...]) * pb).astype(o_ref.dtype)


def merge_branches(pa, yb, wb_stack, layer, proj, tm, tn, name):
    m = yb.shape[0]
    kb = wb_stack.shape[1]
    n = wb_stack.shape[2]
    ga0 = COL_GA // tn
    gb0 = COL_GB // tn
    return pl.pallas_call(
        _merge_kernel,
        grid=(n // tn, m // tm),
        in_specs=[pl.BlockSpec((tm, tn), lambda j, i: (i, j)),
                  pl.BlockSpec((tm, kb), lambda j, i: (i, 0)),
                  _weight_spec(kb, tn, layer),
                  pl.BlockSpec((tm, tn), lambda j, i: (i, ga0 + j)),
                  pl.BlockSpec((tm, tn), lambda j, i: (i, gb0 + j))],
        out_specs=pl.BlockSpec((tm, tn), lambda j, i: (i, j)),
        out_shape=jax.ShapeDtypeStruct((m, n), BF16),
        scratch_shapes=[pltpu.VMEM((kb, tn), BF16)],
        compiler_params=_params("parallel", "arbitrary"),
        name=name,
    )(pa, yb, wb_stack, proj, proj)


SSD_STEP_CHUNKS = 4
SSD_STEP_ROWS = SSD_STEP_CHUNKS * SSM_CHUNK
SSD_CONV_WIDTH = SSM_GROUP_WIDTH + 2 * SSM_STATE


def _softplus(x):
    return jnp.maximum(x, 0.0) + jnp.log1p(jnp.exp(-jnp.abs(x)))


def _split3(x):
    hi = x.astype(BF16)
    r1 = x - hi.astype(F32)
    mid = r1.astype(BF16)
    lo = (r1 - mid.astype(F32)).astype(BF16)
    return hi, mid, lo


def _ssd_kernel(x_ref, xh_ref, b_ref, bh_ref, c_ref, ch_ref, cw_ref, cbias_ref, z_ref,
                dtr_ref, biasc_ref, alogc_ref, dsk_ref, ng_ref,
                u_ref, wb_ref, o_ref, pb_ref, st_ref, pad_ref, xc_ref, y_ref):
    def shadow_slice(k):
        rows_k = slice(k * SHADOW_SLICE_ROWS, (k + 1) * SHADOW_SLICE_ROWS)
        pb_ref[rows_k, :] = _dot(u_ref[rows_k, :], wb_ref[...])

    step = pl.program_id(2)

    @pl.when(step == 0)
    def _():
        st_ref[...] = jnp.zeros_like(st_ref)

    gw = SSM_GROUP_WIDTH
    n = SSM_STATE
    lc = SSM_CHUNK
    hp = SSM_HEAD_DIM
    halo = SUBLANES

    def halo_rows(h_ref):
        first = jnp.full(h_ref.shape, step, jnp.int32) == 0
        return jnp.where(first, 0.0, h_ref[...])

    pad_ref[0:halo, 0:gw] = halo_rows(xh_ref)
    pad_ref[0:halo, gw:gw + n] = halo_rows(bh_ref)
    pad_ref[0:halo, gw + n:] = halo_rows(ch_ref)
    pad_ref[halo:, 0:gw] = x_ref[...]
    pad_ref[halo:, gw:gw + n] = b_ref[...]
    pad_ref[halo:, gw + n:] = c_ref[...]
    cw = cw_ref[...]
    cbias = cbias_ref[...]
    for r0 in range(0, SSD_STEP_ROWS, lc):
        acc = cbias + cw[SSM_CONV - 1:SSM_CONV, :] * pad_ref[halo + r0:halo + r0 + lc, :]
        for back in range(1, SSM_CONV):
            lo = halo + r0 - back
            acc = acc + cw[SSM_CONV - 1 - back:SSM_CONV - back, :] * pad_ref[lo:lo + lc, :]
        xc_ref[r0:r0 + lc, :] = acc * _sigmoid(acc)

    log2e = math.log2(math.e)
    dtr_all = _softplus(dtr_ref[...] + biasc_ref[...])
    adt_r_all = dtr_all * (-log2e * jnp.exp(alogc_ref[...]))

    ii = lax.broadcasted_iota(jnp.int32, (lc, lc), 0)
    jj = lax.broadcasted_iota(jnp.int32, (lc, lc), 1)
    tri_t = jnp.where(ii <= jj, 1.0, 0.0).astype(BF16)
    pad_heads = jnp.zeros((lc - SSM_HEADS_PER_GROUP, lc), F32)

    for cc in range(SSD_STEP_CHUNKS):
        r0 = cc * lc
        cs_r = sum(_dot(p, tri_t) for p in _split3(adt_r_all[:, r0:r0 + lc]))
        cs_c = jnp.concatenate([cs_r, pad_heads], axis=0).T
        ecs_r = jnp.exp2(cs_r)
        ecl = ecs_r[:, lc - 1:lc]
        dtr = dtr_all[:, r0:r0 + lc]
        ddr = dtr * jnp.exp2(cs_r[:, lc - 1:lc] - cs_r)

        x_t = xc_ref[r0:r0 + lc, 0:gw].T
        b16 = xc_ref[r0:r0 + lc, gw:gw + n].astype(BF16)
        cc32 = xc_ref[r0:r0 + lc, gw + n:]
        c_t = cc32.T
        cb_t = lax.dot_general(b16, cc32.astype(BF16), (((1,), (1,)), ((), ())),
                               preferred_element_type=F32)
        for h in range(SSM_HEADS_PER_GROUP):
            if h % (SSM_HEADS_PER_GROUP // SHADOW_SLICES_PER_CHUNK) == 0:
                shadow_slice(cc * SHADOW_SLICES_PER_CHUNK
                             + h // (SSM_HEADS_PER_GROUP // SHADOW_SLICES_PER_CHUNK))
            lo = h * hp
            cs_s = jnp.broadcast_to(cs_c[:, h:h + 1], (lc, lc))
            decay_t = jnp.exp2(jnp.where(ii <= jj, cs_r[h:h + 1, :] - cs_s, -jnp.inf))
            intra_t = cb_t * decay_t
            inter_t = c_t * ecs_r[h:h + 1, :]
            prev = st_ref[h]
            x_h = x_t[lo:lo + hp, :]
            y_t = _dot(
                jnp.concatenate([x_h * dtr[h:h + 1, :], prev], axis=1).astype(BF16),
                jnp.concatenate([intra_t, inter_t], axis=0).astype(BF16))
            if h % 2 == 0:
                y_even = y_t
            else:
                y_ref[r0:r0 + lc, lo - hp:lo + hp] = jnp.concatenate([y_even, y_t], axis=0).T
            st_new = _dot((x_h * ddr[h:h + 1, :]).astype(BF16), b16)
            st_ref[h] = prev * ecl[h:h + 1, :] + st_new

    y = y_ref[...] + dsk_ref[...] * xc_ref[:, 0:gw]
    z = z_ref[...]
    y = y * (z * _sigmoid(z))
    ms = jnp.mean(y * y, axis=-1, keepdims=True)
    o_ref[...] = ((y * lax.rsqrt(ms + EPS)) * ng_ref[...]).astype(o_ref.dtype)


SHADOW_TM = 1024
SHADOW_SLICES_PER_CHUNK = 1
SHADOW_SLICE_ROWS = SHADOW_TM // (SSD_STEP_CHUNKS * SHADOW_SLICES_PER_CHUNK)
SHADOW_TN = PROJ_B_COLS * TOKENS // SHADOW_TM // (BATCH * SSM_GROUPS * (SEQ // SSD_STEP_ROWS))


def ssd_branch(proj, dt_row, conv_w, conv_b, dt_bias, a_log, d_skip, norm_g, u, w_b, name):
    rows = SSD_STEP_ROWS
    m_tiles = TOKENS // SHADOW_TM
    tile_i = lambda b, g, s: ((b * SSM_GROUPS + g) * (SEQ // rows) + s) % m_tiles
    tile_j = lambda b, g, s: ((b * SSM_GROUPS + g) * (SEQ // rows) + s) // m_tiles
    steps = SEQ // rows
    gw = SSM_GROUP_WIDTH
    n = SSM_STATE
    hg = SSM_HEADS_PER_GROUP
    ng = SSM_GROUPS
    cwid = SSD_CONV_WIDTH

    def per_group(p, lead):
        xs = p[:, :SSM_D_INNER].reshape(lead, ng, gw)
        bs = p[:, SSM_D_INNER:SSM_D_INNER + ng * n].reshape(lead, ng, n)
        cs = p[:, SSM_D_INNER + ng * n:].reshape(lead, ng, n)
        return jnp.transpose(jnp.concatenate([xs, bs, cs], axis=2), (1, 0, 2))

    row = lambda b, g, s: b * steps + s
    hrow = lambda b, g, s: jnp.maximum(row(b, g, s) * (rows // SUBLANES) - 1, 0)
    return pl.pallas_call(
        _ssd_kernel,
        grid=(BATCH, ng, steps),
        in_specs=[
            pl.BlockSpec((rows, gw), lambda b, g, s: (row(b, g, s), COL_XS // gw + g)),
            pl.BlockSpec((SUBLANES, gw), lambda b, g, s: (hrow(b, g, s), COL_XS // gw + g)),
            pl.BlockSpec((rows, n), lambda b, g, s: (row(b, g, s), COL_B // n + g)),
            pl.BlockSpec((SUBLANES, n), lambda b, g, s: (hrow(b, g, s), COL_B // n + g)),
            pl.BlockSpec((rows, n), lambda b, g, s: (row(b, g, s), COL_C // n + g)),
            pl.BlockSpec((SUBLANES, n), lambda b, g, s: (hrow(b, g, s), COL_C // n + g)),
            pl.BlockSpec((None, SSM_CONV, cwid), lambda b, g, s: (g, 0, 0)),
            pl.BlockSpec((None, 1, cwid), lambda b, g, s: (g, 0, 0)),
            pl.BlockSpec((rows, gw), lambda b, g, s: (row(b, g, s), COL_Z // gw + g)),
            pl.BlockSpec((hg, rows), lambda b, g, s: (g, row(b, g, s))),
            pl.BlockSpec((None, hg, 1), lambda b, g, s: (g, 0, 0)),
            pl.BlockSpec((None, hg, 1), lambda b, g, s: (g, 0, 0)),
            pl.BlockSpec((1, gw), lambda b, g, s: (0, g)),
            pl.BlockSpec((1, gw), lambda b, g, s: (0, g)),
            pl.BlockSpec((SHADOW_TM, D_MODEL), lambda b, g, s: (tile_i(b, g, s), 0)),
            pl.BlockSpec((D_MODEL, SHADOW_TN), lambda b, g, s: (0, tile_j(b, g, s))),
        ],
        out_specs=[pl.BlockSpec((rows, gw), lambda b, g, s: (row(b, g, s), g)),
                   pl.BlockSpec((SHADOW_TM, SHADOW_TN),
                                lambda b, g, s: (tile_i(b, g, s), tile_j(b, g, s)))],
        out_shape=[jax.ShapeDtypeStruct((TOKENS, SSM_D_INNER), BF16),
                   jax.ShapeDtypeStruct((TOKENS, PROJ_B_COLS), F32)],
        scratch_shapes=[pltpu.VMEM((hg, SSM_HEAD_DIM, n), F32),
                        pltpu.VMEM((rows + SUBLANES, cwid), F32),
                        pltpu.VMEM((rows, cwid), F32),
                        pltpu.VMEM((rows, gw), F32)],
        compiler_params=_params("parallel", "parallel", "arbitrary"),
        name=name,
    )(proj, proj, proj, proj, proj, proj,
      per_group(conv_w, SSM_CONV), per_group(conv_b.reshape(1, SSM_CONV_DIM), 1),
      proj, dt_row, dt_bias.reshape(ng, hg, 1), a_log.reshape(ng, hg, 1),
      jnp.repeat(d_skip, SSM_HEAD_DIM).reshape(1, SSM_D_INNER),
      norm_g.reshape(1, SSM_D_INNER), u, w_b)


KEY_CHUNK = 256
PREP_ROWS = KEY_CHUNK
PREP_QBLOCKS = PREP_ROWS // Q_BLOCK
N_KEY_CHUNKS = SEQ // KEY_CHUNK
HEAD_PAIRS = ATTN_HEADS // 2
IDX_PAIRS = IDX_HEADS // 2
Q_PRESCALE = ATTN_HEAD_DIM ** -0.5 * math.log2(math.e)
W_IDX_SCALE = IDX_HEADS ** -0.5 * IDX_HEAD_DIM ** -0.5
M_INIT = -1e30
PA_TILE = 512
PA_SLICE = 256


def _rope(x, c, sn, sp, half):
    return x * c + pltpu.roll(x, LANES - half, 1) * sn + pltpu.roll(x, half, 1) * sp


def _prep_kernel(q_ref, k_ref, v_ref, qi_ref, smk_ref, smw_ref, ca_ref, sna_ref, spa_ref,
                 ci_ref, sni_ref, spi_ref, gk_ref,
                 qt_ref, ko_ref, vt_ref, qit_ref, kio_ref, wt_ref):
    ca, sna, spa = ca_ref[...], sna_ref[...], spa_ref[...]
    ci, sni, spi = ci_ref[...], sni_ref[...], spi_ref[...]
    ha = ATTN_ROT_DIM // 2
    hi = IDX_ROT_DIM // 2
    hd = ATTN_HEAD_DIM
    zeros_half = jnp.zeros((LANES - IDX_HEAD_DIM, Q_BLOCK), F32)
    for qq in range(PREP_QBLOCKS):
        rows = slice(qq * Q_BLOCK, (qq + 1) * Q_BLOCK)
        for h in range(ATTN_HEADS):
            qr = _rope(q_ref[rows, h * hd:(h + 1) * hd], ca[rows], sna[rows], spa[rows], ha)
            qt_ref[qq, h // 2, :, (h % 2) * Q_BLOCK:(h % 2 + 1) * Q_BLOCK] = (
                (qr * Q_PRESCALE).T.astype(BF16))
        for j in range(IDX_PAIRS):
            xt = _rope(qi_ref[rows, j * LANES:(j + 1) * LANES],
                       ci[rows], sni[rows], spi[rows], hi).T
            even = jnp.concatenate([xt[:IDX_HEAD_DIM], zeros_half], axis=0)
            odd = jnp.concatenate([xt[IDX_HEAD_DIM:], zeros_half], axis=0)
            qit_ref[qq, j, :, 0:Q_BLOCK] = even.astype(BF16)
            qit_ref[qq, j, :, Q_BLOCK:] = odd.astype(BF16)
    for g in range(ATTN_KV_HEADS):
        cols = slice(g * hd, (g + 1) * hd)
        ko_ref[:, cols] = _rope(k_ref[:, cols], ca, sna, spa, ha).astype(BF16)
        vt_ref[g, 0] = v_ref[:, cols].T.astype(BF16)
    xk = smk_ref[...]
    ms = jnp.sum(xk * xk, axis=-1, keepdims=True) * (1.0 / IDX_HEAD_DIM)
    xk = (xk * lax.rsqrt(ms + EPS)) * gk_ref[...]
    kio_ref[...] = _rope(xk, ci, sni, spi, hi).astype(BF16)
    smt = smw_ref[...].T
    w0 = SMALL_W - SMALL_DT
    for qq in range(PREP_QBLOCKS):
        wt_ref[qq] = smt[w0:w0 + IDX_HEADS, qq * Q_BLOCK:(qq + 1) * Q_BLOCK] * W_IDX_SCALE


def dsa_prep(proj, small, tabs, idx_k_norm, name):
    r = PREP_ROWS
    nqb = PREP_QBLOCKS
    gk = jnp.concatenate([idx_k_norm, jnp.zeros((LANES - IDX_HEAD_DIM,), F32)]).reshape(1, LANES)
    tab_spec = pl.BlockSpec((r, LANES), lambda i: (i, 0))
    return pl.pallas_call(
        _prep_kernel,
        grid=(TOKENS // r,),
        in_specs=[
            pl.BlockSpec((r, ATTN_WIDTH), lambda i: (i, COL_Q // ATTN_WIDTH)),
            pl.BlockSpec((r, ATTN_KV_WIDTH), lambda i: (i, COL_K // ATTN_KV_WIDTH)),
            pl.BlockSpec((r, ATTN_KV_WIDTH), lambda i: (i, COL_V // ATTN_KV_WIDTH)),
            pl.BlockSpec((r, IDX_WIDTH), lambda i: (i, COL_QI // IDX_WIDTH)),
            pl.BlockSpec((r, LANES), lambda i: (i, 0)),
            pl.BlockSpec((r, LANES), lambda i: (i, SMALL_DT // LANES)),
            tab_spec, tab_spec, tab_spec, tab_spec, tab_spec, tab_spec,
            pl.BlockSpec((1, LANES), lambda i: (0, 0)),
        ],
        out_specs=[
            pl.BlockSpec((nqb, HEAD_PAIRS, ATTN_HEAD_DIM, 2 * Q_BLOCK), lambda i: (i, 0, 0, 0)),
            pl.BlockSpec((r, ATTN_KV_WIDTH), lambda i: (i, 0)),
            pl.BlockSpec((ATTN_KV_HEADS, 1, ATTN_HEAD_DIM, r), lambda i: (0, i, 0, 0)),
            pl.BlockSpec((nqb, IDX_PAIRS, LANES, 2 * Q_BLOCK), lambda i: (i, 0, 0, 0)),
            pl.BlockSpec((r, LANES), lambda i: (i, 0)),
            pl.BlockSpec((nqb, IDX_HEADS, Q_BLOCK), lambda i: (i, 0, 0)),
        ],
        out_shape=[
            jax.ShapeDtypeStruct((TOKENS // Q_BLOCK, HEAD_PAIRS, ATTN_HEAD_DIM, 2 * Q_BLOCK), BF16),
            jax.ShapeDtypeStruct((TOKENS, ATTN_KV_WIDTH), BF16),
            jax.ShapeDtypeStruct((ATTN_KV_HEADS, TOKENS // r, ATTN_HEAD_DIM, r), BF16),
            jax.ShapeDtypeStruct((TOKENS // Q_BLOCK, IDX_PAIRS, LANES, 2 * Q_BLOCK), BF16),
            jax.ShapeDtypeStruct((TOKENS, LANES), BF16),
            jax.ShapeDtypeStruct((TOKENS // Q_BLOCK, IDX_HEADS, Q_BLOCK), F32),
        ],
        compiler_params=_params("parallel"),
        name=name,
    )(proj, proj, proj, proj, small, small, *tabs, gk)


def _col_reduce8(x, op):
    rows, lanes = x.shape
    g = x.reshape(4, rows // (4 * SUBLANES), SUBLANES, lanes)
    p = op(g, axis=1)
    red = jnp.maximum if op is jnp.max else jnp.add
    return red(red(p[0], p[1]), red(p[2], p[3]))


def _dsa_kernel(qt_ref, k_ref, vt_ref, qit_ref, ki_ref, wt_ref, ya_ref, wpa_ref, o_ref, pa_ref,
                key_ref, madd_ref, acc_ref, m_ref, l_ref, alpha_ref, s_ref, pe_ref, thr_ref,
                wpa16_ref):
    tq = Q_BLOCK
    kc = KEY_CHUNK
    hd = ATTN_HEAD_DIM
    qb = pl.program_id(1)
    n_chunks = qb // (kc // tq) + 1
    neg_inf = -jnp.inf
    q_pos = qb * tq + lax.broadcasted_iota(jnp.int32, (kc, tq), 1)
    k_off = lax.broadcasted_iota(jnp.int32, (kc, tq), 0)

    def causal(c):
        return c * kc + k_off <= q_pos

    def key_rows(c):
        return pl.ds(pl.multiple_of(c * kc, kc), kc)

    wt = wt_ref[...]

    def score_chunk(c, carry):
        ki = ki_ref[key_rows(c), :]
        score = jnp.zeros((kc, tq), F32)
        for j in range(IDX_PAIRS):
            lt = _dot(ki, qit_ref[j])
            score = score + jnp.maximum(lt[:, :tq], 0.0) * wt[2 * j:2 * j + 1, :]
            score = score + jnp.maximum(lt[:, tq:], 0.0) * wt[2 * j + 1:2 * j + 2, :]
        score = jnp.where(score == 0.0, 0.0, score)
        bits = lax.bitcast_convert_type(jnp.where(causal(c), score, neg_inf), jnp.int32)
        key_ref[c] = jnp.where(bits >= 0, bits, bits ^ jnp.int32(0x7FFFFFFF))
        return carry

    lax.fori_loop(0, n_chunks, score_chunk, 0)

    pa_parts = PA_TILE // PA_SLICE

    def pa_slice(k):
        rows_k = slice((k % pa_parts) * PA_SLICE, (k % pa_parts + 1) * PA_SLICE)
        cols_k = slice((k // pa_parts) * PA_SLICE, (k // pa_parts + 1) * PA_SLICE)
        pa_ref[rows_k, cols_k] = _dot(ya_ref[rows_k, :], wpa16_ref[:, cols_k])

    @pl.when(qb == 0)
    def _():
        wpa16_ref[...] = wpa_ref[...].astype(BF16)

    def radix_select(chunks_used):
        def pad_chunk(c, carry):
            key_ref[c] = jnp.full((kc, tq), INT_MIN, jnp.int32)
            return carry

        lax.fori_loop(n_chunks, chunks_used, pad_chunk, 0)
        t = jnp.zeros((1, tq), jnp.int32)
        for i in range(32):
            cand = t | lax.shift_left(jnp.int32(1), 31 - i)
            cand_key = cand ^ jnp.int32(INT_MIN)
            cnt8 = jnp.zeros((SUBLANES, tq), F32)
            for c in range(chunks_used):
                hit = jnp.where(key_ref[c] >= cand_key, 1.0, 0.0)
                cnt8 = cnt8 + _col_reduce8(hit, jnp.sum)
            cnt = jnp.sum(cnt8, axis=0, keepdims=True)
            t = jnp.where(cnt >= float(TOP_K), cand, t)
            if i % (32 // pa_parts ** 2) == 0:
                pa_slice(i // (32 // pa_parts ** 2))
        thr_ref[...] = t ^ jnp.int32(INT_MIN)

    half_chunks = N_KEY_CHUNKS // 2

    @pl.when(n_chunks <= half_chunks)
    def _():
        radix_select(half_chunks)

    @pl.when(n_chunks > half_chunks)
    def _():
        radix_select(N_KEY_CHUNKS)

    thr = thr_ref[...]

    def mask_chunk(c, cnt):
        ge = key_ref[c] >= thr
        ok = causal(c)
        madd_ref[c] = jnp.where(ok, jnp.where(ge, 0.0, neg_inf), neg_inf)
        return cnt + jnp.sum(jnp.where(ok, jnp.where(ge, 1.0, 0.0), 0.0), axis=0, keepdims=True)

    cnt_ge = lax.fori_loop(0, n_chunks, mask_chunk, jnp.zeros((1, tq), F32))

    @pl.when(jnp.max(cnt_ge) > float(TOP_K))
    def _():
        def count_gt(c, cnt):
            return cnt + jnp.sum(jnp.where(key_ref[c] > thr, 1.0, 0.0), axis=0, keepdims=True)

        need = float(TOP_K) - lax.fori_loop(0, n_chunks, count_gt, jnp.zeros((1, tq), F32))
        a_i = lax.broadcasted_iota(jnp.int32, (kc, kc), 0)
        a_j = lax.broadcasted_iota(jnp.int32, (kc, kc), 1)
        earlier = jnp.where(a_j < a_i, 1.0, 0.0).astype(BF16)

        def tie_chunk(c, run):
            key = key_ref[c]
            gt = key > thr
            eq = key == thr
            e = jnp.where(eq, 1.0, 0.0)
            rank = _dot(earlier, e.astype(BF16)) + run
            keep_eq = jnp.where(rank < need, 0.0, neg_inf)
            val = jnp.where(gt, 0.0, jnp.where(eq, keep_eq, neg_inf))
            madd_ref[c] = jnp.where(causal(c), val, neg_inf)
            return run + jnp.sum(e, axis=0, keepdims=True)

        lax.fori_loop(0, n_chunks, tie_chunk, jnp.zeros((1, tq), F32))

    m_ref[...] = jnp.full(m_ref.shape, M_INIT, F32)
    l_ref[...] = jnp.zeros(l_ref.shape, F32)
    acc_ref[...] = jnp.zeros(acc_ref.shape, F32)

    def attn_chunk(c, carry):
        half_rows = kc // 2
        for p in range(HEAD_PAIRS):
            g = p // 2
            top8 = None
            for rh in range(2):
                r0 = pl.multiple_of(c * kc + rh * half_rows, half_rows)
                sub = slice(rh * half_rows, (rh + 1) * half_rows)
                s = _dot(k_ref[pl.ds(r0, half_rows), g * hd:(g + 1) * hd], qt_ref[p])
                madd = madd_ref[c, sub, :]
                s = s + jnp.concatenate([madd, madd], axis=1)
                s_ref[p, sub, :] = s
                t8 = _col_reduce8(s, jnp.max)
                top8 = t8 if top8 is None else jnp.maximum(top8, t8)
            m_old = m_ref[p]
            m_new = jnp.maximum(m_old, jnp.max(top8, axis=0, keepdims=True))
            m_ref[p] = m_new
            alpha_ref[p] = jnp.exp2(m_old - m_new)
        for p in range(HEAD_PAIRS):
            sum8 = None
            for rh in range(2):
                sub = slice(rh * half_rows, (rh + 1) * half_rows)
                pe = jnp.exp2(s_ref[p, sub, :] - m_ref[p])
                pe_ref[p, sub, :] = pe.astype(BF16)
                s8 = _col_reduce8(pe, jnp.sum)
                sum8 = s8 if sum8 is None else sum8 + s8
            l_ref[p] = l_ref[p] * alpha_ref[p] + jnp.sum(sum8, axis=0, keepdims=True)
        for p in range(HEAD_PAIRS):
            acc_ref[p] = acc_ref[p] * alpha_ref[p] + _dot(vt_ref[p // 2, c], pe_ref[p])
        return carry

    lax.fori_loop(0, n_chunks, attn_chunk, 0)

    for p in range(HEAD_PAIRS):
        o = acc_ref[p] / l_ref[p]
        for half in range(2):
            h = 2 * p + half
            o_ref[:, h * hd:(h + 1) * hd] = o[:, half * tq:(half + 1) * tq].T.astype(o_ref.dtype)


def dsa_attention(q_t, k_r, v_t, qi_t, ki, w_t, y_a, w_pa_stack, layer, name):
    nq = SEQ // Q_BLOCK
    row = lambda b, i: b * nq + i
    pa_m = TOKENS // PA_TILE
    assert pa_m == nq and D_MODEL // PA_TILE == BATCH
    return pl.pallas_call(
        _dsa_kernel,
        grid=(BATCH, nq),
        in_specs=[
            pl.BlockSpec((None, HEAD_PAIRS, ATTN_HEAD_DIM, 2 * Q_BLOCK),
                         lambda b, i: (row(b, i), 0, 0, 0)),
            pl.BlockSpec((SEQ, ATTN_KV_WIDTH), lambda b, i: (b, 0)),
            pl.BlockSpec((ATTN_KV_HEADS, N_KEY_CHUNKS, ATTN_HEAD_DIM, KEY_CHUNK),
                         lambda b, i: (0, b, 0, 0)),
            pl.BlockSpec((None, IDX_PAIRS, LANES, 2 * Q_BLOCK), lambda b, i: (row(b, i), 0, 0, 0)),
            pl.BlockSpec((SEQ, LANES), lambda b, i: (b, 0)),
            pl.BlockSpec((None, IDX_HEADS, Q_BLOCK), lambda b, i: (row(b, i), 0, 0)),
            pl.BlockSpec((PA_TILE, SSM_D_INNER), lambda b, i: (row(b, i) % pa_m, 0)),
            pl.BlockSpec((None, SSM_D_INNER, PA_TILE), lambda b, i: (layer, 0, b)),
        ],
        out_specs=[pl.BlockSpec((Q_BLOCK, ATTN_WIDTH), lambda b, i: (row(b, i), 0)),
                   pl.BlockSpec((PA_TILE, PA_TILE),
                                lambda b, i: (row(b, i) % pa_m, row(b, i) // pa_m))],
        out_shape=[jax.ShapeDtypeStruct((TOKENS, ATTN_WIDTH), BF16),
                   jax.ShapeDtypeStruct((TOKENS, D_MODEL), F32)],
        scratch_shapes=[pltpu.VMEM((N_KEY_CHUNKS, KEY_CHUNK, Q_BLOCK), jnp.int32),
                        pltpu.VMEM((N_KEY_CHUNKS, KEY_CHUNK, Q_BLOCK), F32),
                        pltpu.VMEM((HEAD_PAIRS, ATTN_HEAD_DIM, 2 * Q_BLOCK), F32),
                        pltpu.VMEM((HEAD_PAIRS, 1, 2 * Q_BLOCK), F32),
                        pltpu.VMEM((HEAD_PAIRS, 1, 2 * Q_BLOCK), F32),
                        pltpu.VMEM((HEAD_PAIRS, 1, 2 * Q_BLOCK), F32),
                        pltpu.VMEM((HEAD_PAIRS, KEY_CHUNK, 2 * Q_BLOCK), F32),
                        pltpu.VMEM((HEAD_PAIRS, KEY_CHUNK, 2 * Q_BLOCK), BF16),
                        pltpu.VMEM((1, Q_BLOCK), jnp.int32),
                        pltpu.VMEM((SSM_D_INNER, PA_TILE), BF16)],
        compiler_params=_params("parallel", "arbitrary"),
        name=name,
    )(q_t, k_r, v_t, qi_t, ki, w_t, y_a, w_pa_stack)


def _rope_lane_tables(positions):
    pos = positions.astype(F32).reshape(TOKENS, 1)

    def tables(rot_dim, width):
        inv_freq = ROPE_THETA ** (-(jnp.arange(0, rot_dim, 2, dtype=F32) / rot_dim))
        ang = pos * inv_freq
        cos, sin = jnp.cos(ang), jnp.sin(ang)
        half = rot_dim // 2
        ones = jnp.ones((TOKENS, width - rot_dim), F32)
        zeros = lambda n: jnp.zeros((TOKENS, n), F32)
        c = jnp.concatenate([cos, cos, ones], axis=1)
        sn = jnp.concatenate([-sin, zeros(width - half)], axis=1)
        sp = jnp.concatenate([zeros(half), sin, zeros(width - rot_dim)], axis=1)
        reps = LANES // width
        return tuple(jnp.tile(t, (1, reps)) for t in (c, sn, sp))

    return tables(ATTN_ROT_DIM, ATTN_HEAD_DIM) + tables(IDX_ROT_DIM, IDX_HEAD_DIM)


def _regroup_w_in(w):
    sizes = (SSM_HEADS, ATTN_WIDTH, ATTN_KV_WIDTH, ATTN_KV_WIDTH, IDX_WIDTH, IDX_HEAD_DIM,
             IDX_HEADS, D_MODEL, D_MODEL)
    parts = []
    off = PROJ_A_COLS
    for s in sizes:
        parts.append(w[:, off:off + s])
        off += s
    _, q, k, v, qi, _, _, ga, gb = parts
    dt = w[:, PROJ_A_COLS:PROJ_A_COLS + LANES][:, :SSM_HEADS]
    ki0 = PROJ_A_COLS + SSM_HEADS + ATTN_WIDTH + 2 * ATTN_KV_WIDTH + IDX_WIDTH
    win0 = ki0 // LANES * LANES
    win = w[:, win0:win0 + 2 * LANES]
    ki = win[:, ki0 - win0:ki0 - win0 + IDX_HEAD_DIM]
    wi = win[:, ki0 - win0 + IDX_HEAD_DIM:ki0 - win0 + IDX_HEAD_DIM + IDX_HEADS]
    w_a = w[:, :PROJ_A_COLS].astype(BF16)
    w_b = jnp.concatenate([q, k, v, qi, ga, gb], axis=1).astype(BF16)
    zc = lambda n: jnp.zeros((D_MODEL, n), F32)
    small = jnp.concatenate(
        [ki, zc(LANES - IDX_HEAD_DIM), dt, wi, zc(SMALL_COLS - SMALL_W - IDX_HEADS)],
        axis=1).astype(BF16)
    return w_a, w_b, small


def kernel(x, positions, mix_norm, w_in, conv_w, conv_b, dt_bias, a_log, d_skip, ssm_norm,
           idx_k_norm, w_proj_a, w_proj_b, w_out, ffn_norm, w_ffn_gate, w_ffn_up, w_ffn_down,
           final_norm):
    tabs = _rope_lane_tables(positions)
    xf = x.reshape(TOKENS, D_MODEL)
    for i in range(DEPTH):
        w_a, w_b, w_small = _regroup_w_in(w_in[i])
        u = rmsnorm(xf, mix_norm[i], BF16, f"mix_norm_{i}")
        proj_a = matmul_bf16w(u, w_a, 1024, 1024, F32, f"in_proj_a_{i}")
        small, small_t = small_projection(u, w_small, 1024, f"in_proj_small_{i}")

        y_a, proj_b = ssd_branch(proj_a, small_t, conv_w[i], conv_b[i], dt_bias[i], a_log[i],
                                 d_skip[i], ssm_norm[i], u, w_b, f"ssd_{i}")

        q_t, k_r, v_t, qi_t, ki, w_t = dsa_prep(proj_b, small, tabs, idx_k_norm[i],
                                                f"dsa_prep_{i}")
        y_b, pa = dsa_attention(q_t, k_r, v_t, qi_t, ki, w_t, y_a, w_proj_a, i, f"dsa_{i}")

        merged = merge_branches(pa, y_b, w_proj_b, i, proj_b, 512, 1024, f"merge_{i}")
        xf = matmul_residual(merged, w_out, i, xf, 1024, 1024, f"out_proj_{i}")

        h = rmsnorm(xf, ffn_norm[i], BF16, f"ffn_norm_{i}")
        ff = ffn_up(h, w_ffn_gate, w_ffn_up, i, 1024, 512, f"ffn_up_{i}")
        xf = matmul_residual(ff, w_ffn_down, i, xf, 512, 512, f"ffn_down_{i}")
    out = rmsnorm(xf, final_norm, F32, "final_norm")
    return out.reshape(BATCH, SEQ, D_MODEL)
```

```python
import math

import jax
import jax.numpy as jnp
from jax import lax
from jax.experimental import pallas as pl
from jax.experimental.pallas import tpu as pltpu

F32 = jnp.float32
BF16 = jnp.bfloat16

D_MODEL = 2048
BATCH = 4
SEQ = 2048
DEPTH = 4
TOKENS = BATCH * SEQ
EPS = 1e-6

SSM_D_INNER = 4096
SSM_HEAD_DIM = 64
SSM_HEADS = 64
SSM_GROUPS = 8
SSM_HEADS_PER_GROUP = SSM_HEADS // SSM_GROUPS
SSM_GROUP_WIDTH = SSM_D_INNER // SSM_GROUPS
SSM_STATE = 128
SSM_CONV = 4
SSM_CHUNK = 128
SSM_CONV_DIM = SSM_D_INNER + 2 * SSM_GROUPS * SSM_STATE

ATTN_HEADS = 16
ATTN_HEAD_DIM = 128
ATTN_KV_HEADS = 4
ATTN_WIDTH = ATTN_HEADS * ATTN_HEAD_DIM
ATTN_KV_WIDTH = ATTN_KV_HEADS * ATTN_HEAD_DIM
IDX_HEADS = 16
IDX_HEAD_DIM = 64
IDX_WIDTH = IDX_HEADS * IDX_HEAD_DIM
TOP_K = min(256, SEQ // 4)
Q_BLOCK = 128
ROPE_THETA = 500000.0
ATTN_ROT_DIM = ATTN_HEAD_DIM // 4
IDX_ROT_DIM = IDX_HEAD_DIM // 4
FFN_HIDDEN = 5632

COL_Z = 0
COL_XS = COL_Z + SSM_D_INNER
COL_B = COL_XS + SSM_D_INNER
COL_C = COL_B + SSM_GROUPS * SSM_STATE
PROJ_A_COLS = COL_C + SSM_GROUPS * SSM_STATE
COL_Q = 0
COL_K = COL_Q + ATTN_WIDTH
COL_V = COL_K + ATTN_KV_WIDTH
COL_QI = COL_V + ATTN_KV_WIDTH
COL_GA = COL_QI + IDX_WIDTH
COL_GB = COL_GA + D_MODEL
PROJ_B_COLS = COL_GB + D_MODEL
SMALL_COLS = 256
SMALL_DT = 128
SMALL_W = SMALL_DT + SSM_HEADS

LANES = 128
SUBLANES = 8
VMEM_LIMIT_BYTES = 56 * 1024 * 1024

INT_MIN = -(2 ** 31)


def _params(*sem):
    return pltpu.CompilerParams(dimension_semantics=sem, vmem_limit_bytes=VMEM_LIMIT_BYTES)


def _dot(a, b):
    return jnp.dot(a, b, preferred_element_type=F32)


def _sigmoid(x):
    return 1.0 / (1.0 + jnp.exp2(x * (-math.log2(math.e))))


def _rmsnorm_kernel(x_ref, g_ref, o_ref):
    x = x_ref[...]
    ms = jnp.mean(x * x, axis=-1, keepdims=True)
    o_ref[...] = ((x * lax.rsqrt(ms + EPS)) * g_ref[...]).astype(o_ref.dtype)


def rmsnorm(x, g, out_dtype, name, tm=512):
    m, d = x.shape
    return pl.pallas_call(
        _rmsnorm_kernel,
        grid=(m // tm,),
        in_specs=[pl.BlockSpec((tm, d), lambda i: (i, 0)),
                  pl.BlockSpec((1, d), lambda i: (0, 0))],
        out_specs=pl.BlockSpec((tm, d), lambda i: (i, 0)),
        out_shape=jax.ShapeDtypeStruct((m, d), out_dtype),
        compiler_params=_params("parallel"),
        name=name,
    )(x, g.reshape(1, d))


def _cast_on_first_m_step(pairs):
    @pl.when(pl.program_id(1) == 0)
    def _():
        for w_ref, wb_ref in pairs:
            wb_ref[...] = w_ref[...].astype(BF16)


def _weight_spec(k, tn, layer):
    return pl.BlockSpec((None, k, tn), lambda j, i: (layer, 0, j))


def _mm_kernel(a_ref, w_ref, o_ref):
    o_ref[...] = _dot(a_ref[...], w_ref[...]).astype(o_ref.dtype)


def matmul_bf16w(a, w, n, tm, tn, out_dtype, name):
    m, k = a.shape
    return pl.pallas_call(
        _mm_kernel,
        grid=(n // tn, m // tm),
        in_specs=[pl.BlockSpec((tm, k), lambda j, i: (i, 0)),
                  pl.BlockSpec((k, tn), lambda j, i: (0, j))],
        out_specs=pl.BlockSpec((tm, tn), lambda j, i: (i, j)),
        out_shape=jax.ShapeDtypeStruct((m, n), out_dtype),
        compiler_params=_params("parallel", "parallel"),
        name=name,
    )(a, w)


def _mm_small_kernel(a_ref, w_ref, o_ref, ot_ref):
    acc = _dot(a_ref[...], w_ref[...])
    o_ref[...] = acc
    ot_ref[...] = acc[:, SMALL_DT:].T


def small_projection(a, w, tm, name):
    m, k = a.shape
    return pl.pallas_call(
        _mm_small_kernel,
        grid=(m // tm,),
        in_specs=[pl.BlockSpec((tm, k), lambda i: (i, 0)),
                  pl.BlockSpec((k, SMALL_COLS), lambda i: (0, 0))],
        out_specs=[pl.BlockSpec((tm, SMALL_COLS), lambda i: (i, 0)),
                   pl.BlockSpec((SMALL_COLS - SMALL_DT, tm), lambda i: (0, i))],
        out_shape=[jax.ShapeDtypeStruct((m, SMALL_COLS), F32),
                   jax.ShapeDtypeStruct((SMALL_COLS - SMALL_DT, m), F32)],
        compiler_params=_params("parallel"),
        name=name,
    )(a, w)


def _mm_residual_kernel(a_ref, w_ref, r_ref, o_ref, wb_ref):
    _cast_on_first_m_step([(w_ref, wb_ref)])
    o_ref[...] = r_ref[...] + _dot(a_ref[...], wb_ref[...])


def matmul_residual(a, w_stack, layer, r, tm, tn, name):
    m, k = a.shape
    n = w_stack.shape[2]
    return pl.pallas_call(
        _mm_residual_kernel,
        grid=(n // tn, m // tm),
        in_specs=[pl.BlockSpec((tm, k), lambda j, i: (i, 0)),
                  _weight_spec(k, tn, layer),
                  pl.BlockSpec((tm, tn), lambda j, i: (i, j))],
        out_specs=pl.BlockSpec((tm, tn), lambda j, i: (i, j)),
        out_shape=jax.ShapeDtypeStruct((m, n), F32),
        scratch_shapes=[pltpu.VMEM((k, tn), BF16)],
        compiler_params=_params("parallel", "arbitrary"),
        name=name,
    )(a, w_stack, r)


def _ffn_up_kernel(a_ref, wg_ref, wu_ref, o_ref, wgb_ref, wub_ref):
    _cast_on_first_m_step([(wg_ref, wgb_ref), (wu_ref, wub_ref)])
    a = a_ref[...]
    g = _dot(a, wgb_ref[...])
    u = _dot(a, wub_ref[...])
    o_ref[...] = ((g * _sigmoid(g)) * u).astype(o_ref.dtype)


def ffn_up(a, wg_stack, wu_stack, layer, tm, tn, name):
    m, k = a.shape
    n = wg_stack.shape[2]
    return pl.pallas_call(
        _ffn_up_kernel,
        grid=(n // tn, m // tm),
        in_specs=[pl.BlockSpec((tm, k), lambda j, i: (i, 0)),
                  _weight_spec(k, tn, layer),
                  _weight_spec(k, tn, layer)],
        out_specs=pl.BlockSpec((tm, tn), lambda j, i: (i, j)),
        out_shape=jax.ShapeDtypeStruct((m, n), BF16),
        scratch_shapes=[pltpu.VMEM((k, tn), BF16), pltpu.VMEM((k, tn), BF16)],
        compiler_params=_params("parallel", "arbitrary"),
        name=name,
    )(a, wg_stack, wu_stack)


def _merge_kernel(pa_ref, yb_ref, wb_ref, ga_ref, gb_ref, o_ref, wbb_ref):
    _cast_on_first_m_step([(wb_ref, wbb_ref)])
    pb = _dot(yb_ref[...], wbb_ref[...])
    o_ref[...] = (_sigmoid(ga_ref[...]) * pa_ref[...]
                  + _sigmoid(gb_ref[...]) * pb).astype(o_ref.dtype)


def merge_branches(pa, yb, wb_stack, layer, proj, tm, tn, name):
    m = yb.shape[0]
    kb = wb_stack.shape[1]
    n = wb_stack.shape[2]
    ga0 = COL_GA // tn
    gb0 = COL_GB // tn
    return pl.pallas_call(
        _merge_kernel,
        grid=(n // tn, m // tm),
        in_specs=[pl.BlockSpec((tm, tn), lambda j, i: (i, j)),
                  pl.BlockSpec((tm, kb), lambda j, i: (i, 0)),
                  _weight_spec(kb, tn, layer),
                  pl.BlockSpec((tm, tn), lambda j, i: (i, ga0 + j)),
                  pl.BlockSpec((tm, tn), lambda j, i: (i, gb0 + j))],
        out_specs=pl.BlockSpec((tm, tn), lambda j, i: (i, j)),
        out_shape=jax.ShapeDtypeStruct((m, n), BF16),
        scratch_shapes=[pltpu.VMEM((kb, tn), BF16)],
        compiler_params=_params("parallel", "arbitrary"),
        name=name,
    )(pa, yb, wb_stack, proj, proj)


SSD_STEP_CHUNKS = 4
SSD_STEP_ROWS = SSD_STEP_CHUNKS * SSM_CHUNK
SSD_CONV_WIDTH = SSM_GROUP_WIDTH + 2 * SSM_STATE


def _softplus(x):
    return jnp.maximum(x, 0.0) + jnp.log1p(jnp.exp(-jnp.abs(x)))


def _split3(x):
    hi = x.astype(BF16)
    r1 = x - hi.astype(F32)
    mid = r1.astype(BF16)
    lo = (r1 - mid.astype(F32)).astype(BF16)
    return hi, mid, lo


def _ssd_kernel(x_ref, xh_ref, b_ref, bh_ref, c_ref, ch_ref, cw_ref, cbias_ref, z_ref,
                dtr_ref, biasc_ref, alogc_ref, dsk_ref, ng_ref,
                u_ref, wb_ref, o_ref, pb_ref, st_ref, pad_ref, xc_ref, y_ref):
    def shadow_slice(k):
        rows_k = slice(k * SHADOW_SLICE_ROWS, (k + 1) * SHADOW_SLICE_ROWS)
        pb_ref[rows_k, :] = _dot(u_ref[rows_k, :], wb_ref[...])

    step = pl.program_id(2)

    @pl.when(step == 0)
    def _():
        st_ref[...] = jnp.zeros_like(st_ref)

    gw = SSM_GROUP_WIDTH
    n = SSM_STATE
    lc = SSM_CHUNK
    hp = SSM_HEAD_DIM
    halo = SUBLANES

    def halo_rows(h_ref):
        first = jnp.full(h_ref.shape, step, jnp.int32) == 0
        return jnp.where(first, 0.0, h_ref[...])

    pad_ref[0:halo, 0:gw] = halo_rows(xh_ref)
    pad_ref[0:halo, gw:gw + n] = halo_rows(bh_ref)
    pad_ref[0:halo, gw + n:] = halo_rows(ch_ref)
    pad_ref[halo:, 0:gw] = x_ref[...]
    pad_ref[halo:, gw:gw + n] = b_ref[...]
    pad_ref[halo:, gw + n:] = c_ref[...]
    cw = cw_ref[...]
    cbias = cbias_ref[...]
    for r0 in range(0, SSD_STEP_ROWS, lc):
        acc = cbias + cw[SSM_CONV - 1:SSM_CONV, :] * pad_ref[halo + r0:halo + r0 + lc, :]
        for back in range(1, SSM_CONV):
            lo = halo + r0 - back
            acc = acc + cw[SSM_CONV - 1 - back:SSM_CONV - back, :] * pad_ref[lo:lo + lc, :]
        xc_ref[r0:r0 + lc, :] = acc * _sigmoid(acc)

    log2e = math.log2(math.e)
    dtr_all = _softplus(dtr_ref[...] + biasc_ref[...])
    adt_r_all = dtr_all * (-log2e * jnp.exp(alogc_ref[...]))

    ii = lax.broadcasted_iota(jnp.int32, (lc, lc), 0)
    jj = lax.broadcasted_iota(jnp.int32, (lc, lc), 1)
    tri_t = jnp.where(ii <= jj, 1.0, 0.0).astype(BF16)
    pad_heads = jnp.zeros((lc - SSM_HEADS_PER_GROUP, lc), F32)

    for cc in range(SSD_STEP_CHUNKS):
        r0 = cc * lc
        cs_r = sum(_dot(p, tri_t) for p in _split3(adt_r_all[:, r0:r0 + lc]))
        cs_c = jnp.concatenate([cs_r, pad_heads], axis=0).T
        ecs_r = jnp.exp2(cs_r)
        ecl = ecs_r[:, lc - 1:lc]
        dtr = dtr_all[:, r0:r0 + lc]
        ddr = dtr * jnp.exp2(cs_r[:, lc - 1:lc] - cs_r)

        x_t = xc_ref[r0:r0 + lc, 0:gw].T
        b16 = xc_ref[r0:r0 + lc, gw:gw + n].astype(BF16)
        cc32 = xc_ref[r0:r0 + lc, gw + n:]
        c_t = cc32.T
        cb_t = lax.dot_general(b16, cc32.astype(BF16), (((1,), (1,)), ((), ())),
                               preferred_element_type=F32)
        for h in range(SSM_HEADS_PER_GROUP):
            if h % (SSM_HEADS_PER_GROUP // SHADOW_SLICES_PER_CHUNK) == 0:
                shadow_slice(cc * SHADOW_SLICES_PER_CHUNK
                             + h // (SSM_HEADS_PER_GROUP // SHADOW_SLICES_PER_CHUNK))
            lo = h * hp
            cs_s = jnp.broadcast_to(cs_c[:, h:h + 1], (lc, lc))
            decay_t = jnp.exp2(jnp.where(ii <= jj, cs_r[h:h + 1, :] - cs_s, -jnp.inf))
            intra_t = cb_t * decay_t
            inter_t = c_t * ecs_r[h:h + 1, :]
            prev = st_ref[h]
            x_h = x_t[lo:lo + hp, :]
            y_t = _dot(
                jnp.concatenate([x_h * dtr[h:h + 1, :], prev], axis=1).astype(BF16),
                jnp.concatenate([intra_t, inter_t], axis=0).astype(BF16))
            if h % 2 == 0:
                y_even = y_t
            else:
                y_ref[r0:r0 + lc, lo - hp:lo + hp] = jnp.concatenate([y_even, y_t], axis=0).T
            st_new = _dot((x_h * ddr[h:h + 1, :]).astype(BF16), b16)
            st_ref[h] = prev * ecl[h:h + 1, :] + st_new

    y = y_ref[...] + dsk_ref[...] * xc_ref[:, 0:gw]
    z = z_ref[...]
    y = y * (z * _sigmoid(z))
    ms = jnp.mean(y * y, axis=-1, keepdims=True)
    o_ref[...] = ((y * lax.rsqrt(ms + EPS)) * ng_ref[...]).astype(o_ref.dtype)


SHADOW_TM = 1024
SHADOW_SLICES_PER_CHUNK = 1
SHADOW_SLICE_ROWS = SHADOW_TM // (SSD_STEP_CHUNKS * SHADOW_SLICES_PER_CHUNK)
SHADOW_TN = PROJ_B_COLS * TOKENS // SHADOW_TM // (BATCH * SSM_GROUPS * (SEQ // SSD_STEP_ROWS))


def ssd_branch(proj, dt_row, conv_w, conv_b, dt_bias, a_log, d_skip, norm_g, u, w_b, name):
    rows = SSD_STEP_ROWS
    m_tiles = TOKENS // SHADOW_TM
    tile_i = lambda b, g, s: ((b * SSM_GROUPS + g) * (SEQ // rows) + s) % m_tiles
    tile_j = lambda b, g, s: ((b * SSM_GROUPS + g) * (SEQ // rows) + s) // m_tiles
    steps = SEQ // rows
    gw = SSM_GROUP_WIDTH
    n = SSM_STATE
    hg = SSM_HEADS_PER_GROUP
    ng = SSM_GROUPS
    cwid = SSD_CONV_WIDTH

    def per_group(p, lead):
        xs = p[:, :SSM_D_INNER].reshape(lead, ng, gw)
        bs = p[:, SSM_D_INNER:SSM_D_INNER + ng * n].reshape(lead, ng, n)
        cs = p[:, SSM_D_INNER + ng * n:].reshape(lead, ng, n)
        return jnp.transpose(jnp.concatenate([xs, bs, cs], axis=2), (1, 0, 2))

    row = lambda b, g, s: b * steps + s
    hrow = lambda b, g, s: jnp.maximum(row(b, g, s) * (rows // SUBLANES) - 1, 0)
    return pl.pallas_call(
        _ssd_kernel,
        grid=(BATCH, ng, steps),
        in_specs=[
            pl.BlockSpec((rows, gw), lambda b, g, s: (row(b, g, s), COL_XS // gw + g)),
            pl.BlockSpec((SUBLANES, gw), lambda b, g, s: (hrow(b, g, s), COL_XS // gw + g)),
            pl.BlockSpec((rows, n), lambda b, g, s: (row(b, g, s), COL_B // n + g)),
            pl.BlockSpec((SUBLANES, n), lambda b, g, s: (hrow(b, g, s), COL_B // n + g)),
            pl.BlockSpec((rows, n), lambda b, g, s: (row(b, g, s), COL_C // n + g)),
            pl.BlockSpec((SUBLANES, n), lambda b, g, s: (hrow(b, g, s), COL_C // n + g)),
            pl.BlockSpec((None, SSM_CONV, cwid), lambda b, g, s: (g, 0, 0)),
            pl.BlockSpec((None, 1, cwid), lambda b, g, s: (g, 0, 0)),
            pl.BlockSpec((rows, gw), lambda b, g, s: (row(b, g, s), COL_Z // gw + g)),
            pl.BlockSpec((hg, rows), lambda b, g, s: (g, row(b, g, s))),
            pl.BlockSpec((None, hg, 1), lambda b, g, s: (g, 0, 0)),
            pl.BlockSpec((None, hg, 1), lambda b, g, s: (g, 0, 0)),
            pl.BlockSpec((1, gw), lambda b, g, s: (0, g)),
            pl.BlockSpec((1, gw), lambda b, g, s: (0, g)),
            pl.BlockSpec((SHADOW_TM, D_MODEL), lambda b, g, s: (tile_i(b, g, s), 0)),
            pl.BlockSpec((D_MODEL, SHADOW_TN), lambda b, g, s: (0, tile_j(b, g, s))),
        ],
        out_specs=[pl.BlockSpec((rows, gw), lambda b, g, s: (row(b, g, s), g)),
                   pl.BlockSpec((SHADOW_TM, SHADOW_TN),
                                lambda b, g, s: (tile_i(b, g, s), tile_j(b, g, s)))],
        out_shape=[jax.ShapeDtypeStruct((TOKENS, SSM_D_INNER), BF16),
                   jax.ShapeDtypeStruct((TOKENS, PROJ_B_COLS), F32)],
        scratch_shapes=[pltpu.VMEM((hg, SSM_HEAD_DIM, n), F32),
                        pltpu.VMEM((rows + SUBLANES, cwid), F32),
                        pltpu.VMEM((rows, cwid), F32),
                        pltpu.VMEM((rows, gw), F32)],
        compiler_params=_params("parallel", "parallel", "arbitrary"),
        name=name,
    )(proj, proj, proj, proj, proj, proj,
      per_group(conv_w, SSM_CONV), per_group(conv_b.reshape(1, SSM_CONV_DIM), 1),
      proj, dt_row, dt_bias.reshape(ng, hg, 1), a_log.reshape(ng, hg, 1),
      jnp.repeat(d_skip, SSM_HEAD_DIM).reshape(1, SSM_D_INNER),
      norm_g.reshape(1, SSM_D_INNER), u, w_b)


KEY_CHUNK = 256
PREP_ROWS = KEY_CHUNK
PREP_QBLOCKS = PREP_ROWS // Q_BLOCK
N_KEY_CHUNKS = SEQ // KEY_CHUNK
HEAD_PAIRS = ATTN_HEADS // 2
IDX_PAIRS = IDX_HEADS // 2
Q_PRESCALE = ATTN_HEAD_DIM ** -0.5 * math.log2(math.e)
W_IDX_SCALE = IDX_HEADS ** -0.5 * IDX_HEAD_DIM ** -0.5
M_INIT = -1e30
PA_TILE = 512
PA_SLICE = 256


def _rope(x, c, sn, sp, half):
    return x * c + pltpu.roll(x, LANES - half, 1) * sn + pltpu.roll(x, half, 1) * sp


def _prep_kernel(q_ref, k_ref, v_ref, qi_ref, smk_ref, smw_ref, ca_ref, sna_ref, spa_ref,
                 ci_ref, sni_ref, spi_ref, gk_ref,
                 qt_ref, ko_ref, vt_ref, qit_ref, kio_ref, wt_ref):
    ca, sna, spa = ca_ref[...], sna_ref[...], spa_ref[...]
    ci, sni, spi = ci_ref[...], sni_ref[...], spi_ref[...]
    ha = ATTN_ROT_DIM // 2
    hi = IDX_ROT_DIM // 2
    hd = ATTN_HEAD_DIM
    zeros_half = jnp.zeros((LANES - IDX_HEAD_DIM, Q_BLOCK), F32)
    for qq in range(PREP_QBLOCKS):
        rows = slice(qq * Q_BLOCK, (qq + 1) * Q_BLOCK)
        for h in range(ATTN_HEADS):
            qr = _rope(q_ref[rows, h * hd:(h + 1) * hd], ca[rows], sna[rows], spa[rows], ha)
            qt_ref[qq, h // 2, :, (h % 2) * Q_BLOCK:(h % 2 + 1) * Q_BLOCK] = (
                (qr * Q_PRESCALE).T.astype(BF16))
        for j in range(IDX_PAIRS):
            xt = _rope(qi_ref[rows, j * LANES:(j + 1) * LANES],
                       ci[rows], sni[rows], spi[rows], hi).T
            even = jnp.concatenate([xt[:IDX_HEAD_DIM], zeros_half], axis=0)
            odd = jnp.concatenate([xt[IDX_HEAD_DIM:], zeros_half], axis=0)
            qit_ref[qq, j, :, 0:Q_BLOCK] = even.astype(BF16)
            qit_ref[qq, j, :, Q_BLOCK:] = odd.astype(BF16)
    for g in range(ATTN_KV_HEADS):
        cols = slice(g * hd, (g + 1) * hd)
        ko_ref[:, cols] = _rope(k_ref[:, cols], ca, sna, spa, ha).astype(BF16)
        vt_ref[g, 0] = v_ref[:, cols].T.astype(BF16)
    xk = smk_ref[...]
    ms = jnp.sum(xk * xk, axis=-1, keepdims=True) * (1.0 / IDX_HEAD_DIM)
    xk = (xk * lax.rsqrt(ms + EPS)) * gk_ref[...]
    kio_ref[...] = _rope(xk, ci, sni, spi, hi).astype(BF16)
    smt = smw_ref[...].T
    w0 = SMALL_W - SMALL_DT
    for qq in range(PREP_QBLOCKS):
        wt_ref[qq] = smt[w0:w0 + IDX_HEADS, qq * Q_BLOCK:(qq + 1) * Q_BLOCK] * W_IDX_SCALE


def dsa_prep(proj, small, tabs, idx_k_norm, name):
    r = PREP_ROWS
    nqb = PREP_QBLOCKS
    gk = jnp.concatenate([idx_k_norm, jnp.zeros((LANES - IDX_HEAD_DIM,), F32)]).reshape(1, LANES)
    tab_spec = pl.BlockSpec((r, LANES), lambda i: (i, 0))
    return pl.pallas_call(
        _prep_kernel,
        grid=(TOKENS // r,),
        in_specs=[
            pl.BlockSpec((r, ATTN_WIDTH), lambda i: (i, COL_Q // ATTN_WIDTH)),
            pl.BlockSpec((r, ATTN_KV_WIDTH), lambda i: (i, COL_K // ATTN_KV_WIDTH)),
            pl.BlockSpec((r, ATTN_KV_WIDTH), lambda i: (i, COL_V // ATTN_KV_WIDTH)),
            pl.BlockSpec((r, IDX_WIDTH), lambda i: (i, COL_QI // IDX_WIDTH)),
            pl.BlockSpec((r, LANES), lambda i: (i, 0)),
            pl.BlockSpec((r, LANES), lambda i: (i, SMALL_DT // LANES)),
            tab_spec, tab_spec, tab_spec, tab_spec, tab_spec, tab_spec,
            pl.BlockSpec((1, LANES), lambda i: (0, 0)),
        ],
        out_specs=[
            pl.BlockSpec((nqb, HEAD_PAIRS, ATTN_HEAD_DIM, 2 * Q_BLOCK), lambda i: (i, 0, 0, 0)),
            pl.BlockSpec((r, ATTN_KV_WIDTH), lambda i: (i, 0)),
            pl.BlockSpec((ATTN_KV_HEADS, 1, ATTN_HEAD_DIM, r), lambda i: (0, i, 0, 0)),
            pl.BlockSpec((nqb, IDX_PAIRS, LANES, 2 * Q_BLOCK), lambda i: (i, 0, 0, 0)),
            pl.BlockSpec((r, LANES), lambda i: (i, 0)),
            pl.BlockSpec((nqb, IDX_HEADS, Q_BLOCK), lambda i: (i, 0, 0)),
        ],
        out_shape=[
            jax.ShapeDtypeStruct((TOKENS // Q_BLOCK, HEAD_PAIRS, ATTN_HEAD_DIM, 2 * Q_BLOCK), BF16),
            jax.ShapeDtypeStruct((TOKENS, ATTN_KV_WIDTH), BF16),
            jax.ShapeDtypeStruct((ATTN_KV_HEADS, TOKENS // r, ATTN_HEAD_DIM, r), BF16),
            jax.ShapeDtypeStruct((TOKENS // Q_BLOCK, IDX_PAIRS, LANES, 2 * Q_BLOCK), BF16),
            jax.ShapeDtypeStruct((TOKENS, LANES), BF16),
            jax.ShapeDtypeStruct((TOKENS // Q_BLOCK, IDX_HEADS, Q_BLOCK), F32),
        ],
        compiler_params=_params("parallel"),
        name=name,
    )(proj, proj, proj, proj, small, small, *tabs, gk)


def _col_reduce8(x, op):
    rows, lanes = x.shape
    g = x.reshape(4, rows // (4 * SUBLANES), SUBLANES, lanes)
    p = op(g, axis=1)
    red = jnp.maximum if op is jnp.max else jnp.add
    return red(red(p[0], p[1]), red(p[2], p[3]))


def _dsa_kernel(qt_ref, k_ref, vt_ref, qit_ref, ki_ref, wt_ref, ya_ref, wpa_ref, o_ref, pa_ref,
                key_ref, madd_ref, acc_ref, m_ref, l_ref, alpha_ref, s_ref, pe_ref, thr_ref,
                wpa16_ref):
    tq = Q_BLOCK
    kc = KEY_CHUNK
    hd = ATTN_HEAD_DIM
    qb = pl.program_id(1)
    n_chunks = qb // (kc // tq) + 1
    neg_inf = -jnp.inf
    q_pos = qb * tq + lax.broadcasted_iota(jnp.int32, (kc, tq), 1)
    k_off = lax.broadcasted_iota(jnp.int32, (kc, tq), 0)

    def causal(c):
        return c * kc + k_off <= q_pos

    def key_rows(c):
        return pl.ds(pl.multiple_of(c * kc, kc), kc)

    wt = wt_ref[...]

    def score_chunk(c, carry):
        ki = ki_ref[key_rows(c), :]
        score = jnp.zeros((kc, tq), F32)
        for j in range(IDX_PAIRS):
            lt = _dot(ki, qit_ref[j])
            score = score + jnp.maximum(lt[:, :tq], 0.0) * wt[2 * j:2 * j + 1, :]
            score = score + jnp.maximum(lt[:, tq:], 0.0) * wt[2 * j + 1:2 * j + 2, :]
        score = jnp.where(score == 0.0, 0.0, score)
        bits = lax.bitcast_convert_type(jnp.where(causal(c), score, neg_inf), jnp.int32)
        key_ref[c] = jnp.where(bits >= 0, bits, bits ^ jnp.int32(0x7FFFFFFF))
        return carry

    lax.fori_loop(0, n_chunks, score_chunk, 0)

    pa_parts = PA_TILE // PA_SLICE

    def pa_slice(k):
        rows_k = slice((k % pa_parts) * PA_SLICE, (k % pa_parts + 1) * PA_SLICE)
        cols_k = slice((k // pa_parts) * PA_SLICE, (k // pa_parts + 1) * PA_SLICE)
        pa_ref[rows_k, cols_k] = _dot(ya_ref[rows_k, :], wpa16_ref[:, cols_k])

    @pl.when(qb == 0)
    def _():
        wpa16_ref[...] = wpa_ref[...].astype(BF16)

    def radix_select(chunks_used):
        def pad_chunk(c, carry):
            key_ref[c] = jnp.full((kc, tq), INT_MIN, jnp.int32)
            return carry

        lax.fori_loop(n_chunks, chunks_used, pad_chunk, 0)
        t = jnp.zeros((1, tq), jnp.int32)
        for i in range(32):
            cand = t | lax.shift_left(jnp.int32(1), 31 - i)
            cand_key = cand ^ jnp.int32(INT_MIN)
            cnt8 = jnp.zeros((SUBLANES, tq), F32)
            for c in range(chunks_used):
                hit = jnp.where(key_ref[c] >= cand_key, 1.0, 0.0)
                cnt8 = cnt8 + _col_reduce8(hit, jnp.sum)
            cnt = jnp.sum(cnt8, axis=0, keepdims=True)
            t = jnp.where(cnt >= float(TOP_K), cand, t)
            if i % (32 // pa_parts ** 2) == 0:
                pa_slice(i // (32 // pa_parts ** 2))
        thr_ref[...] = t ^ jnp.int32(INT_MIN)

    half_chunks = N_KEY_CHUNKS // 2

    @pl.when(n_chunks <= half_chunks)
    def _():
        radix_select(half_chunks)

    @pl.when(n_chunks > half_chunks)
    def _():
        radix_select(N_KEY_CHUNKS)

    thr = thr_ref[...]

    def mask_chunk(c, cnt):
        ge = key_ref[c] >= thr
        ok = causal(c)
        madd_ref[c] = jnp.where(ok, jnp.where(ge, 0.0, neg_inf), neg_inf)
        return cnt + jnp.sum(jnp.where(ok, jnp.where(ge, 1.0, 0.0), 0.0), axis=0, keepdims=True)

    cnt_ge = lax.fori_loop(0, n_chunks, mask_chunk, jnp.zeros((1, tq), F32))

    @pl.when(jnp.max(cnt_ge) > float(TOP_K))
    def _():
        def count_gt(c, cnt):
            return cnt + jnp.sum(jnp.where(key_ref[c] > thr, 1.0, 0.0), axis=0, keepdims=True)

        need = float(TOP_K) - lax.fori_loop(0, n_chunks, count_gt, jnp.zeros((1, tq), F32))
        a_i = lax.broadcasted_iota(jnp.int32, (kc, kc), 0)
        a_j = lax.broadcasted_iota(jnp.int32, (kc, kc), 1)
        earlier = jnp.where(a_j < a_i, 1.0, 0.0).astype(BF16)

        def tie_chunk(c, run):
            key = key_ref[c]
            gt = key > thr
            eq = key == thr
            e = jnp.where(eq, 1.0, 0.0)
            rank = _dot(earlier, e.astype(BF16)) + run
            keep_eq = jnp.where(rank < need, 0.0, neg_inf)
            val = jnp.where(gt, 0.0, jnp.where(eq, keep_eq, neg_inf))
            madd_ref[c] = jnp.where(causal(c), val, neg_inf)
            return run + jnp.sum(e, axis=0, keepdims=True)

        lax.fori_loop(0, n_chunks, tie_chunk, jnp.zeros((1, tq), F32))

    m_ref[...] = jnp.full(m_ref.shape, M_INIT, F32)
    l_ref[...] = jnp.zeros(l_ref.shape, F32)
    acc_ref[...] = jnp.zeros(acc_ref.shape, F32)

    def attn_chunk(c, carry):
        half_rows = kc // 2
        for p in range(HEAD_PAIRS):
            g = p // 2
            top8 = None
            for rh in range(2):
                r0 = pl.multiple_of(c * kc + rh * half_rows, half_rows)
                sub = slice(rh * half_rows, (rh + 1) * half_rows)
                s = _dot(k_ref[pl.ds(r0, half_rows), g * hd:(g + 1) * hd], qt_ref[p])
                madd = madd_ref[c, sub, :]
                s = s + jnp.concatenate([madd, madd], axis=1)
                s_ref[p, sub, :] = s
                t8 = _col_reduce8(s, jnp.max)
                top8 = t8 if top8 is None else jnp.maximum(top8, t8)
            m_old = m_ref[p]
            m_new = jnp.maximum(m_old, jnp.max(top8, axis=0, keepdims=True))
            m_ref[p] = m_new
            alpha_ref[p] = jnp.exp2(m_old - m_new)
        for p in range(HEAD_PAIRS):
            sum8 = None
            for rh in range(2):
                sub = slice(rh * half_rows, (rh + 1) * half_rows)
                pe = jnp.exp2(s_ref[p, sub, :] - m_ref[p])
                pe_ref[p, sub, :] = pe.astype(BF16)
                s8 = _col_reduce8(pe, jnp.sum)
                sum8 = s8 if sum8 is None else sum8 + s8
            l_ref[p] = l_ref[p] * alpha_ref[p] + jnp.sum(sum8, axis=0, keepdims=True)
        for p in range(HEAD_PAIRS):
            acc_ref[p] = acc_ref[p] * alpha_ref[p] + _dot(vt_ref[p // 2, c], pe_ref[p])
        return carry

    lax.fori_loop(0, n_chunks, attn_chunk, 0)

    for p in range(HEAD_PAIRS):
        o = acc_ref[p] / l_ref[p]
        for half in range(2):
            h = 2 * p + half
            o_ref[:, h * hd:(h + 1) * hd] = o[:, half * tq:(half + 1) * tq].T.astype(o_ref.dtype)


def dsa_attention(q_t, k_r, v_t, qi_t, ki, w_t, y_a, w_pa_stack, layer, name):
    nq = SEQ // Q_BLOCK
    row = lambda b, i: b * nq + i
    pa_m = TOKENS // PA_TILE
    assert pa_m == nq and D_MODEL // PA_TILE == BATCH
    return pl.pallas_call(
        _dsa_kernel,
        grid=(BATCH, nq),
        in_specs=[
            pl.BlockSpec((None, HEAD_PAIRS, ATTN_HEAD_DIM, 2 * Q_BLOCK),
                         lambda b, i: (row(b, i), 0, 0, 0)),
            pl.BlockSpec((SEQ, ATTN_KV_WIDTH), lambda b, i: (b, 0)),
            pl.BlockSpec((ATTN_KV_HEADS, N_KEY_CHUNKS, ATTN_HEAD_DIM, KEY_CHUNK),
                         lambda b, i: (0, b, 0, 0)),
            pl.BlockSpec((None, IDX_PAIRS, LANES, 2 * Q_BLOCK), lambda b, i: (row(b, i), 0, 0, 0)),
            pl.BlockSpec((SEQ, LANES), lambda b, i: (b, 0)),
            pl.BlockSpec((None, IDX_HEADS, Q_BLOCK), lambda b, i: (row(b, i), 0, 0)),
            pl.BlockSpec((PA_TILE, SSM_D_INNER), lambda b, i: (row(b, i) % pa_m, 0)),
            pl.BlockSpec((None, SSM_D_INNER, PA_TILE), lambda b, i: (layer, 0, b)),
        ],
        out_specs=[pl.BlockSpec((Q_BLOCK, ATTN_WIDTH), lambda b, i: (row(b, i), 0)),
                   pl.BlockSpec((PA_TILE, PA_TILE),
                                lambda b, i: (row(b, i) % pa_m, row(b, i) // pa_m))],
        out_shape=[jax.ShapeDtypeStruct((TOKENS, ATTN_WIDTH), BF16),
                   jax.ShapeDtypeStruct((TOKENS, D_MODEL), F32)],
        scratch_shapes=[pltpu.VMEM((N_KEY_CHUNKS, KEY_CHUNK, Q_BLOCK), jnp.int32),
                        pltpu.VMEM((N_KEY_CHUNKS, KEY_CHUNK, Q_BLOCK), F32),
                        pltpu.VMEM((HEAD_PAIRS, ATTN_HEAD_DIM, 2 * Q_BLOCK), F32),
                        pltpu.VMEM((HEAD_PAIRS, 1, 2 * Q_BLOCK), F32),
                        pltpu.VMEM((HEAD_PAIRS, 1, 2 * Q_BLOCK), F32),
                        pltpu.VMEM((HEAD_PAIRS, 1, 2 * Q_BLOCK), F32),
                        pltpu.VMEM((HEAD_PAIRS, KEY_CHUNK, 2 * Q_BLOCK), F32),
                        pltpu.VMEM((HEAD_PAIRS, KEY_CHUNK, 2 * Q_BLOCK), BF16),
                        pltpu.VMEM((1, Q_BLOCK), jnp.int32),
                        pltpu.VMEM((SSM_D_INNER, PA_TILE), BF16)],
        compiler_params=_params("parallel", "arbitrary"),
        name=name,
    )(q_t, k_r, v_t, qi_t, ki, w_t, y_a, w_pa_stack)


def _rope_lane_tables(positions):
    pos = positions.astype(F32).reshape(TOKENS, 1)

    def tables(rot_dim, width):
        inv_freq = ROPE_THETA ** (-(jnp.arange(0, rot_dim, 2, dtype=F32) / rot_dim))
        ang = pos * inv_freq
        cos, sin = jnp.cos(ang), jnp.sin(ang)
        half = rot_dim // 2
        ones = jnp.ones((TOKENS, width - rot_dim), F32)
        zeros = lambda n: jnp.zeros((TOKENS, n), F32)
        c = jnp.concatenate([cos, cos, ones], axis=1)
        sn = jnp.concatenate([-sin, zeros(width - half)], axis=1)
        sp = jnp.concatenate([zeros(half), sin, zeros(width - rot_dim)], axis=1)
        reps = LANES // width
        return tuple(jnp.tile(t, (1, reps)) for t in (c, sn, sp))

    return tables(ATTN_ROT_DIM, ATTN_HEAD_DIM) + tables(IDX_ROT_DIM, IDX_HEAD_DIM)


def _regroup_w_in(w):
    w16 = w.astype(BF16)
    q0 = PROJ_A_COLS + SSM_HEADS
    ki0 = q0 + ATTN_WIDTH + 2 * ATTN_KV_WIDTH + IDX_WIDTH
    ga0 = ki0 + IDX_HEAD_DIM + IDX_HEADS
    w_b = jnp.concatenate([w16[:, q0:ki0], w16[:, ga0:ga0 + 2 * D_MODEL]], axis=1)
    dt = w16[:, PROJ_A_COLS:PROJ_A_COLS + LANES][:, :SSM_HEADS]
    win0 = ki0 // LANES * LANES
    win = w16[:, win0:win0 + 2 * LANES]
    ki = win[:, ki0 - win0:ki0 - win0 + IDX_HEAD_DIM]
    wi = win[:, ki0 - win0 + IDX_HEAD_DIM:ki0 - win0 + IDX_HEAD_DIM + IDX_HEADS]
    zc = lambda n: jnp.zeros((D_MODEL, n), BF16)
    small = jnp.concatenate(
        [ki, zc(LANES - IDX_HEAD_DIM), dt, wi, zc(SMALL_COLS - SMALL_W - IDX_HEADS)], axis=1)
    return w16, w_b, small


def kernel(x, positions, mix_norm, w_in, conv_w, conv_b, dt_bias, a_log, d_skip, ssm_norm,
           idx_k_norm, w_proj_a, w_proj_b, w_out, ffn_norm, w_ffn_gate, w_ffn_up, w_ffn_down,
           final_norm):
    tabs = _rope_lane_tables(positions)
    xf = x.reshape(TOKENS, D_MODEL)
    for i in range(DEPTH):
        w_a, w_b, w_small = _regroup_w_in(w_in[i])
        u = rmsnorm(xf, mix_norm[i], BF16, f"mix_norm_{i}")
        proj_a = matmul_bf16w(u, w_a, PROJ_A_COLS, 1024, 1024, F32, f"in_proj_a_{i}")
        small, small_t = small_projection(u, w_small, 1024, f"in_proj_small_{i}")

        y_a, proj_b = ssd_branch(proj_a, small_t, conv_w[i], conv_b[i], dt_bias[i], a_log[i],
                                 d_skip[i], ssm_norm[i], u, w_b, f"ssd_{i}")

        q_t, k_r, v_t, qi_t, ki, w_t = dsa_prep(proj_b, small, tabs, idx_k_norm[i],
                                                f"dsa_prep_{i}")
        y_b, pa = dsa_attention(q_t, k_r, v_t, qi_t, ki, w_t, y_a, w_proj_a, i, f"dsa_{i}")

        merged = merge_branches(pa, y_b, w_proj_b, i, proj_b, 512, 1024, f"merge_{i}")
        xf = matmul_residual(merged, w_out, i, xf, 1024, 1024, f"out_proj_{i}")

        h = rmsnorm(xf, ffn_norm[i], BF16, f"ffn_norm_{i}")
        ff = ffn_up(h, w_ffn_gate, w_ffn_up, i, 1024, 512, f"ffn_up_{i}")
        xf = matmul_residual(ff, w_ffn_down, i, xf, 512, 512, f"ffn_down_{i}")
    out = rmsnorm(xf, final_norm, F32, "final_norm")
    return out.reshape(BATCH, SEQ, D_MODEL)
```

```python
import math

import jax
import jax.numpy as jnp
from jax import lax
from jax.experimental import pallas as pl
from jax.experimental.pallas import tpu as pltpu

F32 = jnp.float32
BF16 = jnp.bfloat16

D_MODEL = 2048
BATCH = 4
SEQ = 2048
DEPTH = 4
TOKENS = BATCH * SEQ
EPS = 1e-6

SSM_D_INNER = 4096
SSM_HEAD_DIM = 64
SSM_HEADS = 64
SSM_GROUPS = 8
SSM_HEADS_PER_GROUP = SSM_HEADS // SSM_GROUPS
SSM_GROUP_WIDTH = SSM_D_INNER // SSM_GROUPS
SSM_STATE = 128
SSM_CONV = 4
SSM_CHUNK = 128
SSM_CONV_DIM = SSM_D_INNER + 2 * SSM_GROUPS * SSM_STATE

ATTN_HEADS = 16
ATTN_HEAD_DIM = 128
ATTN_KV_HEADS = 4
ATTN_WIDTH = ATTN_HEADS * ATTN_HEAD_DIM
ATTN_KV_WIDTH = ATTN_KV_HEADS * ATTN_HEAD_DIM
IDX_HEADS = 16
IDX_HEAD_DIM = 64
IDX_WIDTH = IDX_HEADS * IDX_HEAD_DIM
TOP_K = min(256, SEQ // 4)
Q_BLOCK = 128
ROPE_THETA = 500000.0
ATTN_ROT_DIM = ATTN_HEAD_DIM // 4
IDX_ROT_DIM = IDX_HEAD_DIM // 4
FFN_HIDDEN = 5632

COL_Z = 0
COL_XS = COL_Z + SSM_D_INNER
COL_B = COL_XS + SSM_D_INNER
COL_C = COL_B + SSM_GROUPS * SSM_STATE
PROJ_A_COLS = COL_C + SSM_GROUPS * SSM_STATE
COL_Q = 0
COL_K = COL_Q + ATTN_WIDTH
COL_V = COL_K + ATTN_KV_WIDTH
COL_QI = COL_V + ATTN_KV_WIDTH
COL_GA = COL_QI + IDX_WIDTH
COL_GB = COL_GA + D_MODEL
PROJ_B_COLS = COL_GB + D_MODEL
SMALL_COLS = 256
SMALL_DT = 128
SMALL_W = SMALL_DT + SSM_HEADS

LANES = 128
SUBLANES = 8
VMEM_LIMIT_BYTES = 56 * 1024 * 1024

INT_MIN = -(2 ** 31)


def _params(*sem):
    return pltpu.CompilerParams(dimension_semantics=sem, vmem_limit_bytes=VMEM_LIMIT_BYTES)


def _dot(a, b):
    return jnp.dot(a, b, preferred_element_type=F32)


def _sigmoid(x):
    return 1.0 / (1.0 + jnp.exp2(x * (-math.log2(math.e))))


def _rmsnorm_kernel(x_ref, g_ref, o_ref):
    x = x_ref[...]
    ms = jnp.mean(x * x, axis=-1, keepdims=True)
    o_ref[...] = ((x * lax.rsqrt(ms + EPS)) * g_ref[...]).astype(o_ref.dtype)


def rmsnorm(x, g, out_dtype, name, tm=512):
    m, d = x.shape
    return pl.pallas_call(
        _rmsnorm_kernel,
        grid=(m // tm,),
        in_specs=[pl.BlockSpec((tm, d), lambda i: (i, 0)),
                  pl.BlockSpec((1, d), lambda i: (0, 0))],
        out_specs=pl.BlockSpec((tm, d), lambda i: (i, 0)),
        out_shape=jax.ShapeDtypeStruct((m, d), out_dtype),
        compiler_params=_params("parallel"),
        name=name,
    )(x, g.reshape(1, d))


def _cast_on_first_m_step(pairs):
    @pl.when(pl.program_id(1) == 0)
    def _():
        for w_ref, wb_ref in pairs:
            wb_ref[...] = w_ref[...].astype(BF16)


def _weight_spec(k, tn, layer):
    return pl.BlockSpec((None, k, tn), lambda j, i: (layer, 0, j))


def _mm_kernel(a_ref, w_ref, o_ref):
    o_ref[...] = _dot(a_ref[...], w_ref[...]).astype(o_ref.dtype)


def matmul_bf16w(a, w, tm, tn, out_dtype, name):
    m, k = a.shape
    n = w.shape[1]
    return pl.pallas_call(
        _mm_kernel,
        grid=(n // tn, m // tm),
        in_specs=[pl.BlockSpec((tm, k), lambda j, i: (i, 0)),
                  pl.BlockSpec((k, tn), lambda j, i: (0, j))],
        out_specs=pl.BlockSpec((tm, tn), lambda j, i: (i, j)),
        out_shape=jax.ShapeDtypeStruct((m, n), out_dtype),
        compiler_params=_params("parallel", "parallel"),
        name=name,
    )(a, w)


def _mm_small_kernel(a_ref, w_ref, o_ref, ot_ref):
    acc = _dot(a_ref[...], w_ref[...])
    o_ref[...] = acc
    ot_ref[...] = acc[:, SMALL_DT:].T


def small_projection(a, w, tm, name):
    m, k = a.shape
    return pl.pallas_call(
        _mm_small_kernel,
        grid=(m // tm,),
        in_specs=[pl.BlockSpec((tm, k), lambda i: (i, 0)),
                  pl.BlockSpec((k, SMALL_COLS), lambda i: (0, 0))],
        out_specs=[pl.BlockSpec((tm, SMALL_COLS), lambda i: (i, 0)),
                   pl.BlockSpec((SMALL_COLS - SMALL_DT, tm), lambda i: (0, i))],
        out_shape=[jax.ShapeDtypeStruct((m, SMALL_COLS), F32),
                   jax.ShapeDtypeStruct((SMALL_COLS - SMALL_DT, m), F32)],
        compiler_params=_params("parallel"),
        name=name,
    )(a, w)


def _mm_residual_kernel(a_ref, w_ref, r_ref, o_ref, wb_ref):
    _cast_on_first_m_step([(w_ref, wb_ref)])
    o_ref[...] = r_ref[...] + _dot(a_ref[...], wb_ref[...])


def matmul_residual(a, w_stack, layer, r, tm, tn, name):
    m, k = a.shape
    n = w_stack.shape[2]
    return pl.pallas_call(
        _mm_residual_kernel,
        grid=(n // tn, m // tm),
        in_specs=[pl.BlockSpec((tm, k), lambda j, i: (i, 0)),
                  _weight_spec(k, tn, layer),
                  pl.BlockSpec((tm, tn), lambda j, i: (i, j))],
        out_specs=pl.BlockSpec((tm, tn), lambda j, i: (i, j)),
        out_shape=jax.ShapeDtypeStruct((m, n), F32),
        scratch_shapes=[pltpu.VMEM((k, tn), BF16)],
        compiler_params=_params("parallel", "arbitrary"),
        name=name,
    )(a, w_stack, r)


def _ffn_up_kernel(a_ref, wg_ref, wu_ref, o_ref, wgb_ref, wub_ref):
    _cast_on_first_m_step([(wg_ref, wgb_ref), (wu_ref, wub_ref)])
    a = a_ref[...]
    g = _dot(a, wgb_ref[...])
    u = _dot(a, wub_ref[...])
    o_ref[...] = ((g * _sigmoid(g)) * u).astype(o_ref.dtype)


def ffn_up(a, wg_stack, wu_stack, layer, tm, tn, name):
    m, k = a.shape
    n = wg_stack.shape[2]
    return pl.pallas_call(
        _ffn_up_kernel,
        grid=(n // tn, m // tm),
        in_specs=[pl.BlockSpec((tm, k), lambda j, i: (i, 0)),
                  _weight_spec(k, tn, layer),
                  _weight_spec(k, tn, layer)],
        out_specs=pl.BlockSpec((tm, tn), lambda j, i: (i, j)),
        out_shape=jax.ShapeDtypeStruct((m, n), BF16),
        scratch_shapes=[pltpu.VMEM((k, tn), BF16), pltpu.VMEM((k, tn), BF16)],
        compiler_params=_params("parallel", "arbitrary"),
        name=name,
    )(a, wg_stack, wu_stack)


def _merge_kernel(pa_ref, yb_ref, wb_ref, ga_ref, gb_ref, o_ref, wbb_ref):
    _cast_on_first_m_step([(wb_ref, wbb_ref)])
    pb = _dot(yb_ref[...], wbb_ref[...])
    o_ref[...] = (_sigmoid(ga_ref[...]) * pa_ref[...]
                  + _sigmoid(gb_ref[...]) * pb).astype(o_ref.dtype)


def merge_branches(pa, yb, wb_stack, layer, proj, tm, tn, name):
    m = yb.shape[0]
    kb = wb_stack.shape[1]
    n = wb_stack.shape[2]
    ga0 = COL_GA // tn
    gb0 = COL_GB // tn
    return pl.pallas_call(
        _merge_kernel,
        grid=(n // tn, m // tm),
        in_specs=[pl.BlockSpec((tm, tn), lambda j, i: (i, j)),
                  pl.BlockSpec((tm, kb), lambda j, i: (i, 0)),
                  _weight_spec(kb, tn, layer),
                  pl.BlockSpec((tm, tn), lambda j, i: (i, ga0 + j)),
                  pl.BlockSpec((tm, tn), lambda j, i: (i, gb0 + j))],
        out_specs=pl.BlockSpec((tm, tn), lambda j, i: (i, j)),
        out_shape=jax.ShapeDtypeStruct((m, n), BF16),
        scratch_shapes=[pltpu.VMEM((kb, tn), BF16)],
        compiler_params=_params("parallel", "arbitrary"),
        name=name,
    )(pa, yb, wb_stack, proj, proj)


SSD_STEP_CHUNKS = 4
SSD_STEP_ROWS = SSD_STEP_CHUNKS * SSM_CHUNK
SSD_CONV_WIDTH = SSM_GROUP_WIDTH + 2 * SSM_STATE


def _softplus(x):
    return jnp.maximum(x, 0.0) + jnp.log1p(jnp.exp(-jnp.abs(x)))


def _split3(x):
    hi = x.astype(BF16)
    r1 = x - hi.astype(F32)
    mid = r1.astype(BF16)
    lo = (r1 - mid.astype(F32)).astype(BF16)
    return hi, mid, lo


def _ssd_kernel(x_ref, xh_ref, b_ref, bh_ref, c_ref, ch_ref, cw_ref, cbias_ref, z_ref,
                dtr_ref, biasc_ref, alogc_ref, dsk_ref, ng_ref,
                u_ref, wb_ref, o_ref, pb_ref, st_ref, pad_ref, xc_ref, y_ref):
    def shadow_slice(k):
        rows_k = slice(k * SHADOW_SLICE_ROWS, (k + 1) * SHADOW_SLICE_ROWS)
        pb_ref[rows_k, :] = _dot(u_ref[rows_k, :], wb_ref[...])

    step = pl.program_id(2)

    @pl.when(step == 0)
    def _():
        st_ref[...] = jnp.zeros_like(st_ref)

    gw = SSM_GROUP_WIDTH
    n = SSM_STATE
    lc = SSM_CHUNK
    hp = SSM_HEAD_DIM
    halo = SUBLANES

    def halo_rows(h_ref):
        first = jnp.full(h_ref.shape, step, jnp.int32) == 0
        return jnp.where(first, 0.0, h_ref[...])

    pad_ref[0:halo, 0:gw] = halo_rows(xh_ref)
    pad_ref[0:halo, gw:gw + n] = halo_rows(bh_ref)
    pad_ref[0:halo, gw + n:] = halo_rows(ch_ref)
    pad_ref[halo:, 0:gw] = x_ref[...]
    pad_ref[halo:, gw:gw + n] = b_ref[...]
    pad_ref[halo:, gw + n:] = c_ref[...]
    cw = cw_ref[...]
    cbias = cbias_ref[...]
    for r0 in range(0, SSD_STEP_ROWS, lc):
        acc = cbias + cw[SSM_CONV - 1:SSM_CONV, :] * pad_ref[halo + r0:halo + r0 + lc, :]
        for back in range(1, SSM_CONV):
            lo = halo + r0 - back
            acc = acc + cw[SSM_CONV - 1 - back:SSM_CONV - back, :] * pad_ref[lo:lo + lc, :]
        xc_ref[r0:r0 + lc, :] = acc * _sigmoid(acc)

    log2e = math.log2(math.e)
    dtr_all = _softplus(dtr_ref[...] + biasc_ref[...])
    adt_r_all = dtr_all * (-log2e * jnp.exp(alogc_ref[...]))

    ii = lax.broadcasted_iota(jnp.int32, (lc, lc), 0)
    jj = lax.broadcasted_iota(jnp.int32, (lc, lc), 1)
    tri_t = jnp.where(ii <= jj, 1.0, 0.0).astype(BF16)
    pad_heads = jnp.zeros((lc - SSM_HEADS_PER_GROUP, lc), F32)

    for cc in range(SSD_STEP_CHUNKS):
        r0 = cc * lc
        cs_r = sum(_dot(p, tri_t) for p in _split3(adt_r_all[:, r0:r0 + lc]))
        cs_c = jnp.concatenate([cs_r, pad_heads], axis=0).T
        ecs_r = jnp.exp2(cs_r)
        ecl = ecs_r[:, lc - 1:lc]
        dtr = dtr_all[:, r0:r0 + lc]
        ddr = dtr * jnp.exp2(cs_r[:, lc - 1:lc] - cs_r)

        x_t = xc_ref[r0:r0 + lc, 0:gw].T
        b16 = xc_ref[r0:r0 + lc, gw:gw + n].astype(BF16)
        cc32 = xc_ref[r0:r0 + lc, gw + n:]
        c_t = cc32.T
        cb_t = lax.dot_general(b16, cc32.astype(BF16), (((1,), (1,)), ((), ())),
                               preferred_element_type=F32)
        for h in range(SSM_HEADS_PER_GROUP):
            if h % (SSM_HEADS_PER_GROUP // SHADOW_SLICES_PER_CHUNK) == 0:
                shadow_slice(cc * SHADOW_SLICES_PER_CHUNK
                             + h // (SSM_HEADS_PER_GROUP // SHADOW_SLICES_PER_CHUNK))
            lo = h * hp
            cs_s = jnp.broadcast_to(cs_c[:, h:h + 1], (lc, lc))
            decay_t = jnp.exp2(jnp.where(ii <= jj, cs_r[h:h + 1, :] - cs_s, -jnp.inf))
            intra_t = cb_t * decay_t
            inter_t = c_t * ecs_r[h:h + 1, :]
            prev = st_ref[h]
            x_h = x_t[lo:lo + hp, :]
            y_t = _dot(
                jnp.concatenate([x_h * dtr[h:h + 1, :], prev], axis=1).astype(BF16),
                jnp.concatenate([intra_t, inter_t], axis=0).astype(BF16))
            if h % 2 == 0:
                y_even = y_t
            else:
                y_ref[r0:r0 + lc, lo - hp:lo + hp] = jnp.concatenate([y_even, y_t], axis=0).T
            st_new = _dot((x_h * ddr[h:h + 1, :]).astype(BF16), b16)
            st_ref[h] = prev * ecl[h:h + 1, :] + st_new

    y = y_ref[...] + dsk_ref[...] * xc_ref[:, 0:gw]
    z = z_ref[...]
    y = y * (z * _sigmoid(z))
    ms = jnp.mean(y * y, axis=-1, keepdims=True)
    o_ref[...] = ((y * lax.rsqrt(ms + EPS)) * ng_ref[...]).astype(o_ref.dtype)


SHADOW_TM = 1024
SHADOW_SLICES_PER_CHUNK = 1
SHADOW_SLICE_ROWS = SHADOW_TM // (SSD_STEP_CHUNKS * SHADOW_SLICES_PER_CHUNK)
SHADOW_TN = PROJ_B_COLS * TOKENS // SHADOW_TM // (BATCH * SSM_GROUPS * (SEQ // SSD_STEP_ROWS))


def ssd_branch(proj, dt_row, conv_w, conv_b, dt_bias, a_log, d_skip, norm_g, u, w_b, name):
    rows = SSD_STEP_ROWS
    m_tiles = TOKENS // SHADOW_TM
    tile_i = lambda b, g, s: ((b * SSM_GROUPS + g) * (SEQ // rows) + s) % m_tiles
    tile_j = lambda b, g, s: ((b * SSM_GROUPS + g) * (SEQ // rows) + s) // m_tiles
    steps = SEQ // rows
    gw = SSM_GROUP_WIDTH
    n = SSM_STATE
    hg = SSM_HEADS_PER_GROUP
    ng = SSM_GROUPS
    cwid = SSD_CONV_WIDTH

    def per_group(p, lead):
        xs = p[:, :SSM_D_INNER].reshape(lead, ng, gw)
        bs = p[:, SSM_D_INNER:SSM_D_INNER + ng * n].reshape(lead, ng, n)
        cs = p[:, SSM_D_INNER + ng * n:].reshape(lead, ng, n)
        return jnp.transpose(jnp.concatenate([xs, bs, cs], axis=2), (1, 0, 2))

    row = lambda b, g, s: b * steps + s
    hrow = lambda b, g, s: jnp.maximum(row(b, g, s) * (rows // SUBLANES) - 1, 0)
    return pl.pallas_call(
        _ssd_kernel,
        grid=(BATCH, ng, steps),
        in_specs=[
            pl.BlockSpec((rows, gw), lambda b, g, s: (row(b, g, s), COL_XS // gw + g)),
            pl.BlockSpec((SUBLANES, gw), lambda b, g, s: (hrow(b, g, s), COL_XS // gw + g)),
            pl.BlockSpec((rows, n), lambda b, g, s: (row(b, g, s), COL_B // n + g)),
            pl.BlockSpec((SUBLANES, n), lambda b, g, s: (hrow(b, g, s), COL_B // n + g)),
            pl.BlockSpec((rows, n), lambda b, g, s: (row(b, g, s), COL_C // n + g)),
            pl.BlockSpec((SUBLANES, n), lambda b, g, s: (hrow(b, g, s), COL_C // n + g)),
            pl.BlockSpec((None, SSM_CONV, cwid), lambda b, g, s: (g, 0, 0)),
            pl.BlockSpec((None, 1, cwid), lambda b, g, s: (g, 0, 0)),
            pl.BlockSpec((rows, gw), lambda b, g, s: (row(b, g, s), COL_Z // gw + g)),
            pl.BlockSpec((hg, rows), lambda b, g, s: (g, row(b, g, s))),
            pl.BlockSpec((None, hg, 1), lambda b, g, s: (g, 0, 0)),
            pl.BlockSpec((None, hg, 1), lambda b, g, s: (g, 0, 0)),
            pl.BlockSpec((1, gw), lambda b, g, s: (0, g)),
            pl.BlockSpec((1, gw), lambda b, g, s: (0, g)),
            pl.BlockSpec((SHADOW_TM, D_MODEL), lambda b, g, s: (tile_i(b, g, s), 0)),
            pl.BlockSpec((D_MODEL, SHADOW_TN), lambda b, g, s: (0, tile_j(b, g, s))),
        ],
        out_specs=[pl.BlockSpec((rows, gw), lambda b, g, s: (row(b, g, s), g)),
                   pl.BlockSpec((SHADOW_TM, SHADOW_TN),
                                lambda b, g, s: (tile_i(b, g, s), tile_j(b, g, s)))],
        out_shape=[jax.ShapeDtypeStruct((TOKENS, SSM_D_INNER), BF16),
                   jax.ShapeDtypeStruct((TOKENS, PROJ_B_COLS), F32)],
        scratch_shapes=[pltpu.VMEM((hg, SSM_HEAD_DIM, n), F32),
                        pltpu.VMEM((rows + SUBLANES, cwid), F32),
                        pltpu.VMEM((rows, cwid), F32),
                        pltpu.VMEM((rows, gw), F32)],
        compiler_params=_params("parallel", "parallel", "arbitrary"),
        name=name,
    )(proj, proj, proj, proj, proj, proj,
      per_group(conv_w, SSM_CONV), per_group(conv_b.reshape(1, SSM_CONV_DIM), 1),
      proj, dt_row, dt_bias.reshape(ng, hg, 1), a_log.reshape(ng, hg, 1),
      jnp.repeat(d_skip, SSM_HEAD_DIM).reshape(1, SSM_D_INNER),
      norm_g.reshape(1, SSM_D_INNER), u, w_b)


KEY_CHUNK = 256
PREP_ROWS = KEY_CHUNK
PREP_QBLOCKS = PREP_ROWS // Q_BLOCK
N_KEY_CHUNKS = SEQ // KEY_CHUNK
HEAD_PAIRS = ATTN_HEADS // 2
IDX_PAIRS = IDX_HEADS // 2
Q_PRESCALE = ATTN_HEAD_DIM ** -0.5 * math.log2(math.e)
W_IDX_SCALE = IDX_HEADS ** -0.5 * IDX_HEAD_DIM ** -0.5
M_INIT = -1e30
PA_TILE = 512
PA_SLICE = 256
RADIX_CHUNK_COUNTS = (4, 6, 8)
assert RADIX_CHUNK_COUNTS[-1] == N_KEY_CHUNKS


def _rope(x, c, sn, sp, half):
    return x * c + pltpu.roll(x, LANES - half, 1) * sn + pltpu.roll(x, half, 1) * sp


def _prep_kernel(q_ref, k_ref, v_ref, qi_ref, smk_ref, smw_ref, ca_ref, sna_ref, spa_ref,
                 ci_ref, sni_ref, spi_ref, gk_ref,
                 qt_ref, ko_ref, vt_ref, qit_ref, kio_ref, wt_ref):
    ca, sna, spa = ca_ref[...], sna_ref[...], spa_ref[...]
    ci, sni, spi = ci_ref[...], sni_ref[...], spi_ref[...]
    ha = ATTN_ROT_DIM // 2
    hi = IDX_ROT_DIM // 2
    hd = ATTN_HEAD_DIM
    zeros_half = jnp.zeros((LANES - IDX_HEAD_DIM, Q_BLOCK), F32)
    for qq in range(PREP_QBLOCKS):
        rows = slice(qq * Q_BLOCK, (qq + 1) * Q_BLOCK)
        for h in range(ATTN_HEADS):
            qr = _rope(q_ref[rows, h * hd:(h + 1) * hd], ca[rows], sna[rows], spa[rows], ha)
            qt_ref[qq, h // 2, :, (h % 2) * Q_BLOCK:(h % 2 + 1) * Q_BLOCK] = (
                (qr * Q_PRESCALE).T.astype(BF16))
        for j in range(IDX_PAIRS):
            xt = _rope(qi_ref[rows, j * LANES:(j + 1) * LANES],
                       ci[rows], sni[rows], spi[rows], hi).T
            even = jnp.concatenate([xt[:IDX_HEAD_DIM], zeros_half], axis=0)
            odd = jnp.concatenate([xt[IDX_HEAD_DIM:], zeros_half], axis=0)
            qit_ref[qq, j, :, 0:Q_BLOCK] = even.astype(BF16)
            qit_ref[qq, j, :, Q_BLOCK:] = odd.astype(BF16)
    for g in range(ATTN_KV_HEADS):
        cols = slice(g * hd, (g + 1) * hd)
        ko_ref[:, cols] = _rope(k_ref[:, cols], ca, sna, spa, ha).astype(BF16)
        vt_ref[g, 0] = v_ref[:, cols].T.astype(BF16)
    xk = smk_ref[...]
    ms = jnp.sum(xk * xk, axis=-1, keepdims=True) * (1.0 / IDX_HEAD_DIM)
    xk = (xk * lax.rsqrt(ms + EPS)) * gk_ref[...]
    kio_ref[...] = _rope(xk, ci, sni, spi, hi).astype(BF16)
    smt = smw_ref[...].T
    w0 = SMALL_W - SMALL_DT
    for qq in range(PREP_QBLOCKS):
        wt_ref[qq] = smt[w0:w0 + IDX_HEADS, qq * Q_BLOCK:(qq + 1) * Q_BLOCK] * W_IDX_SCALE


def dsa_prep(proj, small, tabs, idx_k_norm, name):
    r = PREP_ROWS
    nqb = PREP_QBLOCKS
    gk = jnp.concatenate([idx_k_norm, jnp.zeros((LANES - IDX_HEAD_DIM,), F32)]).reshape(1, LANES)
    tab_spec = pl.BlockSpec((r, LANES), lambda i: (i, 0))
    return pl.pallas_call(
        _prep_kernel,
        grid=(TOKENS // r,),
        in_specs=[
            pl.BlockSpec((r, ATTN_WIDTH), lambda i: (i, COL_Q // ATTN_WIDTH)),
            pl.BlockSpec((r, ATTN_KV_WIDTH), lambda i: (i, COL_K // ATTN_KV_WIDTH)),
            pl.BlockSpec((r, ATTN_KV_WIDTH), lambda i: (i, COL_V // ATTN_KV_WIDTH)),
            pl.BlockSpec((r, IDX_WIDTH), lambda i: (i, COL_QI // IDX_WIDTH)),
            pl.BlockSpec((r, LANES), lambda i: (i, 0)),
            pl.BlockSpec((r, LANES), lambda i: (i, SMALL_DT // LANES)),
            tab_spec, tab_spec, tab_spec, tab_spec, tab_spec, tab_spec,
            pl.BlockSpec((1, LANES), lambda i: (0, 0)),
        ],
        out_specs=[
            pl.BlockSpec((nqb, HEAD_PAIRS, ATTN_HEAD_DIM, 2 * Q_BLOCK), lambda i: (i, 0, 0, 0)),
            pl.BlockSpec((r, ATTN_KV_WIDTH), lambda i: (i, 0)),
            pl.BlockSpec((ATTN_KV_HEADS, 1, ATTN_HEAD_DIM, r), lambda i: (0, i, 0, 0)),
            pl.BlockSpec((nqb, IDX_PAIRS, LANES, 2 * Q_BLOCK), lambda i: (i, 0, 0, 0)),
            pl.BlockSpec((r, LANES), lambda i: (i, 0)),
            pl.BlockSpec((nqb, IDX_HEADS, Q_BLOCK), lambda i: (i, 0, 0)),
        ],
        out_shape=[
            jax.ShapeDtypeStruct((TOKENS // Q_BLOCK, HEAD_PAIRS, ATTN_HEAD_DIM, 2 * Q_BLOCK), BF16),
            jax.ShapeDtypeStruct((TOKENS, ATTN_KV_WIDTH), BF16),
            jax.ShapeDtypeStruct((ATTN_KV_HEADS, TOKENS // r, ATTN_HEAD_DIM, r), BF16),
            jax.ShapeDtypeStruct((TOKENS // Q_BLOCK, IDX_PAIRS, LANES, 2 * Q_BLOCK), BF16),
            jax.ShapeDtypeStruct((TOKENS, LANES), BF16),
            jax.ShapeDtypeStruct((TOKENS // Q_BLOCK, IDX_HEADS, Q_BLOCK), F32),
        ],
        compiler_params=_params("parallel"),
        name=name,
    )(proj, proj, proj, proj, small, small, *tabs, gk)


def _col_reduce8(x, op):
    rows, lanes = x.shape
    g = x.reshape(4, rows // (4 * SUBLANES), SUBLANES, lanes)
    p = op(g, axis=1)
    red = jnp.maximum if op is jnp.max else jnp.add
    return red(red(p[0], p[1]), red(p[2], p[3]))


def _dsa_kernel(qt_ref, k_ref, vt_ref, qit_ref, ki_ref, wt_ref, ya_ref, wpa_ref, o_ref, pa_ref,
                key_ref, madd_ref, acc_ref, m_ref, l_ref, alpha_ref, s_ref, pe_ref, thr_ref,
                wpa16_ref):
    tq = Q_BLOCK
    kc = KEY_CHUNK
    hd = ATTN_HEAD_DIM
    qb = pl.program_id(1)
    n_chunks = qb // (kc // tq) + 1
    neg_inf = -jnp.inf
    q_pos = qb * tq + lax.broadcasted_iota(jnp.int32, (kc, tq), 1)
    k_off = lax.broadcasted_iota(jnp.int32, (kc, tq), 0)

    def causal(c):
        return c * kc + k_off <= q_pos

    def key_rows(c):
        return pl.ds(pl.multiple_of(c * kc, kc), kc)

    wt = wt_ref[...]

    def score_chunk(c, carry):
        ki = ki_ref[key_rows(c), :]
        score = jnp.zeros((kc, tq), F32)
        for j in range(IDX_PAIRS):
            lt = _dot(ki, qit_ref[j])
            score = score + jnp.maximum(lt[:, :tq], 0.0) * wt[2 * j:2 * j + 1, :]
            score = score + jnp.maximum(lt[:, tq:], 0.0) * wt[2 * j + 1:2 * j + 2, :]
        score = jnp.where(score == 0.0, 0.0, score)
        bits = lax.bitcast_convert_type(jnp.where(causal(c), score, neg_inf), jnp.int32)
        key_ref[c] = jnp.where(bits >= 0, bits, bits ^ jnp.int32(0x7FFFFFFF))
        return carry

    lax.fori_loop(0, n_chunks, score_chunk, 0)

    pa_parts = PA_TILE // PA_SLICE

    def pa_slice(k):
        rows_k = slice((k % pa_parts) * PA_SLICE, (k % pa_parts + 1) * PA_SLICE)
        cols_k = slice((k // pa_parts) * PA_SLICE, (k // pa_parts + 1) * PA_SLICE)
        pa_ref[rows_k, cols_k] = _dot(ya_ref[rows_k, :], wpa16_ref[:, cols_k])

    @pl.when(qb == 0)
    def _():
        wpa16_ref[...] = wpa_ref[...].astype(BF16)

    def radix_select(chunks_used):
        def pad_chunk(c, carry):
            key_ref[c] = jnp.full((kc, tq), INT_MIN, jnp.int32)
            return carry

        lax.fori_loop(n_chunks, chunks_used, pad_chunk, 0)
        t = jnp.zeros((1, tq), jnp.int32)
        for i in range(32):
            cand = t | lax.shift_left(jnp.int32(1), 31 - i)
            cand_key = cand ^ jnp.int32(INT_MIN)
            cnt8 = jnp.zeros((SUBLANES, tq), F32)
            for c in range(chunks_used):
                hit = jnp.where(key_ref[c] >= cand_key, 1.0, 0.0)
                cnt8 = cnt8 + _col_reduce8(hit, jnp.sum)
            cnt = jnp.sum(cnt8, axis=0, keepdims=True)
            t = jnp.where(cnt >= float(TOP_K), cand, t)
            if i % (32 // pa_parts ** 2) == 0:
                pa_slice(i // (32 // pa_parts ** 2))
        thr_ref[...] = t ^ jnp.int32(INT_MIN)

    lo_chunks = 0
    for hi_chunks in RADIX_CHUNK_COUNTS:
        @pl.when((n_chunks > lo_chunks) & (n_chunks <= hi_chunks))
        def _(hi_chunks=hi_chunks):
            radix_select(hi_chunks)

        lo_chunks = hi_chunks

    thr = thr_ref[...]

    def mask_chunk(c, cnt):
        ge = key_ref[c] >= thr
        ok = causal(c)
        madd_ref[c] = jnp.where(ok, jnp.where(ge, 0.0, neg_inf), neg_inf)
        return cnt + jnp.sum(jnp.where(ok, jnp.where(ge, 1.0, 0.0), 0.0), axis=0, keepdims=True)

    cnt_ge = lax.fori_loop(0, n_chunks, mask_chunk, jnp.zeros((1, tq), F32))

    @pl.when(jnp.max(cnt_ge) > float(TOP_K))
    def _():
        def count_gt(c, cnt):
            return cnt + jnp.sum(jnp.where(key_ref[c] > thr, 1.0, 0.0), axis=0, keepdims=True)

        need = float(TOP_K) - lax.fori_loop(0, n_chunks, count_gt, jnp.zeros((1, tq), F32))
        a_i = lax.broadcasted_iota(jnp.int32, (kc, kc), 0)
        a_j = lax.broadcasted_iota(jnp.int32, (kc, kc), 1)
        earlier = jnp.where(a_j < a_i, 1.0, 0.0).astype(BF16)

        def tie_chunk(c, run):
            key = key_ref[c]
            gt = key > thr
            eq = key == thr
            e = jnp.where(eq, 1.0, 0.0)
            rank = _dot(earlier, e.astype(BF16)) + run
            keep_eq = jnp.where(rank < need, 0.0, neg_inf)
            val = jnp.where(gt, 0.0, jnp.where(eq, keep_eq, neg_inf))
            madd_ref[c] = jnp.where(causal(c), val, neg_inf)
            return run + jnp.sum(e, axis=0, keepdims=True)

        lax.fori_loop(0, n_chunks, tie_chunk, jnp.zeros((1, tq), F32))

    m_ref[...] = jnp.full(m_ref.shape, M_INIT, F32)
    l_ref[...] = jnp.zeros(l_ref.shape, F32)
    acc_ref[...] = jnp.zeros(acc_ref.shape, F32)

    def attn_chunk(c, carry):
        half_rows = kc // 2
        for p in range(HEAD_PAIRS):
            g = p // 2
            top8 = None
            for rh in range(2):
                r0 = pl.multiple_of(c * kc + rh * half_rows, half_rows)
                sub = slice(rh * half_rows, (rh + 1) * half_rows)
                s = _dot(k_ref[pl.ds(r0, half_rows), g * hd:(g + 1) * hd], qt_ref[p])
                madd = madd_ref[c, sub, :]
                s = s + jnp.concatenate([madd, madd], axis=1)
                s_ref[p, sub, :] = s
                t8 = _col_reduce8(s, jnp.max)
                top8 = t8 if top8 is None else jnp.maximum(top8, t8)
            m_old = m_ref[p]
            m_new = jnp.maximum(m_old, jnp.max(top8, axis=0, keepdims=True))
            m_ref[p] = m_new
            alpha_ref[p] = jnp.exp2(m_old - m_new)
        for p in range(HEAD_PAIRS):
            sum8 = None
            for rh in range(2):
                sub = slice(rh * half_rows, (rh + 1) * half_rows)
                pe = jnp.exp2(s_ref[p, sub, :] - m_ref[p])
                pe_ref[p, sub, :] = pe.astype(BF16)
                s8 = _col_reduce8(pe, jnp.sum)
                sum8 = s8 if sum8 is None else sum8 + s8
            l_ref[p] = l_ref[p] * alpha_ref[p] + jnp.sum(sum8, axis=0, keepdims=True)
        for p in range(HEAD_PAIRS):
            acc_ref[p] = acc_ref[p] * alpha_ref[p] + _dot(vt_ref[p // 2, c], pe_ref[p])
        return carry

    lax.fori_loop(0, n_chunks, attn_chunk, 0)

    for p in range(HEAD_PAIRS):
        o = acc_ref[p] / l_ref[p]
        for half in range(2):
            h = 2 * p + half
            o_ref[:, h * hd:(h + 1) * hd] = o[:, half * tq:(half + 1) * tq].T.astype(o_ref.dtype)


def dsa_attention(q_t, k_r, v_t, qi_t, ki, w_t, y_a, w_pa_stack, layer, name):
    nq = SEQ // Q_BLOCK
    row = lambda b, i: b * nq + i
    pa_m = TOKENS // PA_TILE
    assert pa_m == nq and D_MODEL // PA_TILE == BATCH
    return pl.pallas_call(
        _dsa_kernel,
        grid=(BATCH, nq),
        in_specs=[
            pl.BlockSpec((None, HEAD_PAIRS, ATTN_HEAD_DIM, 2 * Q_BLOCK),
                         lambda b, i: (row(b, i), 0, 0, 0)),
            pl.BlockSpec((SEQ, ATTN_KV_WIDTH), lambda b, i: (b, 0)),
            pl.BlockSpec((ATTN_KV_HEADS, N_KEY_CHUNKS, ATTN_HEAD_DIM, KEY_CHUNK),
                         lambda b, i: (0, b, 0, 0)),
            pl.BlockSpec((None, IDX_PAIRS, LANES, 2 * Q_BLOCK), lambda b, i: (row(b, i), 0, 0, 0)),
            pl.BlockSpec((SEQ, LANES), lambda b, i: (b, 0)),
            pl.BlockSpec((None, IDX_HEADS, Q_BLOCK), lambda b, i: (row(b, i), 0, 0)),
            pl.BlockSpec((PA_TILE, SSM_D_INNER), lambda b, i: (row(b, i) % pa_m, 0)),
            pl.BlockSpec((None, SSM_D_INNER, PA_TILE), lambda b, i: (layer, 0, b)),
        ],
        out_specs=[pl.BlockSpec((Q_BLOCK, ATTN_WIDTH), lambda b, i: (row(b, i), 0)),
                   pl.BlockSpec((PA_TILE, PA_TILE),
                                lambda b, i: (row(b, i) % pa_m, row(b, i) // pa_m))],
        out_shape=[jax.ShapeDtypeStruct((TOKENS, ATTN_WIDTH), BF16),
                   jax.ShapeDtypeStruct((TOKENS, D_MODEL), F32)],
        scratch_shapes=[pltpu.VMEM((N_KEY_CHUNKS, KEY_CHUNK, Q_BLOCK), jnp.int32),
                        pltpu.VMEM((N_KEY_CHUNKS, KEY_CHUNK, Q_BLOCK), F32),
                        pltpu.VMEM((HEAD_PAIRS, ATTN_HEAD_DIM, 2 * Q_BLOCK), F32),
                        pltpu.VMEM((HEAD_PAIRS, 1, 2 * Q_BLOCK), F32),
                        pltpu.VMEM((HEAD_PAIRS, 1, 2 * Q_BLOCK), F32),
                        pltpu.VMEM((HEAD_PAIRS, 1, 2 * Q_BLOCK), F32),
                        pltpu.VMEM((HEAD_PAIRS, KEY_CHUNK, 2 * Q_BLOCK), F32),
                        pltpu.VMEM((HEAD_PAIRS, KEY_CHUNK, 2 * Q_BLOCK), BF16),
                        pltpu.VMEM((1, Q_BLOCK), jnp.int32),
                        pltpu.VMEM((SSM_D_INNER, PA_TILE), BF16)],
        compiler_params=_params("parallel", "arbitrary"),
        name=name,
    )(q_t, k_r, v_t, qi_t, ki, w_t, y_a, w_pa_stack)


def _rope_lane_tables(positions):
    pos = positions.astype(F32).reshape(TOKENS, 1)

    def tables(rot_dim, width):
        inv_freq = ROPE_THETA ** (-(jnp.arange(0, rot_dim, 2, dtype=F32) / rot_dim))
        ang = pos * inv_freq
        cos, sin = jnp.cos(ang), jnp.sin(ang)
        half = rot_dim // 2
        ones = jnp.ones((TOKENS, width - rot_dim), F32)
        zeros = lambda n: jnp.zeros((TOKENS, n), F32)
        c = jnp.concatenate([cos, cos, ones], axis=1)
        sn = jnp.concatenate([-sin, zeros(width - half)], axis=1)
        sp = jnp.concatenate([zeros(half), sin, zeros(width - rot_dim)], axis=1)
        reps = LANES // width
        return tuple(jnp.tile(t, (1, reps)) for t in (c, sn, sp))

    return tables(ATTN_ROT_DIM, ATTN_HEAD_DIM) + tables(IDX_ROT_DIM, IDX_HEAD_DIM)


def _regroup_w_in(w):
    sizes = (SSM_HEADS, ATTN_WIDTH, ATTN_KV_WIDTH, ATTN_KV_WIDTH, IDX_WIDTH, IDX_HEAD_DIM,
             IDX_HEADS, D_MODEL, D_MODEL)
    parts = []
    off = PROJ_A_COLS
    for s in sizes:
        parts.append(w[:, off:off + s])
        off += s
    _, q, k, v, qi, _, _, ga, gb = parts
    dt = w[:, PROJ_A_COLS:PROJ_A_COLS + LANES][:, :SSM_HEADS]
    ki0 = PROJ_A_COLS + SSM_HEADS + ATTN_WIDTH + 2 * ATTN_KV_WIDTH + IDX_WIDTH
    win0 = ki0 // LANES * LANES
    win = w[:, win0:win0 + 2 * LANES]
    ki = win[:, ki0 - win0:ki0 - win0 + IDX_HEAD_DIM]
    wi = win[:, ki0 - win0 + IDX_HEAD_DIM:ki0 - win0 + IDX_HEAD_DIM + IDX_HEADS]
    w_a = w[:, :PROJ_A_COLS].astype(BF16)
    w_b = jnp.concatenate([q, k, v, qi, ga, gb], axis=1).astype(BF16)
    zc = lambda n: jnp.zeros((D_MODEL, n), F32)
    small = jnp.concatenate(
        [ki, zc(LANES - IDX_HEAD_DIM), dt, wi, zc(SMALL_COLS - SMALL_W - IDX_HEADS)],
        axis=1).astype(BF16)
    return w_a, w_b, small


def kernel(x, positions, mix_norm, w_in, conv_w, conv_b, dt_bias, a_log, d_skip, ssm_norm,
           idx_k_norm, w_proj_a, w_proj_b, w_out, ffn_norm, w_ffn_gate, w_ffn_up, w_ffn_down,
           final_norm):
    tabs = _rope_lane_tables(positions)
    xf = x.reshape(TOKENS, D_MODEL)
    for i in range(DEPTH):
        w_a, w_b, w_small = _regroup_w_in(w_in[i])
        u = rmsnorm(xf, mix_norm[i], BF16, f"mix_norm_{i}")
        proj_a = matmul_bf16w(u, w_a, 1024, 1024, F32, f"in_proj_a_{i}")
        small, small_t = small_projection(u, w_small, 1024, f"in_proj_small_{i}")

        y_a, proj_b = ssd_branch(proj_a, small_t, conv_w[i], conv_b[i], dt_bias[i], a_log[i],
                                 d_skip[i], ssm_norm[i], u, w_b, f"ssd_{i}")

        q_t, k_r, v_t, qi_t, ki, w_t = dsa_prep(proj_b, small, tabs, idx_k_norm[i],
                                                f"dsa_prep_{i}")
        y_b, pa = dsa_attention(q_t, k_r, v_t, qi_t, ki, w_t, y_a, w_proj_a, i, f"dsa_{i}")

        merged = merge_branches(pa, y_b, w_proj_b, i, proj_b, 512, 1024, f"merge_{i}")
        xf = matmul_residual(merged, w_out, i, xf, 1024, 1024, f"out_proj_{i}")

        h = rmsnorm(xf, ffn_norm[i], BF16, f"ffn_norm_{i}")
        ff = ffn_up(h, w_ffn_gate, w_ffn_up, i, 2048, 512, f"ffn_up_{i}")
        xf = matmul_residual(ff, w_ffn_down, i, xf, 512, 512, f"ffn_down_{i}")
    out = rmsnorm(xf, final_norm, F32, "final_norm")
    return out.reshape(BATCH, SEQ, D_MODEL)
```

```python
import math

import jax
import jax.numpy as jnp
from jax import lax
from jax.experimental import pallas as pl
from jax.experimental.pallas import tpu as pltpu

F32 = jnp.float32
BF16 = jnp.bfloat16

D_MODEL = 2048
BATCH = 4
SEQ = 2048
DEPTH = 4
TOKENS = BATCH * SEQ
EPS = 1e-6

SSM_D_INNER = 4096
SSM_HEAD_DIM = 64
SSM_HEADS = 64
SSM_GROUPS = 8
SSM_HEADS_PER_GROUP = SSM_HEADS // SSM_GROUPS
SSM_GROUP_WIDTH = SSM_D_INNER // SSM_GROUPS
SSM_STATE = 128
SSM_CONV = 4
SSM_CHUNK = 128
SSM_CONV_DIM = SSM_D_INNER + 2 * SSM_GROUPS * SSM_STATE

ATTN_HEADS = 16
ATTN_HEAD_DIM = 128
ATTN_KV_HEADS = 4
ATTN_WIDTH = ATTN_HEADS * ATTN_HEAD_DIM
ATTN_KV_WIDTH = ATTN_KV_HEADS * ATTN_HEAD_DIM
IDX_HEADS = 16
IDX_HEAD_DIM = 64
IDX_WIDTH = IDX_HEADS * IDX_HEAD_DIM
TOP_K = min(256, SEQ // 4)
Q_BLOCK = 128
ROPE_THETA = 500000.0
ATTN_ROT_DIM = ATTN_HEAD_DIM // 4
IDX_ROT_DIM = IDX_HEAD_DIM // 4
FFN_HIDDEN = 5632

COL_Z = 0
COL_XS = COL_Z + SSM_D_INNER
COL_B = COL_XS + SSM_D_INNER
COL_C = COL_B + SSM_GROUPS * SSM_STATE
PROJ_A_COLS = COL_C + SSM_GROUPS * SSM_STATE
COL_Q = 0
COL_K = COL_Q + ATTN_WIDTH
COL_V = COL_K + ATTN_KV_WIDTH
COL_QI = COL_V + ATTN_KV_WIDTH
COL_GA = COL_QI + IDX_WIDTH
COL_GB = COL_GA + D_MODEL
PROJ_B_COLS = COL_GB + D_MODEL
SMALL_COLS = 256
SMALL_DT = 128
SMALL_W = SMALL_DT + SSM_HEADS

LANES = 128
SUBLANES = 8
VMEM_LIMIT_BYTES = 56 * 1024 * 1024

INT_MIN = -(2 ** 31)


def _params(*sem):
    return pltpu.CompilerParams(dimension_semantics=sem, vmem_limit_bytes=VMEM_LIMIT_BYTES)


def _dot(a, b):
    return jnp.dot(a, b, preferred_element_type=F32)


def _sigmoid(x):
    return 1.0 / (1.0 + jnp.exp2(x * (-math.log2(math.e))))


def _rmsnorm_kernel(x_ref, g_ref, o_ref):
    x = x_ref[...]
    ms = jnp.mean(x * x, axis=-1, keepdims=True)
    o_ref[...] = ((x * lax.rsqrt(ms + EPS)) * g_ref[...]).astype(o_ref.dtype)


def rmsnorm(x, g, out_dtype, name, tm=512):
    m, d = x.shape
    return pl.pallas_call(
        _rmsnorm_kernel,
        grid=(m // tm,),
        in_specs=[pl.BlockSpec((tm, d), lambda i: (i, 0)),
                  pl.BlockSpec((1, d), lambda i: (0, 0))],
        out_specs=pl.BlockSpec((tm, d), lambda i: (i, 0)),
        out_shape=jax.ShapeDtypeStruct((m, d), out_dtype),
        compiler_params=_params("parallel"),
        name=name,
    )(x, g.reshape(1, d))


def _cast_on_first_m_step(pairs):
    @pl.when(pl.program_id(1) == 0)
    def _():
        for w_ref, wb_ref in pairs:
            wb_ref[...] = w_ref[...].astype(BF16)


def _weight_spec(k, tn, layer):
    return pl.BlockSpec((None, k, tn), lambda j, i: (layer, 0, j))


def _mm_kernel(a_ref, w_ref, o_ref):
    o_ref[...] = _dot(a_ref[...], w_ref[...]).astype(o_ref.dtype)


def matmul_bf16w(a, w, tm, tn, out_dtype, name):
    m, k = a.shape
    n = w.shape[1]
    return pl.pallas_call(
        _mm_kernel,
        grid=(n // tn, m // tm),
        in_specs=[pl.BlockSpec((tm, k), lambda j, i: (i, 0)),
                  pl.BlockSpec((k, tn), lambda j, i: (0, j))],
        out_specs=pl.BlockSpec((tm, tn), lambda j, i: (i, j)),
        out_shape=jax.ShapeDtypeStruct((m, n), out_dtype),
        compiler_params=_params("parallel", "parallel"),
        name=name,
    )(a, w)


def _mm_small_kernel(a_ref, w_ref, o_ref, ot_ref):
    acc = _dot(a_ref[...], w_ref[...])
    o_ref[...] = acc
    ot_ref[...] = acc[:, SMALL_DT:].T


def small_projection(a, w, tm, name):
    m, k = a.shape
    return pl.pallas_call(
        _mm_small_kernel,
        grid=(m // tm,),
        in_specs=[pl.BlockSpec((tm, k), lambda i: (i, 0)),
                  pl.BlockSpec((k, SMALL_COLS), lambda i: (0, 0))],
        out_specs=[pl.BlockSpec((tm, SMALL_COLS), lambda i: (i, 0)),
                   pl.BlockSpec((SMALL_COLS - SMALL_DT, tm), lambda i: (0, i))],
        out_shape=[jax.ShapeDtypeStruct((m, SMALL_COLS), F32),
                   jax.ShapeDtypeStruct((SMALL_COLS - SMALL_DT, m), F32)],
        compiler_params=_params("parallel"),
        name=name,
    )(a, w)


def _mm_residual_kernel(a_ref, w_ref, r_ref, o_ref, wb_ref):
    _cast_on_first_m_step([(w_ref, wb_ref)])
    o_ref[...] = r_ref[...] + _dot(a_ref[...], wb_ref[...])


def matmul_residual(a, w_stack, layer, r, tm, tn, name):
    m, k = a.shape
    n = w_stack.shape[2]
    return pl.pallas_call(
        _mm_residual_kernel,
        grid=(n // tn, m // tm),
        in_specs=[pl.BlockSpec((tm, k), lambda j, i: (i, 0)),
                  _weight_spec(k, tn, layer),
                  pl.BlockSpec((tm, tn), lambda j, i: (i, j))],
        out_specs=pl.BlockSpec((tm, tn), lambda j, i: (i, j)),
        out_shape=jax.ShapeDtypeStruct((m, n), F32),
        scratch_shapes=[pltpu.VMEM((k, tn), BF16)],
        compiler_params=_params("parallel", "arbitrary"),
        name=name,
    )(a, w_stack, r)


def _ffn_up_kernel(a_ref, wg_ref, wu_ref, o_ref, wgb_ref, wub_ref):
    _cast_on_first_m_step([(wg_ref, wgb_ref), (wu_ref, wub_ref)])
    a = a_ref[...]
    g = _dot(a, wgb_ref[...])
    u = _dot(a, wub_ref[...])
    o_ref[...] = ((g * _sigmoid(g)) * u).astype(o_ref.dtype)


def ffn_up(a, wg_stack, wu_stack, layer, tm, tn, name):
    m, k = a.shape
    n = wg_stack.shape[2]
    return pl.pallas_call(
        _ffn_up_kernel,
        grid=(n // tn, m // tm),
        in_specs=[pl.BlockSpec((tm, k), lambda j, i: (i, 0)),
                  _weight_spec(k, tn, layer),
                  _weight_spec(k, tn, layer)],
        out_specs=pl.BlockSpec((tm, tn), lambda j, i: (i, j)),
        out_shape=jax.ShapeDtypeStruct((m, n), BF16),
        scratch_shapes=[pltpu.VMEM((k, tn), BF16), pltpu.VMEM((k, tn), BF16)],
        compiler_params=_params("parallel", "arbitrary"),
        name=name,
    )(a, wg_stack, wu_stack)


def _merge_kernel(pa_ref, yb_ref, wb_ref, ga_ref, gb_ref, o_ref, wbb_ref):
    _cast_on_first_m_step([(wb_ref, wbb_ref)])
    pb = _dot(yb_ref[...], wbb_ref[...])
    o_ref[...] = (_sigmoid(ga_ref[...]) * pa_ref[...]
                  + _sigmoid(gb_ref[...]) * pb).astype(o_ref.dtype)


def merge_branches(pa, yb, wb_stack, layer, proj, tm, tn, name):
    m = yb.shape[0]
    kb = wb_stack.shape[1]
    n = wb_stack.shape[2]
    ga0 = COL_GA // tn
    gb0 = COL_GB // tn
    return pl.pallas_call(
        _merge_kernel,
        grid=(n // tn, m // tm),
        in_specs=[pl.BlockSpec((tm, tn), lambda j, i: (i, j)),
                  pl.BlockSpec((tm, kb), lambda j, i: (i, 0)),
                  _weight_spec(kb, tn, layer),
                  pl.BlockSpec((tm, tn), lambda j, i: (i, ga0 + j)),
                  pl.BlockSpec((tm, tn), lambda j, i: (i, gb0 + j))],
        out_specs=pl.BlockSpec((tm, tn), lambda j, i: (i, j)),
        out_shape=jax.ShapeDtypeStruct((m, n), BF16),
        scratch_shapes=[pltpu.VMEM((kb, tn), BF16)],
        compiler_params=_params("parallel", "arbitrary"),
        name=name,
    )(pa, yb, wb_stack, proj, proj)


SSD_STEP_CHUNKS = 4
SSD_STEP_ROWS = SSD_STEP_CHUNKS * SSM_CHUNK
SSD_CONV_WIDTH = SSM_GROUP_WIDTH + 2 * SSM_STATE


def _softplus(x):
    return jnp.maximum(x, 0.0) + jnp.log1p(jnp.exp(-jnp.abs(x)))


def _split3(x):
    hi = x.astype(BF16)
    r1 = x - hi.astype(F32)
    mid = r1.astype(BF16)
    lo = (r1 - mid.astype(F32)).astype(BF16)
    return hi, mid, lo


def _ssd_kernel(x_ref, xh_ref, b_ref, bh_ref, c_ref, ch_ref, cw_ref, cbias_ref, z_ref,
                dtr_ref, biasc_ref, alogc_ref, dsk_ref, ng_ref,
                u_ref, wb_ref, o_ref, pb_ref, st_ref, pad_ref, xc_ref, y_ref):
    def shadow_slice(k):
        rows_k = slice(k * SHADOW_SLICE_ROWS, (k + 1) * SHADOW_SLICE_ROWS)
        pb_ref[rows_k, :] = _dot(u_ref[rows_k, :], wb_ref[...])

    step = pl.program_id(2)

    @pl.when(step == 0)
    def _():
        st_ref[...] = jnp.zeros_like(st_ref)

    gw = SSM_GROUP_WIDTH
    n = SSM_STATE
    lc = SSM_CHUNK
    hp = SSM_HEAD_DIM
    halo = SUBLANES

    def halo_rows(h_ref):
        first = jnp.full(h_ref.shape, step, jnp.int32) == 0
        return jnp.where(first, 0.0, h_ref[...])

    pad_ref[0:halo, 0:gw] = halo_rows(xh_ref)
    pad_ref[0:halo, gw:gw + n] = halo_rows(bh_ref)
    pad_ref[0:halo, gw + n:] = halo_rows(ch_ref)
    pad_ref[halo:, 0:gw] = x_ref[...]
    pad_ref[halo:, gw:gw + n] = b_ref[...]
    pad_ref[halo:, gw + n:] = c_ref[...]
    cw = cw_ref[...]
    cbias = cbias_ref[...]
    for r0 in range(0, SSD_STEP_ROWS, lc):
        acc = cbias + cw[SSM_CONV - 1:SSM_CONV, :] * pad_ref[halo + r0:halo + r0 + lc, :]
        for back in range(1, SSM_CONV):
            lo = halo + r0 - back
            acc = acc + cw[SSM_CONV - 1 - back:SSM_CONV - back, :] * pad_ref[lo:lo + lc, :]
        xc_ref[r0:r0 + lc, :] = acc * _sigmoid(acc)

    log2e = math.log2(math.e)
    dtr_all = _softplus(dtr_ref[...] + biasc_ref[...])
    adt_r_all = dtr_all * (-log2e * jnp.exp(alogc_ref[...]))

    ii = lax.broadcasted_iota(jnp.int32, (lc, lc), 0)
    jj = lax.broadcasted_iota(jnp.int32, (lc, lc), 1)
    tri_t = jnp.where(ii <= jj, 1.0, 0.0).astype(BF16)
    pad_heads = jnp.zeros((lc - SSM_HEADS_PER_GROUP, lc), F32)

    for cc in range(SSD_STEP_CHUNKS):
        r0 = cc * lc
        cs_r = sum(_dot(p, tri_t) for p in _split3(adt_r_all[:, r0:r0 + lc]))
        cs_c = jnp.concatenate([cs_r, pad_heads], axis=0).T
        ecs_r = jnp.exp2(cs_r)
        ecl = ecs_r[:, lc - 1:lc]
        dtr = dtr_all[:, r0:r0 + lc]
        ddr = dtr * jnp.exp2(cs_r[:, lc - 1:lc] - cs_r)

        x_t = xc_ref[r0:r0 + lc, 0:gw].T
        b16 = xc_ref[r0:r0 + lc, gw:gw + n].astype(BF16)
        cc32 = xc_ref[r0:r0 + lc, gw + n:]
        c_t = cc32.T
        cb_t = lax.dot_general(b16, cc32.astype(BF16), (((1,), (1,)), ((), ())),
                               preferred_element_type=F32)
        for h in range(SSM_HEADS_PER_GROUP):
            if h % (SSM_HEADS_PER_GROUP // SHADOW_SLICES_PER_CHUNK) == 0:
                shadow_slice(cc * SHADOW_SLICES_PER_CHUNK
                             + h // (SSM_HEADS_PER_GROUP // SHADOW_SLICES_PER_CHUNK))
            lo = h * hp
            cs_s = jnp.broadcast_to(cs_c[:, h:h + 1], (lc, lc))
            decay_t = jnp.exp2(jnp.where(ii <= jj, cs_r[h:h + 1, :] - cs_s, -jnp.inf))
            intra_t = cb_t * decay_t
            inter_t = c_t * ecs_r[h:h + 1, :]
            prev = st_ref[h]
            x_h = x_t[lo:lo + hp, :]
            y_t = _dot(
                jnp.concatenate([x_h * dtr[h:h + 1, :], prev], axis=1).astype(BF16),
                jnp.concatenate([intra_t, inter_t], axis=0).astype(BF16))
            if h % 2 == 0:
                y_even = y_t
            else:
                y_ref[r0:r0 + lc, lo - hp:lo + hp] = jnp.concatenate([y_even, y_t], axis=0).T
            st_new = _dot((x_h * ddr[h:h + 1, :]).astype(BF16), b16)
            st_ref[h] = prev * ecl[h:h + 1, :] + st_new

    y = y_ref[...] + dsk_ref[...] * xc_ref[:, 0:gw]
    z = z_ref[...]
    y = y * (z * _sigmoid(z))
    ms = jnp.mean(y * y, axis=-1, keepdims=True)
    o_ref[...] = ((y * lax.rsqrt(ms + EPS)) * ng_ref[...]).astype(o_ref.dtype)


SHADOW_TM = 1024
SHADOW_SLICES_PER_CHUNK = 1
SHADOW_SLICE_ROWS = SHADOW_TM // (SSD_STEP_CHUNKS * SHADOW_SLICES_PER_CHUNK)
SHADOW_TN = PROJ_B_COLS * TOKENS // SHADOW_TM // (BATCH * SSM_GROUPS * (SEQ // SSD_STEP_ROWS))


def ssd_branch(proj, dt_row, conv_w, conv_b, dt_bias, a_log, d_skip, norm_g, u, w_b, name):
    rows = SSD_STEP_ROWS
    m_tiles = TOKENS // SHADOW_TM
    tile_i = lambda b, g, s: ((b * SSM_GROUPS + g) * (SEQ // rows) + s) % m_tiles
    tile_j = lambda b, g, s: ((b * SSM_GROUPS + g) * (SEQ // rows) + s) // m_tiles
    steps = SEQ // rows
    gw = SSM_GROUP_WIDTH
    n = SSM_STATE
    hg = SSM_HEADS_PER_GROUP
    ng = SSM_GROUPS
    cwid = SSD_CONV_WIDTH

    def per_group(p, lead):
        xs = p[:, :SSM_D_INNER].reshape(lead, ng, gw)
        bs = p[:, SSM_D_INNER:SSM_D_INNER + ng * n].reshape(lead, ng, n)
        cs = p[:, SSM_D_INNER + ng * n:].reshape(lead, ng, n)
        return jnp.transpose(jnp.concatenate([xs, bs, cs], axis=2), (1, 0, 2))

    row = lambda b, g, s: b * steps + s
    hrow = lambda b, g, s: jnp.maximum(row(b, g, s) * (rows // SUBLANES) - 1, 0)
    return pl.pallas_call(
        _ssd_kernel,
        grid=(BATCH, ng, steps),
        in_specs=[
            pl.BlockSpec((rows, gw), lambda b, g, s: (row(b, g, s), COL_XS // gw + g)),
            pl.BlockSpec((SUBLANES, gw), lambda b, g, s: (hrow(b, g, s), COL_XS // gw + g)),
            pl.BlockSpec((rows, n), lambda b, g, s: (row(b, g, s), COL_B // n + g)),
            pl.BlockSpec((SUBLANES, n), lambda b, g, s: (hrow(b, g, s), COL_B // n + g)),
            pl.BlockSpec((rows, n), lambda b, g, s: (row(b, g, s), COL_C // n + g)),
            pl.BlockSpec((SUBLANES, n), lambda b, g, s: (hrow(b, g, s), COL_C // n + g)),
            pl.BlockSpec((None, SSM_CONV, cwid), lambda b, g, s: (g, 0, 0)),
            pl.BlockSpec((None, 1, cwid), lambda b, g, s: (g, 0, 0)),
            pl.BlockSpec((rows, gw), lambda b, g, s: (row(b, g, s), COL_Z // gw + g)),
            pl.BlockSpec((hg, rows), lambda b, g, s: (g, row(b, g, s))),
            pl.BlockSpec((None, hg, 1), lambda b, g, s: (g, 0, 0)),
            pl.BlockSpec((None, hg, 1), lambda b, g, s: (g, 0, 0)),
            pl.BlockSpec((1, gw), lambda b, g, s: (0, g)),
            pl.BlockSpec((1, gw), lambda b, g, s: (0, g)),
            pl.BlockSpec((SHADOW_TM, D_MODEL), lambda b, g, s: (tile_i(b, g, s), 0)),
            pl.BlockSpec((D_MODEL, SHADOW_TN), lambda b, g, s: (0, tile_j(b, g, s))),
        ],
        out_specs=[pl.BlockSpec((rows, gw), lambda b, g, s: (row(b, g, s), g)),
                   pl.BlockSpec((SHADOW_TM, SHADOW_TN),
                                lambda b, g, s: (tile_i(b, g, s), tile_j(b, g, s)))],
        out_shape=[jax.ShapeDtypeStruct((TOKENS, SSM_D_INNER), BF16),
                   jax.ShapeDtypeStruct((TOKENS, PROJ_B_COLS), F32)],
        scratch_shapes=[pltpu.VMEM((hg, SSM_HEAD_DIM, n), F32),
                        pltpu.VMEM((rows + SUBLANES, cwid), F32),
                        pltpu.VMEM((rows, cwid), F32),
                        pltpu.VMEM((rows, gw), F32)],
        compiler_params=_params("parallel", "parallel", "arbitrary"),
        name=name,
    )(proj, proj, proj, proj, proj, proj,
      per_group(conv_w, SSM_CONV), per_group(conv_b.reshape(1, SSM_CONV_DIM), 1),
      proj, dt_row, dt_bias.reshape(ng, hg, 1), a_log.reshape(ng, hg, 1),
      jnp.repeat(d_skip, SSM_HEAD_DIM).reshape(1, SSM_D_INNER),
      norm_g.reshape(1, SSM_D_INNER), u, w_b)


KEY_CHUNK = 256
PREP_ROWS = KEY_CHUNK
PREP_QBLOCKS = PREP_ROWS // Q_BLOCK
N_KEY_CHUNKS = SEQ // KEY_CHUNK
HEAD_PAIRS = ATTN_HEADS // 2
IDX_PAIRS = IDX_HEADS // 2
Q_PRESCALE = ATTN_HEAD_DIM ** -0.5 * math.log2(math.e)
W_IDX_SCALE = IDX_HEADS ** -0.5 * IDX_HEAD_DIM ** -0.5
M_INIT = -1e30
PA_TILE = 512
PA_SLICE = 256
RADIX_CHUNK_COUNTS = (4, 6, 8)
assert RADIX_CHUNK_COUNTS[-1] == N_KEY_CHUNKS


def _rope(x, c, sn, sp, half):
    return x * c + pltpu.roll(x, LANES - half, 1) * sn + pltpu.roll(x, half, 1) * sp


def _prep_kernel(q_ref, k_ref, v_ref, qi_ref, smk_ref, smw_ref, ca_ref, sna_ref, spa_ref,
                 ci_ref, sni_ref, spi_ref, gk_ref,
                 qt_ref, ko_ref, vt_ref, qit_ref, kio_ref, wt_ref):
    ca, sna, spa = ca_ref[...], sna_ref[...], spa_ref[...]
    ci, sni, spi = ci_ref[...], sni_ref[...], spi_ref[...]
    ha = ATTN_ROT_DIM // 2
    hi = IDX_ROT_DIM // 2
    hd = ATTN_HEAD_DIM
    zeros_half = jnp.zeros((LANES - IDX_HEAD_DIM, Q_BLOCK), F32)
    for qq in range(PREP_QBLOCKS):
        rows = slice(qq * Q_BLOCK, (qq + 1) * Q_BLOCK)
        for h in range(ATTN_HEADS):
            qr = _rope(q_ref[rows, h * hd:(h + 1) * hd], ca[rows], sna[rows], spa[rows], ha)
            qt_ref[qq, h // 2, :, (h % 2) * Q_BLOCK:(h % 2 + 1) * Q_BLOCK] = (
                (qr * Q_PRESCALE).T.astype(BF16))
        for j in range(IDX_PAIRS):
            xt = _rope(qi_ref[rows, j * LANES:(j + 1) * LANES],
                       ci[rows], sni[rows], spi[rows], hi).T
            even = jnp.concatenate([xt[:IDX_HEAD_DIM], zeros_half], axis=0)
            odd = jnp.concatenate([xt[IDX_HEAD_DIM:], zeros_half], axis=0)
            qit_ref[qq, j, :, 0:Q_BLOCK] = even.astype(BF16)
            qit_ref[qq, j, :, Q_BLOCK:] = odd.astype(BF16)
    for g in range(ATTN_KV_HEADS):
        cols = slice(g * hd, (g + 1) * hd)
        ko_ref[:, cols] = _rope(k_ref[:, cols], ca, sna, spa, ha).astype(BF16)
        vt_ref[g, 0] = v_ref[:, cols].T.astype(BF16)
    xk = smk_ref[...]
    ms = jnp.sum(xk * xk, axis=-1, keepdims=True) * (1.0 / IDX_HEAD_DIM)
    xk = (xk * lax.rsqrt(ms + EPS)) * gk_ref[...]
    kio_ref[...] = _rope(xk, ci, sni, spi, hi).astype(BF16)
    smt = smw_ref[...].T
    w0 = SMALL_W - SMALL_DT
    for qq in range(PREP_QBLOCKS):
        wt_ref[qq] = smt[w0:w0 + IDX_HEADS, qq * Q_BLOCK:(qq + 1) * Q_BLOCK] * W_IDX_SCALE


def dsa_prep(proj, small, tabs, idx_k_norm, name):
    r = PREP_ROWS
    nqb = PREP_QBLOCKS
    gk = jnp.concatenate([idx_k_norm, jnp.zeros((LANES - IDX_HEAD_DIM,), F32)]).reshape(1, LANES)
    tab_spec = pl.BlockSpec((r, LANES), lambda i: (i, 0))
    return pl.pallas_call(
        _prep_kernel,
        grid=(TOKENS // r,),
        in_specs=[
            pl.BlockSpec((r, ATTN_WIDTH), lambda i: (i, COL_Q // ATTN_WIDTH)),
            pl.BlockSpec((r, ATTN_KV_WIDTH), lambda i: (i, COL_K // ATTN_KV_WIDTH)),
            pl.BlockSpec((r, ATTN_KV_WIDTH), lambda i: (i, COL_V // ATTN_KV_WIDTH)),
            pl.BlockSpec((r, IDX_WIDTH), lambda i: (i, COL_QI // IDX_WIDTH)),
            pl.BlockSpec((r, LANES), lambda i: (i, 0)),
            pl.BlockSpec((r, LANES), lambda i: (i, SMALL_DT // LANES)),
            tab_spec, tab_spec, tab_spec, tab_spec, tab_spec, tab_spec,
            pl.BlockSpec((1, LANES), lambda i: (0, 0)),
        ],
        out_specs=[
            pl.BlockSpec((nqb, HEAD_PAIRS, ATTN_HEAD_DIM, 2 * Q_BLOCK), lambda i: (i, 0, 0, 0)),
            pl.BlockSpec((r, ATTN_KV_WIDTH), lambda i: (i, 0)),
            pl.BlockSpec((ATTN_KV_HEADS, 1, ATTN_HEAD_DIM, r), lambda i: (0, i, 0, 0)),
            pl.BlockSpec((nqb, IDX_PAIRS, LANES, 2 * Q_BLOCK), lambda i: (i, 0, 0, 0)),
            pl.BlockSpec((r, LANES), lambda i: (i, 0)),
            pl.BlockSpec((nqb, IDX_HEADS, Q_BLOCK), lambda i: (i, 0, 0)),
        ],
        out_shape=[
            jax.ShapeDtypeStruct((TOKENS // Q_BLOCK, HEAD_PAIRS, ATTN_HEAD_DIM, 2 * Q_BLOCK), BF16),
            jax.ShapeDtypeStruct((TOKENS, ATTN_KV_WIDTH), BF16),
            jax.ShapeDtypeStruct((ATTN_KV_HEADS, TOKENS // r, ATTN_HEAD_DIM, r), BF16),
            jax.ShapeDtypeStruct((TOKENS // Q_BLOCK, IDX_PAIRS, LANES, 2 * Q_BLOCK), BF16),
            jax.ShapeDtypeStruct((TOKENS, LANES), BF16),
            jax.ShapeDtypeStruct((TOKENS // Q_BLOCK, IDX_HEADS, Q_BLOCK), F32),
        ],
        compiler_params=_params("parallel"),
        name=name,
    )(proj, proj, proj, proj, small, small, *tabs, gk)


def _col_reduce8(x, op):
    rows, lanes = x.shape
    g = x.reshape(4, rows // (4 * SUBLANES), SUBLANES, lanes)
    p = op(g, axis=1)
    red = jnp.maximum if op is jnp.max else jnp.add
    return red(red(p[0], p[1]), red(p[2], p[3]))


def _dsa_kernel(qt_ref, k_ref, vt_ref, qit_ref, ki_ref, wt_ref, ya_ref, wpa_ref, o_ref, pa_ref,
                key_ref, madd_ref, acc_ref, m_ref, l_ref, alpha_ref, s_ref, pe_ref, thr_ref,
                wpa16_ref):
    tq = Q_BLOCK
    kc = KEY_CHUNK
    hd = ATTN_HEAD_DIM
    qb = pl.program_id(1)
    n_chunks = qb // (kc // tq) + 1
    neg_inf = -jnp.inf
    q_pos = qb * tq + lax.broadcasted_iota(jnp.int32, (kc, tq), 1)
    k_off = lax.broadcasted_iota(jnp.int32, (kc, tq), 0)

    def causal(c):
        return c * kc + k_off <= q_pos

    def key_rows(c):
        return pl.ds(pl.multiple_of(c * kc, kc), kc)

    wt = wt_ref[...]

    def score_chunk(c, carry):
        ki = ki_ref[key_rows(c), :]
        score = jnp.zeros((kc, tq), F32)
        for j in range(IDX_PAIRS):
            lt = _dot(ki, qit_ref[j])
            score = score + jnp.maximum(lt[:, :tq], 0.0) * wt[2 * j:2 * j + 1, :]
            score = score + jnp.maximum(lt[:, tq:], 0.0) * wt[2 * j + 1:2 * j + 2, :]
        score = jnp.where(score == 0.0, 0.0, score)
        bits = lax.bitcast_convert_type(jnp.where(causal(c), score, neg_inf), jnp.int32)
        key_ref[c] = jnp.where(bits >= 0, bits, bits ^ jnp.int32(0x7FFFFFFF))
        return carry

    lax.fori_loop(0, n_chunks, score_chunk, 0)

    pa_parts = PA_TILE // PA_SLICE

    def pa_slice(k):
        rows_k = slice((k % pa_parts) * PA_SLICE, (k % pa_parts + 1) * PA_SLICE)
        cols_k = slice((k // pa_parts) * PA_SLICE, (k // pa_parts + 1) * PA_SLICE)
        pa_ref[rows_k, cols_k] = _dot(ya_ref[rows_k, :], wpa16_ref[:, cols_k])

    @pl.when(qb == 0)
    def _():
        wpa16_ref[...] = wpa_ref[...].astype(BF16)

    def radix_select(chunks_used):
        def pad_chunk(c, carry):
            key_ref[c] = jnp.full((kc, tq), INT_MIN, jnp.int32)
            return carry

        lax.fori_loop(n_chunks, chunks_used, pad_chunk, 0)
        t = jnp.zeros((1, tq), jnp.int32)
        for i in range(32):
            cand = t | lax.shift_left(jnp.int32(1), 31 - i)
            cand_key = cand ^ jnp.int32(INT_MIN)
            cnt8 = jnp.zeros((SUBLANES, tq), F32)
            for c in range(chunks_used):
                hit = jnp.where(key_ref[c] >= cand_key, 1.0, 0.0)
                cnt8 = cnt8 + _col_reduce8(hit, jnp.sum)
            cnt = jnp.sum(cnt8, axis=0, keepdims=True)
            t = jnp.where(cnt >= float(TOP_K), cand, t)
            if i % (32 // pa_parts ** 2) == 0:
                pa_slice(i // (32 // pa_parts ** 2))
        thr_ref[...] = t ^ jnp.int32(INT_MIN)

    lo_chunks = 0
    for hi_chunks in RADIX_CHUNK_COUNTS:
        @pl.when((n_chunks > lo_chunks) & (n_chunks <= hi_chunks))
        def _(hi_chunks=hi_chunks):
            radix_select(hi_chunks)

        lo_chunks = hi_chunks

    thr = thr_ref[...]

    def mask_chunk(c, cnt):
        ge = key_ref[c] >= thr
        ok = causal(c)
        madd_ref[c] = jnp.where(ok, jnp.where(ge, 0.0, neg_inf), neg_inf)
        return cnt + jnp.sum(jnp.where(ok, jnp.where(ge, 1.0, 0.0), 0.0), axis=0, keepdims=True)

    cnt_ge = lax.fori_loop(0, n_chunks, mask_chunk, jnp.zeros((1, tq), F32))

    @pl.when(jnp.max(cnt_ge) > float(TOP_K))
    def _():
        def count_gt(c, cnt):
            return cnt + jnp.sum(jnp.where(key_ref[c] > thr, 1.0, 0.0), axis=0, keepdims=True)

        need = float(TOP_K) - lax.fori_loop(0, n_chunks, count_gt, jnp.zeros((1, tq), F32))
        a_i = lax.broadcasted_iota(jnp.int32, (kc, kc), 0)
        a_j = lax.broadcasted_iota(jnp.int32, (kc, kc), 1)
        earlier = jnp.where(a_j < a_i, 1.0, 0.0).astype(BF16)

        def tie_chunk(c, run):
            key = key_ref[c]
            gt = key > thr
            eq = key == thr
            e = jnp.where(eq, 1.0, 0.0)
            rank = _dot(earlier, e.astype(BF16)) + run
            keep_eq = jnp.where(rank < need, 0.0, neg_inf)
            val = jnp.where(gt, 0.0, jnp.where(eq, keep_eq, neg_inf))
            madd_ref[c] = jnp.where(causal(c), val, neg_inf)
            return run + jnp.sum(e, axis=0, keepdims=True)

        lax.fori_loop(0, n_chunks, tie_chunk, jnp.zeros((1, tq), F32))

    m_ref[...] = jnp.full(m_ref.shape, M_INIT, F32)
    l_ref[...] = jnp.zeros(l_ref.shape, F32)
    acc_ref[...] = jnp.zeros(acc_ref.shape, F32)

    def attn_chunk(c, carry):
        half_rows = kc // 2
        for p in range(HEAD_PAIRS):
            g = p // 2
            top8 = None
            for rh in range(2):
                r0 = pl.multiple_of(c * kc + rh * half_rows, half_rows)
                sub = slice(rh * half_rows, (rh + 1) * half_rows)
                s = _dot(k_ref[pl.ds(r0, half_rows), g * hd:(g + 1) * hd], qt_ref[p])
                madd = madd_ref[c, sub, :]
                s = s + jnp.concatenate([madd, madd], axis=1)
                s_ref[p, sub, :] = s
                t8 = _col_reduce8(s, jnp.max)
                top8 = t8 if top8 is None else jnp.maximum(top8, t8)
            m_old = m_ref[p]
            m_new = jnp.maximum(m_old, jnp.max(top8, axis=0, keepdims=True))
            m_ref[p] = m_new
            alpha_ref[p] = jnp.exp2(m_old - m_new)
        for p in range(HEAD_PAIRS):
            sum8 = None
            for rh in range(2):
                sub = slice(rh * half_rows, (rh + 1) * half_rows)
                pe = jnp.exp2(s_ref[p, sub, :] - m_ref[p])
                pe_ref[p, sub, :] = pe.astype(BF16)
                s8 = _col_reduce8(pe, jnp.sum)
                sum8 = s8 if sum8 is None else sum8 + s8
            l_ref[p] = l_ref[p] * alpha_ref[p] + jnp.sum(sum8, axis=0, keepdims=True)
        for p in range(HEAD_PAIRS):
            acc_ref[p] = acc_ref[p] * alpha_ref[p] + _dot(vt_ref[p // 2, c], pe_ref[p])
        return carry

    lax.fori_loop(0, n_chunks, attn_chunk, 0)

    for p in range(HEAD_PAIRS):
        o = acc_ref[p] / l_ref[p]
        for half in range(2):
            h = 2 * p + half
            o_ref[:, h * hd:(h + 1) * hd] = o[:, half * tq:(half + 1) * tq].T.astype(o_ref.dtype)


def dsa_attention(q_t, k_r, v_t, qi_t, ki, w_t, y_a, w_pa_stack, layer, name):
    nq = SEQ // Q_BLOCK
    row = lambda b, i: b * nq + i
    pa_m = TOKENS // PA_TILE
    assert pa_m == nq and D_MODEL // PA_TILE == BATCH
    return pl.pallas_call(
        _dsa_kernel,
        grid=(BATCH, nq),
        in_specs=[
            pl.BlockSpec((None, HEAD_PAIRS, ATTN_HEAD_DIM, 2 * Q_BLOCK),
                         lambda b, i: (row(b, i), 0, 0, 0)),
            pl.BlockSpec((SEQ, ATTN_KV_WIDTH), lambda b, i: (b, 0)),
            pl.BlockSpec((ATTN_KV_HEADS, N_KEY_CHUNKS, ATTN_HEAD_DIM, KEY_CHUNK),
                         lambda b, i: (0, b, 0, 0)),
            pl.BlockSpec((None, IDX_PAIRS, LANES, 2 * Q_BLOCK), lambda b, i: (row(b, i), 0, 0, 0)),
            pl.BlockSpec((SEQ, LANES), lambda b, i: (b, 0)),
            pl.BlockSpec((None, IDX_HEADS, Q_BLOCK), lambda b, i: (row(b, i), 0, 0)),
            pl.BlockSpec((PA_TILE, SSM_D_INNER), lambda b, i: (row(b, i) % pa_m, 0)),
            pl.BlockSpec((None, SSM_D_INNER, PA_TILE), lambda b, i: (layer, 0, b)),
        ],
        out_specs=[pl.BlockSpec((Q_BLOCK, ATTN_WIDTH), lambda b, i: (row(b, i), 0)),
                   pl.BlockSpec((PA_TILE, PA_TILE),
                                lambda b, i: (row(b, i) % pa_m, row(b, i) // pa_m))],
        out_shape=[jax.ShapeDtypeStruct((TOKENS, ATTN_WIDTH), BF16),
                   jax.ShapeDtypeStruct((TOKENS, D_MODEL), F32)],
        scratch_shapes=[pltpu.VMEM((N_KEY_CHUNKS, KEY_CHUNK, Q_BLOCK), jnp.int32),
                        pltpu.VMEM((N_KEY_CHUNKS, KEY_CHUNK, Q_BLOCK), F32),
                        pltpu.VMEM((HEAD_PAIRS, ATTN_HEAD_DIM, 2 * Q_BLOCK), F32),
                        pltpu.VMEM((HEAD_PAIRS, 1, 2 * Q_BLOCK), F32),
                        pltpu.VMEM((HEAD_PAIRS, 1, 2 * Q_BLOCK), F32),
                        pltpu.VMEM((HEAD_PAIRS, 1, 2 * Q_BLOCK), F32),
                        pltpu.VMEM((HEAD_PAIRS, KEY_CHUNK, 2 * Q_BLOCK), F32),
                        pltpu.VMEM((HEAD_PAIRS, KEY_CHUNK, 2 * Q_BLOCK), BF16),
                        pltpu.VMEM((1, Q_BLOCK), jnp.int32),
                        pltpu.VMEM((SSM_D_INNER, PA_TILE), BF16)],
        compiler_params=_params("parallel", "arbitrary"),
        name=name,
    )(q_t, k_r, v_t, qi_t, ki, w_t, y_a, w_pa_stack)


def _rope_lane_tables(positions):
    pos = positions.astype(F32).reshape(TOKENS, 1)

    def tables(rot_dim, width):
        inv_freq = ROPE_THETA ** (-(jnp.arange(0, rot_dim, 2, dtype=F32) / rot_dim))
        ang = pos * inv_freq
        cos, sin = jnp.cos(ang), jnp.sin(ang)
        half = rot_dim // 2
        ones = jnp.ones((TOKENS, width - rot_dim), F32)
        zeros = lambda n: jnp.zeros((TOKENS, n), F32)
        c = jnp.concatenate([cos, cos, ones], axis=1)
        sn = jnp.concatenate([-sin, zeros(width - half)], axis=1)
        sp = jnp.concatenate([zeros(half), sin, zeros(width - rot_dim)], axis=1)
        reps = LANES // width
        return tuple(jnp.tile(t, (1, reps)) for t in (c, sn, sp))

    return tables(ATTN_ROT_DIM, ATTN_HEAD_DIM) + tables(IDX_ROT_DIM, IDX_HEAD_DIM)


def _regroup_w_in(w):
    sizes = (SSM_HEADS, ATTN_WIDTH, ATTN_KV_WIDTH, ATTN_KV_WIDTH, IDX_WIDTH, IDX_HEAD_DIM,
             IDX_HEADS, D_MODEL, D_MODEL)
    parts = []
    off = PROJ_A_COLS
    for s in sizes:
        parts.append(w[:, off:off + s])
        off += s
    _, q, k, v, qi, _, _, ga, gb = parts
    dt = w[:, PROJ_A_COLS:PROJ_A_COLS + LANES][:, :SSM_HEADS]
    ki0 = PROJ_A_COLS + SSM_HEADS + ATTN_WIDTH + 2 * ATTN_KV_WIDTH + IDX_WIDTH
    win0 = ki0 // LANES * LANES
    win = w[:, win0:win0 + 2 * LANES]
    ki = win[:, ki0 - win0:ki0 - win0 + IDX_HEAD_DIM]
    wi = win[:, ki0 - win0 + IDX_HEAD_DIM:ki0 - win0 + IDX_HEAD_DIM + IDX_HEADS]
    w_a = w[:, :PROJ_A_COLS].astype(BF16)
    w_b = jnp.concatenate([q, k, v, qi, ga, gb], axis=1).astype(BF16)
    zc = lambda n: jnp.zeros((D_MODEL, n), F32)
    small = jnp.concatenate(
        [ki, zc(LANES - IDX_HEAD_DIM), dt, wi, zc(SMALL_COLS - SMALL_W - IDX_HEADS)],
        axis=1).astype(BF16)
    return w_a, w_b, small


def kernel(x, positions, mix_norm, w_in, conv_w, conv_b, dt_bias, a_log, d_skip, ssm_norm,
           idx_k_norm, w_proj_a, w_proj_b, w_out, ffn_norm, w_ffn_gate, w_ffn_up, w_ffn_down,
           final_norm):
    tabs = _rope_lane_tables(positions)
    xf = x.reshape(TOKENS, D_MODEL)
    for i in range(DEPTH):
        w_a, w_b, w_small = _regroup_w_in(w_in[i])
        u = rmsnorm(xf, mix_norm[i], BF16, f"mix_norm_{i}")
        proj_a = matmul_bf16w(u, w_a, 1024, 1024, F32, f"in_proj_a_{i}")
        small, small_t = small_projection(u, w_small, 1024, f"in_proj_small_{i}")

        y_a, proj_b = ssd_branch(proj_a, small_t, conv_w[i], conv_b[i], dt_bias[i], a_log[i],
                                 d_skip[i], ssm_norm[i], u, w_b, f"ssd_{i}")

        q_t, k_r, v_t, qi_t, ki, w_t = dsa_prep(proj_b, small, tabs, idx_k_norm[i],
                                                f"dsa_prep_{i}")
        y_b, pa = dsa_attention(q_t, k_r, v_t, qi_t, ki, w_t, y_a, w_proj_a, i, f"dsa_{i}")

        merged = merge_branches(pa, y_b, w_proj_b, i, proj_b, 512, 1024, f"merge_{i}")
        xf = matmul_residual(merged, w_out, i, xf, 1024, 1024, f"out_proj_{i}")

        h = rmsnorm(xf, ffn_norm[i], BF16, f"ffn_norm_{i}")
        ff = ffn_up(h, w_ffn_gate, w_ffn_up, i, 1024, 512, f"ffn_up_{i}")
        xf = matmul_residual(ff, w_ffn_down, i, xf, 512, 512, f"ffn_down_{i}")
    out = rmsnorm(xf, final_norm, F32, "final_norm")
    return out.reshape(BATCH, SEQ, D_MODEL)
```
